```python
import math
import jax, jax.numpy as jnp
from jax import lax
import numpy as np

D_MODEL = 1024
BATCH = 2
SEQ = 8192
DEPTH = 4
DEC_BATCH = 32
DEC_SEQ = 1
PAST_LEN = 8192
PAGE_SIZE = 128

DN_HEADS = 4
DN_HEAD_DIM = 128
DN_WIDTH = DN_HEADS * DN_HEAD_DIM
CONV_WIDTH = 4
DN_CHUNK = 64
ATT_HEADS = 4
ATT_HEAD_DIM = 128
ATT_WIDTH = ATT_HEADS * ATT_HEAD_DIM
IDX_HEADS = 8
IDX_DIM = 64
TOPK_MAX = 256
Q_BLOCK = 128
D_FF = 2816
N_EXPERTS = 8
MOE_TOP_K = 2
D_FF_EXPERT = 2816
MOE_BLOCK = 128
N_DENSE = (DEPTH + 1) // 2
N_MOE = DEPTH // 2
DEEPNORM_ALPHA = (2.0 * DEPTH) ** 0.25
DEEPNORM_BETA = (8.0 * DEPTH) ** -0.25
NORM_EPS = 1e-5
IN_SPLITS = (3 * DN_WIDTH, DN_WIDTH, DN_HEADS, DN_HEADS, ATT_WIDTH, ATT_WIDTH, ATT_WIDTH, IDX_HEADS * IDX_DIM, IDX_DIM, IDX_HEADS, 2 * D_MODEL)
N_IN = sum(IN_SPLITS)
F32 = jnp.float32

kernel_name = 'hybrid_deltanet_dsa_alibi_deepnorm_decode_step'


def _layernorm(x, g, b):
    xf = x.astype(F32)
    xc = xf - jnp.mean(xf, -1, keepdims=True)
    var = jnp.mean(xc * xc, -1, keepdims=True)
    return (xc * lax.rsqrt(var + NORM_EPS) * g.astype(F32) + b.astype(F32)).astype(x.dtype)


def _rmsnorm(x, g):
    xf = x.astype(F32)
    return xf * lax.rsqrt(jnp.mean(xf * xf, -1, keepdims=True) + NORM_EPS) * g.astype(F32)


def _l2norm(x):
    xf = x.astype(F32)
    return xf * lax.rsqrt(jnp.sum(xf * xf, -1, keepdims=True) + 1e-6)


def _swiglu(x, w_gate, w_up, w_down):
    return (jax.nn.silu(x @ w_gate) * (x @ w_up)) @ w_down


def _alibi_slopes():
    return jnp.exp2(-8.0 * jnp.arange(1, ATT_HEADS + 1, dtype=F32) / ATT_HEADS)


def _causal_conv(x, buf, w):
    T = x.shape[1]
    xx = jnp.concatenate([buf.astype(x.dtype), x], axis=1)
    y = xx[:, 0:T] * w[0]
    for i in range(1, CONV_WIDTH):
        y = y + xx[:, i:i + T] * w[i]
    return jax.nn.silu(y), xx[:, T:]


def _gated_delta(q, k, v, g, beta, s0):
    B, T, H, dk = q.shape
    dv = v.shape[-1]
    C = min(DN_CHUNK, T)
    n = -(-T // C)
    pad = n * C - T

    def chunks(a):
        a = a.astype(F32)
        a = jnp.pad(a, [(0, 0), (0, pad)] + [(0, 0)] * (a.ndim - 2))
        a = a.reshape((B, n, C) + a.shape[2:])
        return jnp.swapaxes(jnp.moveaxis(a, 1, 0), 2, 3)

    qc, kc, vc, gc, bc = (chunks(t) for t in (q * dk ** -0.5, k, v, g, beta))
    ar = jnp.arange(C)
    incl = ar[:, None] >= ar[None, :]
    strict = ar[:, None] > ar[None, :]
    eye = jnp.eye(C, dtype=F32)

    def step(S, inp):
        qi, ki, vi, gi, bi = inp
        G = jnp.cumsum(gi, axis=-1)
        decay = jnp.exp(jnp.where(incl, G[..., :, None] - G[..., None, :], -jnp.inf))
        a_mat = jnp.where(strict, bi[..., :, None] * jnp.einsum('bhid,bhjd->bhij', ki, ki) * decay, 0.0)
        rhs = jnp.concatenate([vi * bi[..., None], ki * (bi * jnp.exp(G))[..., None]], axis=-1)
        sol = lax.linalg.triangular_solve(a_mat + eye, rhs, left_side=True, lower=True, unit_diagonal=True)
        u, w = sol[..., :dv], sol[..., dv:]
        v_new = u - jnp.einsum('bhck,bhkv->bhcv', w, S)
        o = (jnp.einsum('bhck,bhkv->bhcv', qi * jnp.exp(G)[..., None], S)
             + jnp.einsum('bhij,bhjv->bhiv', jnp.einsum('bhid,bhjd->bhij', qi, ki) * decay, v_new))
        g_last = G[..., -1:]
        S = S * jnp.exp(g_last)[..., None] + jnp.einsum('bhck,bhcv->bhkv', ki * jnp.exp(g_last - G)[..., None], v_new)
        return S, o

    s_fin, o = lax.scan(step, s0.astype(F32), (qc, kc, vc, gc, bc))
    o = jnp.moveaxis(jnp.swapaxes(o, 2, 3), 0, 1).reshape(B, n * C, H, dv)[:, :T]
    return o, s_fin.astype(s0.dtype)


def _indexer_scores(iq, iw, ik, q_pos, k_pos):
    dots = jnp.einsum('bqhd,bld->bqhl', iq, ik, preferred_element_type=F32) * IDX_DIM ** -0.5
    s = jnp.einsum('bqhl,bqh->bql', jax.nn.relu(dots), iw.astype(F32))
    return jnp.where(k_pos[None, None, :] <= q_pos[None, :, None], s, -jnp.inf)


def _select_and_attend(q, q_pos, scores, gather_kv, n_sel):
    _, sel = lax.top_k(scores, n_sel)
    k_sel, v_sel = gather_kv(sel)
    dist = (q_pos[None, :, None] - sel).astype(F32)
    logits = (jnp.einsum('bqhd,bqnhd->bqhn', q, k_sel, preferred_element_type=F32) * ATT_HEAD_DIM ** -0.5
              - _alibi_slopes()[None, None, :, None] * dist[:, :, None, :])
    logits = jnp.where(dist[:, :, None, :] >= 0, logits, -jnp.inf)
    p = jax.nn.softmax(logits, axis=-1)
    return jnp.einsum('bqhn,bqnhd->bqhd', p.astype(v_sel.dtype), v_sel)


def _prompt_sparse_attention(q, k, v, iq, ik, iw):
    B, S, H, Dh = q.shape
    qb = min(Q_BLOCK, S)
    nb = S // qb
    n_sel = min(TOPK_MAX, S // 4)
    pos = jnp.arange(S, dtype=jnp.int32)
    take = jax.vmap(lambda rows, idx: rows[idx])

    def blocks(a):
        return jnp.swapaxes(a.reshape((B, nb, qb) + a.shape[2:]), 0, 1)

    def gather_kv(sel):
        return take(k, sel), take(v, sel)

    def one_block(args):
        q_blk, iq_blk, iw_blk, pos_blk = args
        scores = _indexer_scores(iq_blk, iw_blk, ik, pos_blk, pos)
        return _select_and_attend(q_blk, pos_blk, scores, gather_kv, n_sel)

    out = lax.map(one_block, (blocks(q), blocks(iq), blocks(iw), pos.reshape(nb, qb)))
    return jnp.swapaxes(out, 0, 1).reshape(B, S, H, Dh)


def _sample_sparse_attention(q, k_new, v_new, iq, ik_new, iw, cache_k, cache_v, cache_idx_k, page_table, layer):
    DB, T = q.shape[:2]
    past = page_table.shape[1] * PAGE_SIZE
    n_sel = min(TOPK_MAX, (past + T) // 4)
    ik_past = cache_idx_k[layer, page_table].reshape(DB, past, IDX_DIM).astype(ik_new.dtype)
    ik_all = jnp.concatenate([ik_past, ik_new], axis=1)
    q_pos = past + jnp.arange(T, dtype=jnp.int32)
    k_pos = jnp.arange(past + T, dtype=jnp.int32)
    scores = _indexer_scores(iq, iw, ik_all, q_pos, k_pos)
    take = jax.vmap(lambda rows, idx: rows[idx])

    def gather_kv(sel):
        is_past = (sel < past)[..., None, None]
        s_past = jnp.minimum(sel, past - 1)
        phys = take(page_table, s_past // PAGE_SIZE)
        off = s_past % PAGE_SIZE
        s_cur = jnp.clip(sel - past, 0, T - 1)
        k_sel = jnp.where(is_past, cache_k[layer, phys, off].astype(k_new.dtype), take(k_new, s_cur))
        v_sel = jnp.where(is_past, cache_v[layer, phys, off].astype(v_new.dtype), take(v_new, s_cur))
        return k_sel, v_sel

    return _select_and_attend(q, q_pos, scores, gather_kv, n_sel)


def _moe_swiglu(h, w_router, w_gate, w_up, w_down):
    shape = h.shape
    x = h.reshape(-1, shape[-1])
    n = x.shape[0]
    probs = jax.nn.softmax(jnp.einsum('nd,de->ne', x, w_router, preferred_element_type=F32), axis=-1)
    top_p, top_e = lax.top_k(probs, MOE_TOP_K)
    gates = top_p / jnp.sum(top_p, -1, keepdims=True)
    blk = min(MOE_BLOCK, n)
    n_assign = n * MOE_TOP_K
    flat_e = top_e.reshape(-1).astype(jnp.int32)
    order = jnp.argsort(flat_e).astype(jnp.int32)
    sorted_e = flat_e[order]
    counts = jnp.zeros((N_EXPERTS,), jnp.int32).at[flat_e].add(1)
    padded = (counts + blk - 1) // blk * blk
    pad_end = jnp.cumsum(padded)
    pad_start = pad_end - padded
    start = jnp.cumsum(counts) - counts
    dest = pad_start[sorted_e] + jnp.arange(n_assign, dtype=jnp.int32) - start[sorted_e]
    n_blocks = -(-n_assign // blk) + N_EXPERTS
    slot_token = jnp.full((n_blocks * blk,), n, jnp.int32).at[dest].set(order // MOE_TOP_K)
    block_expert = jnp.minimum(jnp.searchsorted(pad_end, jnp.arange(n_blocks, dtype=jnp.int32) * blk, side='right'), N_EXPERTS - 1)
    x_pad = jnp.concatenate([x, jnp.zeros((1, x.shape[1]), x.dtype)], axis=0)
    xb = x_pad[slot_token].reshape(n_blocks, blk, -1)
    yb = lax.map(lambda a: _swiglu(a[0], w_gate[a[1]], w_up[a[1]], w_down[a[1]]), (xb, block_expert))
    y_sorted = yb.reshape(n_blocks * blk, -1)[dest]
    y_assign = jnp.zeros_like(y_sorted).at[order].set(y_sorted)
    y = jnp.einsum('nkd,nk->nd', y_assign.reshape(n, MOE_TOP_K, -1), gates.astype(y_assign.dtype))
    return y.reshape(shape)


def _token_mixer(h, conv_buf, s0, attend, w_in, conv_w, a_log, dt_bias, dn_norm_g, w_branch, w_out):
    B, T, _ = h.shape
    offsets = [int(o) for o in np.cumsum(IN_SPLITS)[:-1]]
    qkv_a, z, b_raw, a_raw, q_b, k_b, v_b, iq, ik, iw, gates = jnp.split(h @ w_in, offsets, axis=-1)

    def heads(t, nh):
        return t.reshape(B, T, nh, -1)

    qkv_a, new_buf = _causal_conv(qkv_a, conv_buf, conv_w)
    qa, ka, va = (heads(t, DN_HEADS) for t in jnp.split(qkv_a, 3, axis=-1))
    beta = jax.nn.sigmoid(b_raw.astype(F32))
    g = -jnp.exp(a_log.astype(F32)) * jax.nn.softplus(a_raw.astype(F32) + dt_bias.astype(F32))
    o_a, s_new = _gated_delta(_l2norm(qa), _l2norm(ka), va, g, beta, s0)
    o_a = _rmsnorm(o_a, dn_norm_g) * jax.nn.silu(heads(z, DN_HEADS).astype(F32))
    br_a = o_a.astype(h.dtype).reshape(B, T, DN_WIDTH) @ w_branch[0]
    kb = heads(k_b, ATT_HEADS)
    vb = heads(v_b, ATT_HEADS)
    o_b = attend(heads(q_b, ATT_HEADS), kb, vb, heads(iq, IDX_HEADS), ik, iw * IDX_HEADS ** -0.5)
    br_b = o_b.reshape(B, T, ATT_WIDTH) @ w_branch[1]
    g_a, g_b = jnp.split(gates, 2, axis=-1)
    merged = jax.nn.sigmoid(g_a) * br_a + jax.nn.sigmoid(g_b) * br_b
    return merged @ w_out, new_buf, s_new, kb, vb, ik


def _layer(x, c, conv_buf, s0, attend, ffn, w_ada, b_ada, w_in, conv_w, a_log, dt_bias, dn_norm_g, w_branch, w_out,
           ln1_g, ln1_b, ln2_g, ln2_b):
    mod = (jax.nn.silu(c) @ w_ada + b_ada)[:, None, :]
    sh_m, sc_m, g_m, sh_f, sc_f, g_f = jnp.split(mod, 6, axis=-1)
    h = x * (1 + sc_m) + sh_m
    y, new_buf, s_new, k, v, ik = _token_mixer(h, conv_buf, s0, attend, w_in, conv_w, a_log, dt_bias, dn_norm_g, w_branch, w_out)
    x = _layernorm(DEEPNORM_ALPHA * x + g_m * y, ln1_g, ln1_b)
    h = x * (1 + sc_f) + sh_f
    x = _layernorm(DEEPNORM_ALPHA * x + g_f * ffn(h), ln2_g, ln2_b)
    return x, (new_buf, s_new, k, v, ik)


def setup_inputs(seed: int = 0) -> dict:
    key = jax.random.key(seed)
    ks = iter(jax.random.split(key, 40))

    def nrm(shape, scale):
        return jax.random.normal(next(ks), shape, jnp.float32) * scale

    n_pages = PAST_LEN // PAGE_SIZE
    n_used = DEC_BATCH * n_pages
    n_phys = n_used + max(1, n_used // 4)
    page_table = jax.random.permutation(next(ks), n_phys)[:n_used].reshape(DEC_BATCH, n_pages).astype(jnp.int32)
    dt = jnp.exp(jax.random.uniform(next(ks), (DEPTH, DN_HEADS), jnp.float32, math.log(1e-3), math.log(1e-1)))
    a_log = jnp.log(jax.random.uniform(next(ks), (DEPTH, DN_HEADS), jnp.float32, 1.0, 16.0))
    return {
        'x_prompt': nrm((BATCH, SEQ, D_MODEL), 1.0),
        'x_sample': nrm((DEC_BATCH, DEC_SEQ, D_MODEL), 1.0),
        'cache_k': nrm((DEPTH, n_phys, PAGE_SIZE, ATT_HEADS, ATT_HEAD_DIM), 1.0),
        'cache_v': nrm((DEPTH, n_phys, PAGE_SIZE, ATT_HEADS, ATT_HEAD_DIM), 1.0),
        'cache_idx_k': nrm((DEPTH, n_phys, PAGE_SIZE, IDX_DIM), 1.0),
        'state_delta': nrm((DEPTH, DEC_BATCH, DN_HEADS, DN_HEAD_DIM, DN_HEAD_DIM), 0.1),
        'state_conv': nrm((DEPTH, DEC_BATCH, CONV_WIDTH - 1, 3 * DN_WIDTH), 1.0),
        'page_table': page_table,
        'c_prompt': nrm((BATCH, D_MODEL), 1.0),
        'c_sample': nrm((DEC_BATCH, D_MODEL), 1.0),
        'w_ada': nrm((DEPTH, D_MODEL, 6 * D_MODEL), 0.5 * D_MODEL ** -0.5),
        'b_ada': nrm((DEPTH, 6 * D_MODEL), 0.02),
        'w_in': nrm((DEPTH, D_MODEL, N_IN), D_MODEL ** -0.5),
        'conv_w': nrm((DEPTH, CONV_WIDTH, 3 * DN_WIDTH), CONV_WIDTH ** -0.5),
        'a_log': a_log,
        'dt_bias': dt + jnp.log(-jnp.expm1(-dt)),
        'dn_norm_g': 1.0 + nrm((DEPTH, DN_HEAD_DIM), 0.02),
        'w_branch': nrm((DEPTH, 2, DN_WIDTH, D_MODEL), DN_WIDTH ** -0.5),
        'w_out': nrm((DEPTH, D_MODEL, D_MODEL), DEEPNORM_BETA * D_MODEL ** -0.5),
        'ln1_g': 1.0 + nrm((DEPTH, D_MODEL), 0.02),
        'ln1_b': nrm((DEPTH, D_MODEL), 0.02),
        'ln2_g': 1.0 + nrm((DEPTH, D_MODEL), 0.02),
        'ln2_b': nrm((DEPTH, D_MODEL), 0.02),
        'w_ffn_gate': nrm((N_DENSE, D_MODEL, D_FF), D_MODEL ** -0.5),
        'w_ffn_up': nrm((N_DENSE, D_MODEL, D_FF), D_MODEL ** -0.5),
        'w_ffn_down': nrm((N_DENSE, D_FF, D_MODEL), DEEPNORM_BETA * D_FF ** -0.5),
        'w_router': nrm((N_MOE, D_MODEL, N_EXPERTS), D_MODEL ** -0.5),
        'w_exp_gate': nrm((N_MOE, N_EXPERTS, D_MODEL, D_FF_EXPERT), D_MODEL ** -0.5),
        'w_exp_up': nrm((N_MOE, N_EXPERTS, D_MODEL, D_FF_EXPERT), D_MODEL ** -0.5),
        'w_exp_down': nrm((N_MOE, N_EXPERTS, D_FF_EXPERT, D_MODEL), DEEPNORM_BETA * D_FF_EXPERT ** -0.5),
    }


def reference(x_prompt, x_sample, cache_k, cache_v, cache_idx_k, state_delta, state_conv, page_table, c_prompt, c_sample,
              w_ada, b_ada, w_in, conv_w, a_log, dt_bias, dn_norm_g, w_branch, w_out, ln1_g, ln1_b, ln2_g, ln2_b,
              w_ffn_gate, w_ffn_up, w_ffn_down, w_router, w_exp_gate, w_exp_up, w_exp_down):
    xp, xs = x_prompt, x_sample
    bp = xp.shape[0]
    conv_zero = jnp.zeros((bp, CONV_WIDTH - 1, 3 * DN_WIDTH), xp.dtype)
    s_zero = jnp.zeros((bp, DN_HEADS, DN_HEAD_DIM, DN_HEAD_DIM), state_delta.dtype)
    rows_p = []
    rows_s = []
    for l in range(DEPTH):
        j = l // 2
        if l % 2 == 0:
            ffn = lambda h, j=j: _swiglu(h, w_ffn_gate[j], w_ffn_up[j], w_ffn_down[j])
        else:
            ffn = lambda h, j=j: _moe_swiglu(h, w_router[j], w_exp_gate[j], w_exp_up[j], w_exp_down[j])
        shared = (w_ada[l], b_ada[l], w_in[l], conv_w[l], a_log[l], dt_bias[l], dn_norm_g[l], w_branch[l], w_out[l],
                  ln1_g[l], ln1_b[l], ln2_g[l], ln2_b[l])
        xp, out_p = _layer(xp, c_prompt, conv_zero, s_zero, _prompt_sparse_attention, ffn, *shared)
        sample_attend = lambda q, k, v, iq, ik, iw, l=l: _sample_sparse_attention(
            q, k, v, iq, ik, iw, cache_k, cache_v, cache_idx_k, page_table, l)
        xs, out_s = _layer(xs, c_sample, state_conv[l], state_delta[l], sample_attend, ffn, *shared)
        rows_p.append(out_p)
        rows_s.append(out_s)
    conv_p = jnp.stack([r[0] for r in rows_p])
    delta_p = jnp.stack([r[1] for r in rows_p])
    k_p = jnp.stack([r[2] for r in rows_p])
    v_p = jnp.stack([r[3] for r in rows_p])
    ik_p = jnp.stack([r[4] for r in rows_p])
    conv_s = jnp.stack([r[0] for r in rows_s])
    delta_s = jnp.stack([r[1] for r in rows_s])
    k_s = jnp.stack([r[2] for r in rows_s])
    v_s = jnp.stack([r[3] for r in rows_s])
    ik_s = jnp.stack([r[4] for r in rows_s])
    return (xp, xs, k_p, v_p, ik_p, delta_p, conv_p, k_s, v_s, ik_s, delta_s, conv_s)
```

```python
import functools

import jax
import jax.numpy as jnp
from jax import lax
from jax.experimental import pallas as pl
from jax.experimental.pallas import tpu as pltpu

F32 = jnp.float32
BF16 = jnp.bfloat16
I32 = jnp.int32
HI = lax.Precision.HIGHEST

D_MODEL = 1024
DEPTH = 4
PAGE = 128
DN_HEADS = 4
DN_DIM = 128
DN_WIDTH = DN_HEADS * DN_DIM
CONV_W = 4
DN_CHUNK = 64
ATT_HEADS = 4
ATT_DIM = 128
ATT_WIDTH = ATT_HEADS * ATT_DIM
IDX_HEADS = 8
IDX_DIM = 64
TOPK = 256
D_FF = 2816
N_EXPERTS = 8
ALPHA = (2.0 * DEPTH) ** 0.25
EPS = 1e-5
NEG = -1e30

SM_IK = 0
SM_B = 64
SM_A = 68
SM_IW = 72

VMEM_LIMIT = 56 * 1024 * 1024


def _cparams(sem):
    return pltpu.CompilerParams(dimension_semantics=sem, vmem_limit_bytes=VMEM_LIMIT)


def _sigmoid(x):
    return jax.nn.sigmoid(x)


def _silu(x):
    return x * jax.nn.sigmoid(x)


def _softplus(x):
    return jnp.maximum(x, 0.0) + jnp.log1p(jnp.exp(-jnp.abs(x)))


def _dot(a, b, precision=None):
    return jnp.dot(a, b, precision=precision, preferred_element_type=F32)


def _dot_nt(a, b, precision=None):
    return lax.dot_general(a, b, (((1,), (1,)), ((), ())), precision=precision, preferred_element_type=F32)


def _dot_tn(a, b, precision=None):
    return lax.dot_general(a, b, (((0,), (0,)), ((), ())), precision=precision, preferred_element_type=F32)


def _layernorm(r, g, b):
    mu = jnp.mean(r, axis=-1, keepdims=True)
    rc = r - mu
    var = jnp.mean(rc * rc, axis=-1, keepdims=True)
    return rc * lax.rsqrt(var + EPS) * g + b


def _ada_kernel(c_ref, w_ref, b_ref, o_ref):
    o_ref[...] = _dot(_silu(c_ref[...]), w_ref[...], HI) + b_ref[...]


def _ada_all(c, w_ada, b_ada):
    rows = c.shape[0]
    nj = w_ada.shape[2] // D_MODEL
    return pl.pallas_call(
        _ada_kernel,
        grid=(DEPTH, nj),
        in_specs=[
            pl.BlockSpec((rows, D_MODEL), lambda l, j: (0, 0)),
            pl.BlockSpec((None, D_MODEL, D_MODEL), lambda l, j: (l, 0, j)),
            pl.BlockSpec((None, 1, D_MODEL), lambda l, j: (l, 0, j)),
        ],
        out_specs=pl.BlockSpec((None, rows, D_MODEL), lambda l, j: (l, 0, j)),
        out_shape=jax.ShapeDtypeStruct((DEPTH, rows, nj * D_MODEL), F32),
        compiler_params=_cparams(("arbitrary", "arbitrary")),
        name="ada_mod",
    )(c, w_ada, b_ada.reshape(DEPTH, 1, -1))


IN_GROUPS = (
    ("qkv_a", 3 * DN_WIDTH, (F32,)),
    ("z", DN_WIDTH, (F32,)),
    ("q_b", ATT_WIDTH, (BF16,)),
    ("k_b", ATT_WIDTH, (F32, BF16)),
    ("v_b", ATT_WIDTH, (F32, BF16)),
    ("iq", IDX_HEADS * IDX_DIM, (BF16,)),
    ("gates", 2 * D_MODEL, (F32,)),
    ("small", 128, (F32,)),
)


def _split_w_in(w):
    o = 0
    parts = {}
    for name, width in (("qkv_a", 3 * DN_WIDTH), ("z", DN_WIDTH), ("b", DN_HEADS), ("a", DN_HEADS),
                        ("q_b", ATT_WIDTH), ("k_b", ATT_WIDTH), ("v_b", ATT_WIDTH),
                        ("iq", IDX_HEADS * IDX_DIM), ("ik", IDX_DIM), ("iw", IDX_HEADS), ("gates", 2 * D_MODEL)):
        parts[name] = w[:, o:o + width]
        o += width
    small = jnp.concatenate(
        [parts["ik"], parts["b"], parts["a"], parts["iw"],
         jnp.zeros((w.shape[0], 128 - IDX_DIM - 2 * DN_HEADS - IDX_HEADS), w.dtype)], axis=1)
    parts["small"] = small
    return [parts[name].astype(BF16) for name, _, _ in IN_GROUPS]


def _in_kernel(x_ref, sc_ref, sh_ref, *refs):
    n = len(IN_GROUPS)
    w_refs, o_refs = refs[:n], refs[n:]
    h = (x_ref[...] * (1.0 + sc_ref[...]) + sh_ref[...]).astype(BF16)
    k = 0
    for (name, width, dtypes), w_ref in zip(IN_GROUPS, w_refs):
        y = _dot(h, w_ref[...])
        for dt in dtypes:
            o_refs[k][...] = y.astype(dt)
            k += 1


def _in_proj(x, sc, sh, w_groups, tm):
    n = x.shape[0]
    nb, r, _ = sc.shape
    nt = n // tm
    tiles_per_b = nt // nb
    mod_spec = pl.BlockSpec((None, r, D_MODEL), lambda i: (i // tiles_per_b, 0, 0))
    in_specs = [pl.BlockSpec((tm, D_MODEL), lambda i: (i, 0)), mod_spec, mod_spec]
    for (name, width, _), w in zip(IN_GROUPS, w_groups):
        in_specs.append(pl.BlockSpec((D_MODEL, width), lambda i: (0, 0), pipeline_mode=pl.Buffered(1)))
    out_specs, out_shape = [], []
    for name, width, dtypes in IN_GROUPS:
        for dt in dtypes:
            out_specs.append(pl.BlockSpec((tm, width), lambda i: (i, 0)))
            out_shape.append(jax.ShapeDtypeStruct((n, width), dt))
    outs = pl.pallas_call(
        _in_kernel,
        grid=(nt,),
        in_specs=in_specs,
        out_specs=out_specs,
        out_shape=out_shape,
        compiler_params=_cparams(("parallel",)),
        name="in_proj",
    )(x, sc, sh, *w_groups)
    names = []
    for name, _, dtypes in IN_GROUPS:
        for dt in dtypes:
            names.append(name if dt == dtypes[0] else name + "16")
    return dict(zip(names, outs))


def _delta_kernel(qkv_ref, z_ref, sm_ref, cw_ref, alog_ref, dt_ref, ng_ref, o_ref, s_out_ref, xs, s_sc, *, tt):
    i = pl.program_id(1)
    nc = tt // DN_CHUNK
    c = DN_CHUNK

    @pl.when(i == 0)
    def _():
        xs[0:8, :] = jnp.zeros((8, 3 * DN_WIDTH), F32)
        s_sc[...] = jnp.zeros_like(s_sc)

    xs[8:8 + tt, :] = qkv_ref[...]
    cw = cw_ref[...]
    y = xs[8:8 + tt, :] * cw[3:4, :]
    for j in range(1, CONV_W):
        y = y + xs[8 - j:8 - j + tt, :] * cw[3 - j:4 - j, :]
    act = _silu(y)
    xs[0:8, :] = xs[tt:tt + 8, :]

    sm = sm_ref[...]
    beta_all = _sigmoid(sm)
    g_all = -jnp.exp(alog_ref[...]) * _softplus(sm + dt_ref[...])
    r_io = lax.broadcasted_iota(I32, (tt, tt), 0)
    c_io = lax.broadcasted_iota(I32, (tt, tt), 1)
    lblk = jnp.where((r_io // c == c_io // c) & (r_io >= c_io), 1.0, 0.0).astype(F32)
    gcum = _dot(lblk, g_all, HI)
    gcum_t = gcum.T

    ri = lax.broadcasted_iota(I32, (c, c), 0)
    ci = lax.broadcasted_iota(I32, (c, c), 1)
    incl = ri >= ci
    strict = ri > ci
    eye = jnp.where(ri == ci, 1.0, 0.0).astype(F32)
    ng = ng_ref[...]

    for h in range(DN_HEADS):
        lo, hi = h * DN_DIM, (h + 1) * DN_DIM
        q_h = act[:, lo:hi]
        k_h = act[:, DN_WIDTH + lo:DN_WIDTH + hi]
        v_h = act[:, 2 * DN_WIDTH + lo:2 * DN_WIDTH + hi]
        q_h = q_h * lax.rsqrt(jnp.sum(q_h * q_h, axis=-1, keepdims=True) + 1e-6) * (DN_DIM ** -0.5)
        k_h = k_h * lax.rsqrt(jnp.sum(k_h * k_h, axis=-1, keepdims=True) + 1e-6)
        z_h = z_ref[:, lo:hi]
        s_h = s_sc[h]
        for cc in range(nc):
            r0, r1 = cc * c, (cc + 1) * c
            qc, kc, vc = q_h[r0:r1], k_h[r0:r1], v_h[r0:r1]
            gi = gcum[r0:r1, SM_A + h:SM_A + h + 1]
            gj = gcum_t[SM_A + h:SM_A + h + 1, r0:r1]
            bi = beta_all[r0:r1, SM_B + h:SM_B + h + 1]
            dmat = jnp.exp(jnp.where(incl, gi - gj, -jnp.inf))
            kk = _dot_nt(kc, kc, HI)
            qk = _dot_nt(qc, kc, HI)
            nm = jnp.where(strict, -(bi * kk * dmat), 0.0)
            tm_ = eye + nm
            p = nm
            for _ in range(5):
                p = _dot(p, p, HI)
                tm_ = tm_ + _dot(tm_, p, HI)
            eg = jnp.exp(gi)
            rhs = jnp.concatenate([vc * bi, kc * (bi * eg)], axis=1)
            sol = _dot(tm_, rhs, HI)
            u, w = sol[:, :DN_DIM], sol[:, DN_DIM:]
            v_new = u - _dot(w, s_h, HI)
            o = _dot(qc * eg, s_h, HI) + _dot(qk * dmat, v_new, HI)
            g_last = gi[c - 1:c, :]
            s_h = s_h * jnp.exp(g_last) + _dot_tn(kc * jnp.exp(g_last - gi), v_new, HI)
            o = o * lax.rsqrt(jnp.mean(o * o, axis=-1, keepdims=True) + EPS) * ng
            o_ref[r0:r1, lo:hi] = o * _silu(z_h[r0:r1])
        s_sc[h] = s_h

    @pl.when(i == pl.num_programs(1) - 1)
    def _():
        s_out_ref[...] = s_sc[...]


def _delta_prompt(qkv, z, small, conv_w, alog_row, dt_row, norm_g, nb, tt=256):
    n = qkv.shape[0]
    nt = n // nb // tt
    row = lambda b, i: (b * nt + i, 0)
    const = lambda b, i: (0, 0)
    return pl.pallas_call(
        functools.partial(_delta_kernel, tt=tt),
        grid=(nb, nt),
        in_specs=[
            pl.BlockSpec((tt, 3 * DN_WIDTH), row),
            pl.BlockSpec((tt, DN_WIDTH), row),
            pl.BlockSpec((tt, 128), row),
            pl.BlockSpec((CONV_W, 3 * DN_WIDTH), const),
            pl.BlockSpec((1, 128), const),
            pl.BlockSpec((1, 128), const),
            pl.BlockSpec((1, DN_DIM), const),
        ],
        out_specs=[
            pl.BlockSpec((tt, DN_WIDTH), row),
            pl.BlockSpec((None, DN_HEADS, DN_DIM, DN_DIM), lambda b, i: (b, 0, 0, 0)),
        ],
        out_shape=[
            jax.ShapeDtypeStruct((n, DN_WIDTH), F32),
            jax.ShapeDtypeStruct((nb, DN_HEADS, DN_DIM, DN_DIM), F32),
        ],
        scratch_shapes=[
            pltpu.VMEM((tt + 8, 3 * DN_WIDTH), F32),
            pltpu.VMEM((DN_HEADS, DN_DIM, DN_DIM), F32),
        ],
        compiler_params=_cparams(("arbitrary", "arbitrary")),
        name="delta_prompt",
    )(qkv, z, small, conv_w, alog_row, dt_row, norm_g)


def _sdelta_kernel(qkv_ref, buf_ref, z_ref, sm_ref, cw_ref, alog_ref, dt_ref, ng_ref, s_ref, o_ref, s_out_ref, *, rows):
    cw = cw_ref[...]
    y = qkv_ref[...] * cw[3:4, :]
    for j in range(CONV_W - 1):
        y = y + buf_ref[:, j, :] * cw[j:j + 1, :]
    act = _silu(y)
    sm = sm_ref[...]
    beta_all = _sigmoid(sm)
    g_all = -jnp.exp(alog_ref[...]) * _softplus(sm + dt_ref[...])
    ng = ng_ref[...]
    sub = lax.broadcasted_iota(I32, (8, DN_DIM), 0)
    for h in range(DN_HEADS):
        lo, hi = h * DN_DIM, (h + 1) * DN_DIM
        q_h = act[:, lo:hi]
        k_h = act[:, DN_WIDTH + lo:DN_WIDTH + hi]
        v_h = act[:, 2 * DN_WIDTH + lo:2 * DN_WIDTH + hi]
        q_h = q_h * lax.rsqrt(jnp.sum(q_h * q_h, axis=-1, keepdims=True) + 1e-6) * (DN_DIM ** -0.5)
        k_h = k_h * lax.rsqrt(jnp.sum(k_h * k_h, axis=-1, keepdims=True) + 1e-6)
        g = g_all[:, SM_A + h:SM_A + h + 1]
        b = beta_all[:, SM_B + h:SM_B + h + 1]
        eg = jnp.exp(g)
        u = v_h * b
        w = k_h * (b * eg)
        qe = q_h * eg
        qk = jnp.sum(q_h * k_h, axis=-1, keepdims=True)
        z_h = z_ref[:, lo:hi]
        for r in range(rows):
            s = s_ref[r, h]
            w8 = jnp.where(sub == 0, w[r:r + 1], 0.0)
            q8 = jnp.where(sub == 0, qe[r:r + 1], 0.0)
            k8 = jnp.where(sub == 0, k_h[r:r + 1], 0.0)
            v_new = u[r:r + 1] - _dot(w8, s, HI)[0:1]
            o = _dot(q8, s, HI)[0:1] + qk[r:r + 1] * v_new
            vn8 = jnp.where(sub == 0, v_new, 0.0)
            s_out_ref[r, h] = s * eg[r:r + 1] + _dot_tn(k8, vn8, HI)
            o = o * lax.rsqrt(jnp.mean(o * o, axis=-1, keepdims=True) + EPS) * ng
            o_ref[r:r + 1, lo:hi] = o * _silu(z_h[r:r + 1])


def _delta_sample(qkv, buf, z, small, conv_w, alog_row, dt_row, norm_g, state, rows=8):
    n = qkv.shape[0]
    row = lambda i: (i, 0)
    const = lambda i: (0, 0)
    return pl.pallas_call(
        functools.partial(_sdelta_kernel, rows=rows),
        grid=(n // rows,),
        in_specs=[
            pl.BlockSpec((rows, 3 * DN_WIDTH), row),
            pl.BlockSpec((rows, CONV_W - 1, 3 * DN_WIDTH), lambda i: (i, 0, 0)),
            pl.BlockSpec((rows, DN_WIDTH), row),
            pl.BlockSpec((rows, 128), row),
            pl.BlockSpec((CONV_W, 3 * DN_WIDTH), const),
            pl.BlockSpec((1, 128), const),
            pl.BlockSpec((1, 128), const),
            pl.BlockSpec((1, DN_DIM), const),
            pl.BlockSpec((rows, DN_HEADS, DN_DIM, DN_DIM), lambda i: (i, 0, 0, 0)),
        ],
        out_specs=[
            pl.BlockSpec((rows, DN_WIDTH), row),
            pl.BlockSpec((rows, DN_HEADS, DN_DIM, DN_DIM), lambda i: (i, 0, 0, 0)),
        ],
        out_shape=[
            jax.ShapeDtypeStruct((n, DN_WIDTH), F32),
            jax.ShapeDtypeStruct(state.shape, F32),
        ],
        compiler_params=_cparams(("parallel",)),
        name="delta_sample",
    )(qkv, buf, z, small, conv_w, alog_row, dt_row, norm_g, state)


def _score_keys(s):
    s = jnp.where(s == 0.0, 0.0, s)
    k = pltpu.bitcast(s, I32)
    return jnp.where(k < 0, k ^ jnp.int32(0x7FFFFFFF), k)


def _attn_kernel(q_ref, iq_ref, sm_ref, k_ref, v_ref, ik_ref, o_ref, keys, m_sc, l_sc, acc_sc, *, tq, ck):
    i = pl.program_id(1)
    nk = (i * tq + tq + ck - 1) // ck
    t = i * tq + lax.broadcasted_iota(I32, (tq, 1), 0)
    lane = lax.broadcasted_iota(I32, (1, ck), 1)
    iq = iq_ref[...]
    sm = sm_ref[...]
    iw = [sm[:, SM_IW + h:SM_IW + h + 1] * (IDX_HEADS ** -0.5) for h in range(IDX_HEADS)]

    def score_chunk(c, carry):
        off = pl.multiple_of(c * ck, ck)
        ikc = ik_ref[pl.ds(off, ck), :]
        s = jnp.zeros((tq, ck), F32)
        for h in range(IDX_HEADS):
            d = _dot_nt(iq[:, h * IDX_DIM:(h + 1) * IDX_DIM], ikc) * (IDX_DIM ** -0.5)
            s = s + jnp.maximum(d, 0.0) * iw[h]
        s = jnp.where(off + lane <= t, s, -jnp.inf)
        keys[c] = _score_keys(s)
        return carry

    lax.fori_loop(0, nk, score_chunk, 0)

    kth = jnp.minimum(t + 1, TOPK)

    def count(pred_fn):
        def body(c, cnt):
            kc = keys[c]
            m = pred_fn(kc, c)
            part = jnp.where(m, 1, 0).astype(I32)
            acc = part[:, 0:128]
            for u in range(1, ck // 128):
                acc = acc + part[:, u * 128:(u + 1) * 128]
            return cnt + acc
        cnt = lax.fori_loop(0, nk, body, jnp.zeros((tq, 128), I32))
        return jnp.sum(cnt, axis=-1, keepdims=True)

    def search_step(it, thr):
        cand = thr ^ jnp.left_shift(jnp.int32(1), 31 - it)
        cnt = count(lambda kc, c: kc >= cand)
        return jnp.where(cnt >= kth, cand, thr)

    thr = lax.fori_loop(0, 32, search_step, jnp.full((tq, 1), -2 ** 31, I32))

    n_gt = count(lambda kc, c: kc > thr)
    n_eq = count(lambda kc, c: kc == thr)
    need = kth - n_gt
    excess = jnp.max(n_eq - need)

    @pl.when(excess > 0)
    def _():
        def pos_step(it, cut):
            cand = cut | jnp.left_shift(jnp.int32(1), 13 - it)
            cnt = count(lambda kc, c: (kc == thr) & (c * ck + lane < cand))
            return jnp.where(cnt < need, cand, cut)
        cut = lax.fori_loop(0, 14, pos_step, jnp.zeros((tq, 1), I32))

        def drop(c, carry):
            kc = keys[c]
            keys[c] = jnp.where((kc == thr) & (c * ck + lane > cut), jnp.int32(-2 ** 31), kc)
            return carry
        lax.fori_loop(0, nk, drop, 0)

    m_sc[...] = jnp.full_like(m_sc, NEG)
    l_sc[...] = jnp.zeros_like(l_sc)
    acc_sc[...] = jnp.zeros_like(acc_sc)
    q = q_ref[...]

    def attend_chunk(c, carry):
        off = pl.multiple_of(c * ck, ck)
        sel = keys[c] >= thr
        dist = (t - (off + lane)).astype(F32)
        kc = k_ref[pl.ds(off, ck), :]
        vc = v_ref[pl.ds(off, ck), :]
        for h in range(ATT_HEADS):
            lo, hi = h * ATT_DIM, (h + 1) * ATT_DIM
            slope = 2.0 ** (-8.0 * (h + 1) / ATT_HEADS)
            lg = _dot_nt(q[:, lo:hi], kc[:, lo:hi]) * (ATT_DIM ** -0.5) - slope * dist
            lg = jnp.where(sel, lg, NEG)
            m_old = m_sc[h]
            m_new = jnp.maximum(m_old, jnp.max(lg, axis=-1, keepdims=True))
            a = jnp.exp(m_old - m_new)
            p = jnp.exp(lg - m_new)
            l_sc[h] = a * l_sc[h] + jnp.sum(p, axis=-1, keepdims=True)
            acc_sc[h] = a * acc_sc[h] + _dot(p.astype(BF16), vc[:, lo:hi])
            m_sc[h] = m_new
        return carry

    lax.fori_loop(0, nk, attend_chunk, 0)
    for h in range(ATT_HEADS):
        o_ref[:, h * ATT_DIM:(h + 1) * ATT_DIM] = acc_sc[h] / l_sc[h]


def _attn_prompt(q16, iq16, small, k16, v16, ik16, nb, tq=128, ck=512):
    n = q16.shape[0]
    s = n // nb
    nq = s // tq
    row = lambda b, i: (b * nq + i, 0)
    full = lambda b, i: (b, 0)
    return pl.pallas_call(
        functools.partial(_attn_kernel, tq=tq, ck=ck),
        grid=(nb, nq),
        in_specs=[
            pl.BlockSpec((tq, ATT_WIDTH), row),
            pl.BlockSpec((tq, IDX_HEADS * IDX_DIM), row),
            pl.BlockSpec((tq, 128), row),
            pl.BlockSpec((s, ATT_WIDTH), full),
            pl.BlockSpec((s, ATT_WIDTH), full),
            pl.BlockSpec((s, IDX_DIM), full),
        ],
        out_specs=pl.BlockSpec((tq, ATT_WIDTH), row),
        out_shape=jax.ShapeDtypeStruct((n, ATT_WIDTH), F32),
        scratch_shapes=[
            pltpu.VMEM((s // ck, tq, ck), I32),
            pltpu.VMEM((ATT_HEADS, tq, 1), F32),
            pltpu.VMEM((ATT_HEADS, tq, 1), F32),
            pltpu.VMEM((ATT_HEADS, tq, ATT_DIM), F32),
        ],
        compiler_params=_cparams(("arbitrary", "arbitrary")),
        name="attn_prompt",
    )(q16, iq16, small, k16, v16, ik16)


PG = 8


def _sscore_kernel(pt_ref, iq_ref, iw_ref, ikn_ref, *refs):
    pages, o_ref = refs[:PG], refs[PG]
    j = pl.program_id(1)
    iq = iq_ref[...]
    iw = iw_ref[...] * (IDX_HEADS ** -0.5)

    @pl.when(j < pl.num_programs(1) - 1)
    def _():
        for u in range(PG):
            d = _dot_nt(iq, pages[u][...]) * (IDX_DIM ** -0.5)
            o_ref[:, u * PAGE:(u + 1) * PAGE] = jnp.sum(jnp.maximum(d, 0.0) * iw, axis=0, keepdims=True)

    @pl.when(j == pl.num_programs(1) - 1)
    def _():
        d = _dot_nt(iq, jnp.broadcast_to(ikn_ref[...], (8, IDX_DIM)))[:, 0:1] * (IDX_DIM ** -0.5)
        s_new = jnp.sum(jnp.maximum(d, 0.0) * iw, axis=0, keepdims=True)
        lane = lax.broadcasted_iota(I32, (1, PG * PAGE), 1)
        o_ref[...] = jnp.where(lane == 0, s_new, -jnp.inf)


def _sample_scores(layer, page_table, iq3, iw3, ik_new3, cache_idx_k):
    nbatch, n_pages = page_table.shape
    nj = n_pages // PG

    def page_map(u):
        return lambda b, j, pt: (layer, pt[b, jnp.minimum(j * PG + u, n_pages - 1)], 0, 0)

    grid_spec = pltpu.PrefetchScalarGridSpec(
        num_scalar_prefetch=1,
        grid=(nbatch, nj + 1),
        in_specs=[
            pl.BlockSpec((None, IDX_HEADS, IDX_DIM), lambda b, j, pt: (b, 0, 0)),
            pl.BlockSpec((None, IDX_HEADS, 1), lambda b, j, pt: (b, 0, 0)),
            pl.BlockSpec((None, 1, IDX_DIM), lambda b, j, pt: (b, 0, 0)),
        ] + [pl.BlockSpec((None, None, PAGE, IDX_DIM), page_map(u)) for u in range(PG)],
        out_specs=pl.BlockSpec((None, 1, PG * PAGE), lambda b, j, pt: (b, 0, j)),
    )
    return pl.pallas_call(
        _sscore_kernel,
        grid_spec=grid_spec,
        out_shape=jax.ShapeDtypeStruct((nbatch, 1, (nj + 1) * PG * PAGE), F32),
        compiler_params=_cparams(("arbitrary", "arbitrary")),
        name="sample_scores",
    )(page_table, iq3, iw3, ik_new3, *([cache_idx_k] * PG))


def _sselect_kernel(s_ref, o_ref):
    keys = _score_keys(s_ref[...])
    rows, width = keys.shape
    pos = lax.broadcasted_iota(I32, (1, width), 1)

    def cnt(m):
        return jnp.sum(jnp.where(m, 1, 0).astype(I32), axis=-1, keepdims=True)

    def search_step(it, thr):
        cand = thr ^ jnp.left_shift(jnp.int32(1), 31 - it)
        return jnp.where(cnt(keys >= cand) >= TOPK, cand, thr)

    thr = lax.fori_loop(0, 32, search_step, jnp.full((rows, 1), -2 ** 31, I32))
    need = TOPK - cnt(keys > thr)
    eq = keys == thr

    def pos_step(it, cut):
        cand = cut | jnp.left_shift(jnp.int32(1), 14 - it)
        return jnp.where(cnt(eq & (pos < cand)) < need, cand, cut)

    cut = lax.fori_loop(0, 15, pos_step, jnp.zeros((rows, 1), I32))
    sel = (keys > thr) | (eq & (pos <= cut))
    o_ref[...] = jnp.where(sel, 0.0, NEG)


def _sample_select(scores):
    return pl.pallas_call(
        _sselect_kernel,
        out_shape=jax.ShapeDtypeStruct(scores.shape, F32),
        compiler_params=pltpu.CompilerParams(vmem_limit_bytes=VMEM_LIMIT),
        name="sample_select",
    )(scores)


def _sattn_kernel(pt_ref, q_ref, kn_ref, vn_ref, selb_ref, seln_ref, *refs, past):
    kp, vp, o_ref, m_sc, l_sc, acc_sc = refs[:PG], refs[PG:2 * PG], refs[2 * PG], refs[2 * PG + 1], refs[2 * PG + 2], refs[2 * PG + 3]
    j = pl.program_id(1)

    @pl.when(j == 0)
    def _():
        m_sc[...] = jnp.full_like(m_sc, NEG)
        l_sc[...] = jnp.zeros_like(l_sc)
        acc_sc[...] = jnp.zeros_like(acc_sc)

    q = q_ref[...]
    lane = lax.broadcasted_iota(I32, (1, PAGE), 1)

    def update(h, lg, vals):
        m_old = m_sc[h]
        m_new = jnp.maximum(m_old, jnp.max(lg, axis=-1, keepdims=True))
        a = jnp.exp(m_old - m_new)
        p = jnp.exp(lg - m_new)
        l_sc[h] = a * l_sc[h] + jnp.sum(p, axis=-1, keepdims=True)
        acc_sc[h] = a * acc_sc[h] + _dot(p, vals)
        m_sc[h] = m_new

    for u in range(PG):
        kpage = kp[u][...]
        vpage = vp[u][...]
        bias = selb_ref[:, u * PAGE:(u + 1) * PAGE]
        dist = (past - ((j * PG + u) * PAGE + lane)).astype(F32)
        for h in range(ATT_HEADS):
            lo, hi = h * ATT_DIM, (h + 1) * ATT_DIM
            slope = 2.0 ** (-8.0 * (h + 1) / ATT_HEADS)
            q8 = jnp.broadcast_to(q[:, lo:hi], (8, ATT_DIM))
            lg = _dot_nt(q8, kpage[:, lo:hi]) * (ATT_DIM ** -0.5) - slope * dist + bias
            update(h, lg, vpage[:, lo:hi])

    @pl.when(j == pl.num_programs(1) - 1)
    def _():
        kn = kn_ref[...]
        vn = vn_ref[...]
        bias = seln_ref[:, 0:1]
        for h in range(ATT_HEADS):
            lo, hi = h * ATT_DIM, (h + 1) * ATT_DIM
            q8 = jnp.broadcast_to(q[:, lo:hi], (8, ATT_DIM))
            k8 = jnp.broadcast_to(kn[:, lo:hi], (8, ATT_DIM))
            v8 = jnp.broadcast_to(vn[:, lo:hi], (8, ATT_DIM))
            lg = _dot_nt(q8, k8)[:, 0:1] * (ATT_DIM ** -0.5) + bias
            m_old = m_sc[h]
            m_new = jnp.maximum(m_old, lg)
            a = jnp.exp(m_old - m_new)
            p = jnp.exp(lg - m_new)
            l_fin = a * l_sc[h] + p
            acc = a * acc_sc[h] + p * v8
            o_ref[:, lo:hi] = (acc / l_fin)[0:1]


def _sample_attend(layer, page_table, q3, kn3, vn3, selb, cache_k4, cache_v4):
    nbatch, n_pages = page_table.shape
    nj = n_pages // PG
    past = n_pages * PAGE

    def page_map(u):
        return lambda b, j, pt: (layer, pt[b, j * PG + u], 0, 0)

    vec = pl.BlockSpec((None, 1, ATT_WIDTH), lambda b, j, pt: (b, 0, 0))
    grid_spec = pltpu.PrefetchScalarGridSpec(
        num_scalar_prefetch=1,
        grid=(nbatch, nj),
        in_specs=[
            vec, vec, vec,
            pl.BlockSpec((None, 1, PG * PAGE), lambda b, j, pt: (b, 0, j)),
            pl.BlockSpec((None, 1, 128), lambda b, j, pt: (b, 0, past // 128)),
        ] + [pl.BlockSpec((None, None, PAGE, ATT_WIDTH), page_map(u)) for u in range(PG)] * 2,
        out_specs=pl.BlockSpec((None, 1, ATT_WIDTH), lambda b, j, pt: (b, 0, 0)),
        scratch_shapes=[
            pltpu.VMEM((ATT_HEADS, 8, 1), F32),
            pltpu.VMEM((ATT_HEADS, 8, 1), F32),
            pltpu.VMEM((ATT_HEADS, 8, ATT_DIM), F32),
        ],
    )
    return pl.pallas_call(
        functools.partial(_sattn_kernel, past=past),
        grid_spec=grid_spec,
        out_shape=jax.ShapeDtypeStruct((nbatch, 1, ATT_WIDTH), F32),
        compiler_params=_cparams(("arbitrary", "arbitrary")),
        name="sample_attend",
    )(page_table, q3, kn3, vn3, selb, selb, *([cache_k4] * PG), *([cache_v4] * PG))


def _mix_kernel(oa_ref, ob_ref, gates_ref, x_ref, gm_ref, wa_ref, wb_ref, wo_ref, g_ref, b_ref, o_ref):
    br_a = _dot(oa_ref[...].astype(BF16), wa_ref[...])
    br_b = _dot(ob_ref[...].astype(BF16), wb_ref[...])
    merged = _sigmoid(gates_ref[:, :D_MODEL]) * br_a + _sigmoid(gates_ref[:, D_MODEL:]) * br_b
    y = _dot(merged.astype(BF16), wo_ref[...])
    r = ALPHA * x_ref[...] + gm_ref[...] * y
    o_ref[...] = _layernorm(r, g_ref[...], b_ref[...])


def _mix(oa, ob, gates, x, gm, wa, wb, wo, ln_g, ln_b, tm):
    n = x.shape[0]
    nb, r, _ = gm.shape
    nt = n // tm
    tiles_per_b = nt // nb
    row = lambda i: (i, 0)
    const = lambda i: (0, 0)
    return pl.pallas_call(
        _mix_kernel,
        grid=(nt,),
        in_specs=[
            pl.BlockSpec((tm, DN_WIDTH), row),
            pl.BlockSpec((tm, ATT_WIDTH), row),
            pl.BlockSpec((tm, 2 * D_MODEL), row),
            pl.BlockSpec((tm, D_MODEL), row),
            pl.BlockSpec((None, r, D_MODEL), lambda i: (i // tiles_per_b, 0, 0)),
            pl.BlockSpec((DN_WIDTH, D_MODEL), const),
            pl.BlockSpec((ATT_WIDTH, D_MODEL), const),
            pl.BlockSpec((D_MODEL, D_MODEL), const),
            pl.BlockSpec((1, D_MODEL), const),
            pl.BlockSpec((1, D_MODEL), const),
        ],
        out_specs=pl.BlockSpec((tm, D_MODEL), row),
        out_shape=jax.ShapeDtypeStruct((n, D_MODEL), F32),
        compiler_params=_cparams(("parallel",)),
        name="mix_out",
    )(oa, ob, gates, x, gm, wa, wb, wo, ln_g, ln_b)


def _mod_kernel(x_ref, sc_ref, sh_ref, o_ref):
    o_ref[...] = (x_ref[...] * (1.0 + sc_ref[...]) + sh_ref[...]).astype(o_ref.dtype)


def _route_kernel(x_ref, sc_ref, sh_ref, wr_ref, o_ref, e_ref, g_ref):
    h = x_ref[...] * (1.0 + sc_ref[...]) + sh_ref[...]
    o_ref[...] = h.astype(o_ref.dtype)
    logits = _dot(h, wr_ref[...], HI)
    lane = lax.broadcasted_iota(I32, logits.shape, 1)
    logits = jnp.where(lane < N_EXPERTS, logits, -jnp.inf)
    e = jnp.exp(logits - jnp.max(logits, axis=-1, keepdims=True))
    p = e / jnp.sum(e, axis=-1, keepdims=True)
    p1 = jnp.max(p, axis=-1, keepdims=True)
    e1 = jnp.min(jnp.where(p == p1, lane, 128), axis=-1, keepdims=True)
    rest = jnp.where((lane == e1) | (lane >= N_EXPERTS), -1.0, p)
    p2 = jnp.max(rest, axis=-1, keepdims=True)
    e2 = jnp.min(jnp.where(rest == p2, lane, 128), axis=-1, keepdims=True)
    tot = p1 + p2
    e_ref[...] = jnp.concatenate([e1, e2], axis=1)
    g_ref[...] = jnp.concatenate([p1 / tot, p2 / tot], axis=1)


def _ffn_input(x, sc, sh, tm, w_router=None):
    n = x.shape[0]
    nb, r, _ = sc.shape
    nt = n // tm
    tiles_per_b = nt // nb
    row = lambda i: (i, 0)
    mod_spec = pl.BlockSpec((None, r, D_MODEL), lambda i: (i // tiles_per_b, 0, 0))
    in_specs = [pl.BlockSpec((tm, D_MODEL), row), mod_spec, mod_spec]
    if w_router is None:
        return pl.pallas_call(
            _mod_kernel, grid=(nt,), in_specs=in_specs,
            out_specs=pl.BlockSpec((tm, D_MODEL), row),
            out_shape=jax.ShapeDtypeStruct((n, D_MODEL), BF16),
            compiler_params=_cparams(("parallel",)), name="ffn_mod",
        )(x, sc, sh)
    wr = jnp.pad(w_router, ((0, 0), (0, 128 - N_EXPERTS)))
    return pl.pallas_call(
        _route_kernel, grid=(nt,),
        in_specs=in_specs + [pl.BlockSpec((D_MODEL, 128), lambda i: (0, 0))],
        out_specs=[pl.BlockSpec((tm, D_MODEL), row), pl.BlockSpec((tm, 2), row), pl.BlockSpec((tm, 2), row)],
        out_shape=[jax.ShapeDtypeStruct((n, D_MODEL), BF16), jax.ShapeDtypeStruct((n, 2), I32),
                   jax.ShapeDtypeStruct((n, 2), F32)],
        compiler_params=_cparams(("parallel",)), name="ffn_route",
    )(x, sc, sh, wr)


def _gffn_kernel(te_ref, nu_ref, x_ref, wg_ref, wu_ref, wd_ref, o_ref, acc):
    t = pl.program_id(0)
    f = pl.program_id(1)

    @pl.when(t < nu_ref[0])
    def _():
        x = x_ref[...]
        a = _dot(x, wg_ref[...].astype(BF16))
        b = _dot(x, wu_ref[...].astype(BF16))
        y = _dot((_silu(a) * b).astype(BF16), wd_ref[...].astype(BF16))

        @pl.when(f == 0)
        def _():
            acc[...] = y

        @pl.when(f > 0)
        def _():
            acc[...] += y

    @pl.when(f == pl.num_programs(1) - 1)
    def _():
        o_ref[...] = jnp.where(t < nu_ref[0], acc[...], 0.0)


def _grouped_ffn(x16, tile_expert, n_used, w_gate, w_up, w_down, tm, tf=256):
    n = x16.shape[0]
    nt = n // tm
    nf = D_FF // tf
    grid_spec = pltpu.PrefetchScalarGridSpec(
        num_scalar_prefetch=2,
        grid=(nt, nf),
        in_specs=[
            pl.BlockSpec((tm, D_MODEL), lambda t, f, te, nu: (t, 0)),
            pl.BlockSpec((None, D_MODEL, tf), lambda t, f, te, nu: (te[t], 0, f)),
            pl.BlockSpec((None, D_MODEL, tf), lambda t, f, te, nu: (te[t], 0, f)),
            pl.BlockSpec((None, tf, D_MODEL), lambda t, f, te, nu: (te[t], f, 0)),
        ],
        out_specs=pl.BlockSpec((tm, D_MODEL), lambda t, f, te, nu: (t, 0)),
        scratch_shapes=[pltpu.VMEM((tm, D_MODEL), F32)],
    )
    return pl.pallas_call(
        _gffn_kernel,
        grid_spec=grid_spec,
        out_shape=jax.ShapeDtypeStruct((n, D_MODEL), F32),
        compiler_params=_cparams(("arbitrary", "arbitrary")),
        name="grouped_ffn",
    )(tile_expert, n_used, x16, w_gate, w_up, w_down)


def _res_ln_kernel(x_ref, y_ref, gf_ref, g_ref, b_ref, o_ref):
    r = ALPHA * x_ref[...] + gf_ref[...] * y_ref[...]
    o_ref[...] = _layernorm(r, g_ref[...], b_ref[...])


def _res_ln2_kernel(x_ref, y0_ref, y1_ref, gt_ref, gf_ref, g_ref, b_ref, o_ref):
    gt = gt_ref[...]
    y = y0_ref[...] * gt[:, 0:1] + y1_ref[...] * gt[:, 1:2]
    r = ALPHA * x_ref[...] + gf_ref[...] * y
    o_ref[...] = _layernorm(r, g_ref[...], b_ref[...])


def _res_ln(x, ys, gate2, gf, ln_g, ln_b, tm):
    n = x.shape[0]
    nb, r, _ = gf.shape
    nt = n // tm
    tiles_per_b = nt // nb
    row = lambda i: (i, 0)
    const = lambda i: (0, 0)
    big = pl.BlockSpec((tm, D_MODEL), row)
    mod_spec = pl.BlockSpec((None, r, D_MODEL), lambda i: (i // tiles_per_b, 0, 0))
    vec = pl.BlockSpec((1, D_MODEL), const)
    if gate2 is None:
        kern, ins, specs = _res_ln_kernel, (x, ys[0], gf, ln_g, ln_b), [big, big, mod_spec, vec, vec]
    else:
        kern = _res_ln2_kernel
        ins = (x, ys[0], ys[1], gate2, gf, ln_g, ln_b)
        specs = [big, big, big, pl.BlockSpec((tm, 2), row), mod_spec, vec, vec]
    return pl.pallas_call(
        kern, grid=(nt,), in_specs=specs, out_specs=big,
        out_shape=jax.ShapeDtypeStruct((n, D_MODEL), F32),
        compiler_params=_cparams(("parallel",)), name="res_ln",
    )(*ins)


def _dense_ffn(x, sc, sh, gf, w_gate, w_up, w_down, ln_g, ln_b, tm_mod, tm_ffn):
    n = x.shape[0]
    h16 = _ffn_input(x, sc, sh, tm_mod)
    nt = n // tm_ffn
    y = _grouped_ffn(h16, jnp.zeros((nt,), I32), jnp.full((1,), nt, I32),
                     w_gate[None], w_up[None], w_down[None], tm_ffn)
    return _res_ln(x, (y,), None, gf, ln_g, ln_b, tm_mod)


def _moe_ffn(x, sc, sh, gf, w_router, w_gate, w_up, w_down, ln_g, ln_b, tm_mod, tm_e):
    n = x.shape[0]
    h16, top_e, gate2 = _ffn_input(x, sc, sh, tm_mod, w_router)
    n_assign = 2 * n
    flat_e = top_e.reshape(-1)
    order = jnp.argsort(flat_e).astype(I32)
    sorted_e = flat_e[order]
    counts = jnp.zeros((N_EXPERTS,), I32).at[flat_e].add(1)
    padded = (counts + tm_e - 1) // tm_e * tm_e
    pad_end = jnp.cumsum(padded)
    pad_start = pad_end - padded
    start = jnp.cumsum(counts) - counts
    dest = pad_start[sorted_e] + jnp.arange(n_assign, dtype=I32) - start[sorted_e]
    n_tiles = -(-n_assign // tm_e) + N_EXPERTS
    slot_token = jnp.full((n_tiles * tm_e,), n, I32).at[dest].set(order // 2)
    tile_expert = jnp.minimum(
        jnp.searchsorted(pad_end, jnp.arange(n_tiles, dtype=I32) * tm_e, side="right"), N_EXPERTS - 1).astype(I32)
    n_used = (pad_end[-1] // tm_e).astype(I32).reshape(1)
    h_pad = jnp.concatenate([h16, jnp.zeros((1, D_MODEL), h16.dtype)], axis=0)
    xb = h_pad[slot_token]
    yb = _grouped_ffn(xb, tile_expert, n_used, w_gate, w_up, w_down, tm_e)
    slot_of = jnp.zeros((n_assign,), I32).at[order].set(dest).reshape(n, 2)
    y0 = yb[slot_of[:, 0]]
    y1 = yb[slot_of[:, 1]]
    return _res_ln(x, (y0, y1), gate2, gf, ln_g, ln_b, tm_mod)


def kernel(x_prompt, x_sample, cache_k, cache_v, cache_idx_k, state_delta, state_conv, page_table, c_prompt, c_sample, w_ada, b_ada, w_in, conv_w, a_log, dt_bias, dn_norm_g, w_branch, w_out, ln1_g, ln1_b, ln2_g, ln2_b, w_ffn_gate, w_ffn_up, w_ffn_down, w_router, w_exp_gate, w_exp_up, w_exp_down):
    bp, seq, _ = x_prompt.shape
    bs = x_sample.shape[0]
    n_p = bp * seq
    n_phys = cache_k.shape[1]
    past = page_table.shape[1] * PAGE

    c_all = jnp.concatenate([c_prompt, c_sample], axis=0)
    c_rows = -(-c_all.shape[0] // 8) * 8
    c_all = jnp.pad(c_all, ((0, c_rows - c_all.shape[0]), (0, 0)))
    mods = _ada_all(c_all, w_ada, b_ada)

    cache_k4 = cache_k.reshape(DEPTH, n_phys, PAGE, ATT_WIDTH)
    cache_v4 = cache_v.reshape(DEPTH, n_phys, PAGE, ATT_WIDTH)

    xp = x_prompt.reshape(n_p, D_MODEL)
    xs = x_sample.reshape(bs, D_MODEL)
    lane = jnp.arange(128)
    outs_p, outs_s = [], []
    for l in range(DEPTH):
        j = l // 2
        mod_p = [mods[l, :bp, u * D_MODEL:(u + 1) * D_MODEL].reshape(bp, 1, D_MODEL) for u in range(6)]
        mod_s = [mods[l, bp:bp + bs, u * D_MODEL:(u + 1) * D_MODEL].reshape(1, bs, D_MODEL) for u in range(6)]
        w_groups = _split_w_in(w_in[l])
        head_lane = jnp.clip(lane - SM_A, 0, DN_HEADS - 1)
        in_a = (lane >= SM_A) & (lane < SM_A + DN_HEADS)
        alog_row = jnp.where(in_a, a_log[l][head_lane], 0.0).reshape(1, 128)
        dt_row = jnp.where(in_a, dt_bias[l][head_lane], 0.0).reshape(1, 128)
        norm_g = dn_norm_g[l].reshape(1, DN_DIM)
        wa16 = w_branch[l, 0].astype(BF16)
        wb16 = w_branch[l, 1].astype(BF16)
        wo16 = w_out[l].astype(BF16)
        ln1 = (ln1_g[l].reshape(1, -1), ln1_b[l].reshape(1, -1))
        ln2 = (ln2_g[l].reshape(1, -1), ln2_b[l].reshape(1, -1))

        pr = _in_proj(xp, mod_p[1], mod_p[0], w_groups, tm=256)
        oa, s_new_p = _delta_prompt(pr["qkv_a"], pr["z"], pr["small"], conv_w[l], alog_row, dt_row, norm_g, bp)
        ik16 = pr["small"][:, SM_IK:SM_IK + IDX_DIM].astype(BF16)
        ob = _attn_prompt(pr["q_b"], pr["iq"], pr["small"], pr["k_b16"], pr["v_b16"], ik16, bp)
        x1 = _mix(oa, ob, pr["gates"], xp, mod_p[2], wa16, wb16, wo16, *ln1, tm=512)
        if l % 2 == 0:
            xp = _dense_ffn(x1, mod_p[4], mod_p[3], mod_p[5], w_ffn_gate[j], w_ffn_up[j], w_ffn_down[j], *ln2,
                            tm_mod=512, tm_ffn=1024)
        else:
            xp = _moe_ffn(x1, mod_p[4], mod_p[3], mod_p[5], w_router[j], w_exp_gate[j], w_exp_up[j], w_exp_down[j],
                          *ln2, tm_mod=512, tm_e=1024)
        conv_p = pr["qkv_a"].reshape(bp, seq, -1)[:, seq - (CONV_W - 1):, :]
        outs_p.append((conv_p, s_new_p, pr["k_b"].reshape(bp, seq, ATT_HEADS, ATT_DIM),
                       pr["v_b"].reshape(bp, seq, ATT_HEADS, ATT_DIM),
                       pr["small"][:, SM_IK:SM_IK + IDX_DIM].reshape(bp, seq, IDX_DIM)))

        sr = _in_proj(xs, mod_s[1], mod_s[0], w_groups, tm=bs)
        oa_s, s_new_s = _delta_sample(sr["qkv_a"], state_conv[l], sr["z"], sr["small"], conv_w[l], alog_row, dt_row,
                                      norm_g, state_delta[l])
        ik_s = sr["small"][:, SM_IK:SM_IK + IDX_DIM]
        iq3 = sr["iq"].astype(F32).reshape(bs, IDX_HEADS, IDX_DIM)
        iw3 = sr["small"][:, SM_IW:SM_IW + IDX_HEADS].reshape(bs, IDX_HEADS, 1)
        scores = _sample_scores(l, page_table, iq3, iw3, ik_s.reshape(bs, 1, IDX_DIM), cache_idx_k)
        selb = _sample_select(scores.reshape(bs, -1)).reshape(scores.shape)
        ob_s = _sample_attend(l, page_table, sr["q_b"].astype(F32).reshape(bs, 1, ATT_WIDTH),
                              sr["k_b"].reshape(bs, 1, ATT_WIDTH), sr["v_b"].reshape(bs, 1, ATT_WIDTH),
                              selb, cache_k4, cache_v4).reshape(bs, ATT_WIDTH)
        x1s = _mix(oa_s, ob_s, sr["gates"], xs, mod_s[2], wa16, wb16, wo16, *ln1, tm=bs)
        if l % 2 == 0:
            xs = _dense_ffn(x1s, mod_s[4], mod_s[3], mod_s[5], w_ffn_gate[j], w_ffn_up[j], w_ffn_down[j], *ln2,
                            tm_mod=bs, tm_ffn=bs)
        else:
            xs = _moe_ffn(x1s, mod_s[4], mod_s[3], mod_s[5], w_router[j], w_exp_gate[j], w_exp_up[j], w_exp_down[j],
                          *ln2, tm_mod=bs, tm_e=64)
        conv_s = jnp.concatenate([state_conv[l][:, 1:, :], sr["qkv_a"][:, None, :]], axis=1)
        outs_s.append((conv_s, s_new_s, sr["k_b"].reshape(bs, 1, ATT_HEADS, ATT_DIM),
                       sr["v_b"].reshape(bs, 1, ATT_HEADS, ATT_DIM), ik_s.reshape(bs, 1, IDX_DIM)))

    stack = lambda rows, u: jnp.stack([r[u] for r in rows])
    return (xp.reshape(bp, seq, D_MODEL), xs.reshape(bs, 1, D_MODEL),
            stack(outs_p, 2), stack(outs_p, 3), stack(outs_p, 4), stack(outs_p, 1), stack(outs_p, 0),
            stack(outs_s, 2), stack(outs_s, 3), stack(outs_s, 4), stack(outs_s, 1), stack(outs_s, 0))
```

```python
import functools

import jax
import jax.numpy as jnp
from jax import lax
from jax.experimental import pallas as pl
from jax.experimental.pallas import tpu as pltpu

F32 = jnp.float32
BF16 = jnp.bfloat16
I32 = jnp.int32
HI = lax.Precision.HIGHEST

D_MODEL = 1024
DEPTH = 4
PAGE = 128
DN_HEADS = 4
DN_DIM = 128
DN_WIDTH = DN_HEADS * DN_DIM
CONV_W = 4
DN_CHUNK = 64
ATT_HEADS = 4
ATT_DIM = 128
ATT_WIDTH = ATT_HEADS * ATT_DIM
IDX_HEADS = 8
IDX_DIM = 64
TOPK = 256
D_FF = 2816
N_EXPERTS = 8
ALPHA = (2.0 * DEPTH) ** 0.25
EPS = 1e-5
NEG = -1e30
LOG2E = 1.4426950408889634

SM_IK = 0
SM_B = 64
SM_A = 68
SM_IW = 72

VMEM_LIMIT = 56 * 1024 * 1024


def _cparams(sem):
    return pltpu.CompilerParams(dimension_semantics=sem, vmem_limit_bytes=VMEM_LIMIT)


def _sigmoid(x):
    return jax.nn.sigmoid(x)


def _silu(x):
    return x * jax.nn.sigmoid(x)


def _softplus(x):
    return jnp.maximum(x, 0.0) + jnp.log1p(jnp.exp(-jnp.abs(x)))


def _dot(a, b, precision=None):
    return jnp.dot(a, b, precision=precision, preferred_element_type=F32)


def _dot_nt(a, b, precision=None):
    return lax.dot_general(a, b, (((1,), (1,)), ((), ())), precision=precision, preferred_element_type=F32)


def _dot_tn(a, b, precision=None):
    return lax.dot_general(a, b, (((0,), (0,)), ((), ())), precision=precision, preferred_element_type=F32)


def _split(a):
    hi = a.astype(BF16)
    return hi, (a - hi.astype(F32)).astype(BF16)


def _dot3(a, b):
    return _dot(a[0], b[0]) + (_dot(a[0], b[1]) + _dot(a[1], b[0]))


def _layernorm(r, g, b):
    mu = jnp.mean(r, axis=-1, keepdims=True)
    rc = r - mu
    var = jnp.mean(rc * rc, axis=-1, keepdims=True)
    return rc * lax.rsqrt(var + EPS) * g + b


def _ada_kernel(c_ref, w_ref, b_ref, o_ref):
    o_ref[...] = _dot(_silu(c_ref[...]), w_ref[...], HI) + b_ref[...]


def _ada_all(c, w_ada, b_ada):
    rows = c.shape[0]
    nj = w_ada.shape[2] // D_MODEL
    return pl.pallas_call(
        _ada_kernel,
        grid=(DEPTH, nj),
        in_specs=[
            pl.BlockSpec((rows, D_MODEL), lambda l, j: (0, 0)),
            pl.BlockSpec((None, D_MODEL, D_MODEL), lambda l, j: (l, 0, j)),
            pl.BlockSpec((None, 1, D_MODEL), lambda l, j: (l, 0, j)),
        ],
        out_specs=pl.BlockSpec((None, rows, D_MODEL), lambda l, j: (l, 0, j)),
        out_shape=jax.ShapeDtypeStruct((DEPTH, rows, nj * D_MODEL), F32),
        compiler_params=_cparams(("arbitrary", "arbitrary")),
        name="ada_mod",
    )(c, w_ada, b_ada.reshape(DEPTH, 1, -1))


IN_GROUPS = (
    ("qkv_a", 3 * DN_WIDTH, (F32,)),
    ("z", DN_WIDTH, (F32,)),
    ("q_b", ATT_WIDTH, (BF16,)),
    ("k_b", ATT_WIDTH, (F32, BF16)),
    ("v_b", ATT_WIDTH, (F32, BF16)),
    ("iq", IDX_HEADS * IDX_DIM, (BF16,)),
    ("gates", 2 * D_MODEL, (F32,)),
    ("small", 128, (F32,)),
)


def _split_w_in(w):
    o = 0
    parts = {}
    for name, width in (("qkv_a", 3 * DN_WIDTH), ("z", DN_WIDTH), ("b", DN_HEADS), ("a", DN_HEADS),
                        ("q_b", ATT_WIDTH), ("k_b", ATT_WIDTH), ("v_b", ATT_WIDTH),
                        ("iq", IDX_HEADS * IDX_DIM), ("ik", IDX_DIM), ("iw", IDX_HEADS), ("gates", 2 * D_MODEL)):
        parts[name] = w[:, o:o + width]
        o += width
    small = jnp.concatenate(
        [parts["ik"], parts["b"], parts["a"], parts["iw"],
         jnp.zeros((w.shape[0], 128 - IDX_DIM - 2 * DN_HEADS - IDX_HEADS), w.dtype)], axis=1)
    parts["small"] = small
    return [parts[name].astype(BF16) for name, _, _ in IN_GROUPS]


def _in_kernel(x_ref, sc_ref, sh_ref, *refs):
    n = len(IN_GROUPS)
    w_refs, o_refs = refs[:n], refs[n:]
    h = (x_ref[...] * (1.0 + sc_ref[...]) + sh_ref[...]).astype(BF16)
    k = 0
    for (name, width, dtypes), w_ref in zip(IN_GROUPS, w_refs):
        y = _dot(h, w_ref[...])
        for dt in dtypes:
            o_refs[k][...] = y.astype(dt)
            k += 1


def _in_proj(x, sc, sh, w_groups, tm):
    n = x.shape[0]
    nb, r, _ = sc.shape
    nt = n // tm
    tiles_per_b = nt // nb
    mod_spec = pl.BlockSpec((None, r, D_MODEL), lambda i: (i // tiles_per_b, 0, 0))
    in_specs = [pl.BlockSpec((tm, D_MODEL), lambda i: (i, 0)), mod_spec, mod_spec]
    for (name, width, _), w in zip(IN_GROUPS, w_groups):
        in_specs.append(pl.BlockSpec((D_MODEL, width), lambda i: (0, 0), pipeline_mode=pl.Buffered(1)))
    out_specs, out_shape = [], []
    for name, width, dtypes in IN_GROUPS:
        for dt in dtypes:
            out_specs.append(pl.BlockSpec((tm, width), lambda i: (i, 0)))
            out_shape.append(jax.ShapeDtypeStruct((n, width), dt))
    outs = pl.pallas_call(
        _in_kernel,
        grid=(nt,),
        in_specs=in_specs,
        out_specs=out_specs,
        out_shape=out_shape,
        compiler_params=_cparams(("parallel",)),
        name="in_proj",
    )(x, sc, sh, *w_groups)
    names = []
    for name, _, dtypes in IN_GROUPS:
        for dt in dtypes:
            names.append(name if dt == dtypes[0] else name + "16")
    return dict(zip(names, outs))


def _delta_kernel(qkv_ref, z_ref, sm_ref, cw_ref, alog_ref, dt_ref, ng_ref, o_ref, s_out_ref, xs, s_sc, *, tt):
    i = pl.program_id(1)
    nc = tt // DN_CHUNK
    c = DN_CHUNK

    @pl.when(i == 0)
    def _():
        xs[0:8, :] = jnp.zeros((8, 3 * DN_WIDTH), F32)
        s_sc[...] = jnp.zeros_like(s_sc)

    xs[8:8 + tt, :] = qkv_ref[...]
    cw = cw_ref[...]
    y = xs[8:8 + tt, :] * cw[3:4, :]
    for j in range(1, CONV_W):
        y = y + xs[8 - j:8 - j + tt, :] * cw[3 - j:4 - j, :]
    act = _silu(y)
    xs[0:8, :] = xs[tt:tt + 8, :]

    sm = sm_ref[...]
    beta_all = _sigmoid(sm)
    g_all = -jnp.exp(alog_ref[...]) * _softplus(sm + dt_ref[...])
    r_io = lax.broadcasted_iota(I32, (tt, tt), 0)
    c_io = lax.broadcasted_iota(I32, (tt, tt), 1)
    lblk = jnp.where((r_io // c == c_io // c) & (r_io >= c_io), 1.0, 0.0).astype(F32)
    gcum = _dot(lblk, g_all, HI)
    gcum_t = gcum.T

    ri = lax.broadcasted_iota(I32, (c, c), 0)
    ci = lax.broadcasted_iota(I32, (c, c), 1)
    incl = ri >= ci
    strict = ri > ci
    eye = jnp.where(ri == ci, 1.0, 0.0).astype(F32)
    ng = ng_ref[...]

    blocks = [(h, cc) for h in range(DN_HEADS) for cc in range(nc)]
    qs, ks, gis, egs, dmats, qks, nms, rhss = {}, {}, {}, {}, {}, {}, {}, {}
    for h in range(DN_HEADS):
        lo, hi = h * DN_DIM, (h + 1) * DN_DIM
        q_h = act[:, lo:hi]
        k_h = act[:, DN_WIDTH + lo:DN_WIDTH + hi]
        v_h = act[:, 2 * DN_WIDTH + lo:2 * DN_WIDTH + hi]
        q_h = q_h * lax.rsqrt(jnp.sum(q_h * q_h, axis=-1, keepdims=True) + 1e-6) * (DN_DIM ** -0.5)
        k_h = k_h * lax.rsqrt(jnp.sum(k_h * k_h, axis=-1, keepdims=True) + 1e-6)
        for cc in range(nc):
            r0, r1 = cc * c, (cc + 1) * c
            qc, kc, vc = q_h[r0:r1], k_h[r0:r1], v_h[r0:r1]
            gi = gcum[r0:r1, SM_A + h:SM_A + h + 1]
            gj = gcum_t[SM_A + h:SM_A + h + 1, r0:r1]
            bi = beta_all[r0:r1, SM_B + h:SM_B + h + 1]
            dmat = jnp.exp(jnp.where(incl, gi - gj, -jnp.inf))
            eg = jnp.exp(gi)
            key = (h, cc)
            qs[key], ks[key], gis[key], egs[key], dmats[key] = qc, kc, gi, eg, dmat
            qks[key] = _dot_nt(qc, kc) * dmat
            nms[key] = jnp.where(strict, -(bi * _dot_nt(kc, kc) * dmat), 0.0)
            rhss[key] = jnp.concatenate([vc * bi, kc * (bi * eg)], axis=1)

    tms = {key: eye + nms[key] for key in blocks}
    ps = {key: _split(nms[key]) for key in blocks}
    for _ in range(5):
        for key in blocks:
            p2 = _dot3(ps[key], ps[key])
            ps[key] = _split(p2)
            tms[key] = tms[key] + _dot3(_split(tms[key]), ps[key])
    sols = {key: _dot3(_split(tms[key]), _split(rhss[key])) for key in blocks}

    states = [s_sc[h] for h in range(DN_HEADS)]
    for cc in range(nc):
        r0, r1 = cc * c, (cc + 1) * c
        for h in range(DN_HEADS):
            lo, hi = h * DN_DIM, (h + 1) * DN_DIM
            key = (h, cc)
            s_h = states[h]
            gi, eg = gis[key], egs[key]
            u, w = sols[key][:, :DN_DIM], sols[key][:, DN_DIM:]
            v_new = u - _dot(w, s_h)
            o = _dot(qs[key] * eg, s_h) + _dot(qks[key], v_new)
            g_last = gi[c - 1:c, :]
            states[h] = s_h * jnp.exp(g_last) + _dot_tn(ks[key] * jnp.exp(g_last - gi), v_new)
            o = o * lax.rsqrt(jnp.mean(o * o, axis=-1, keepdims=True) + EPS) * ng
            o_ref[r0:r1, lo:hi] = o * _silu(z_ref[r0:r1, lo:hi])
    for h in range(DN_HEADS):
        s_sc[h] = states[h]

    @pl.when(i == pl.num_programs(1) - 1)
    def _():
        s_out_ref[...] = s_sc[...]


def _delta_prompt(qkv, z, small, conv_w, alog_row, dt_row, norm_g, nb, tt=256):
    n = qkv.shape[0]
    nt = n // nb // tt
    row = lambda b, i: (b * nt + i, 0)
    const = lambda b, i: (0, 0)
    return pl.pallas_call(
        functools.partial(_delta_kernel, tt=tt),
        grid=(nb, nt),
        in_specs=[
            pl.BlockSpec((tt, 3 * DN_WIDTH), row),
            pl.BlockSpec((tt, DN_WIDTH), row),
            pl.BlockSpec((tt, 128), row),
            pl.BlockSpec((CONV_W, 3 * DN_WIDTH), const),
            pl.BlockSpec((1, 128), const),
            pl.BlockSpec((1, 128), const),
            pl.BlockSpec((1, DN_DIM), const),
        ],
        out_specs=[
            pl.BlockSpec((tt, DN_WIDTH), row),
            pl.BlockSpec((None, DN_HEADS, DN_DIM, DN_DIM), lambda b, i: (b, 0, 0, 0)),
        ],
        out_shape=[
            jax.ShapeDtypeStruct((n, DN_WIDTH), F32),
            jax.ShapeDtypeStruct((nb, DN_HEADS, DN_DIM, DN_DIM), F32),
        ],
        scratch_shapes=[
            pltpu.VMEM((tt + 8, 3 * DN_WIDTH), F32),
            pltpu.VMEM((DN_HEADS, DN_DIM, DN_DIM), F32),
        ],
        compiler_params=_cparams(("arbitrary", "arbitrary")),
        name="delta_prompt",
    )(qkv, z, small, conv_w, alog_row, dt_row, norm_g)


def _sdelta_kernel(qkv_ref, buf_ref, z_ref, sm_ref, cw_ref, alog_ref, dt_ref, ng_ref, s_ref, o_ref, s_out_ref, *, rows):
    cw = cw_ref[...]
    y = qkv_ref[...] * cw[3:4, :]
    for j in range(CONV_W - 1):
        y = y + buf_ref[:, j, :] * cw[j:j + 1, :]
    act = _silu(y)
    sm = sm_ref[...]
    beta_all = _sigmoid(sm)
    g_all = -jnp.exp(alog_ref[...]) * _softplus(sm + dt_ref[...])
    ng = ng_ref[...]
    sub = lax.broadcasted_iota(I32, (8, DN_DIM), 0)
    for h in range(DN_HEADS):
        lo, hi = h * DN_DIM, (h + 1) * DN_DIM
        q_h = act[:, lo:hi]
        k_h = act[:, DN_WIDTH + lo:DN_WIDTH + hi]
        v_h = act[:, 2 * DN_WIDTH + lo:2 * DN_WIDTH + hi]
        q_h = q_h * lax.rsqrt(jnp.sum(q_h * q_h, axis=-1, keepdims=True) + 1e-6) * (DN_DIM ** -0.5)
        k_h = k_h * lax.rsqrt(jnp.sum(k_h * k_h, axis=-1, keepdims=True) + 1e-6)
        g = g_all[:, SM_A + h:SM_A + h + 1]
        b = beta_all[:, SM_B + h:SM_B + h + 1]
        eg = jnp.exp(g)
        u = v_h * b
        w = k_h * (b * eg)
        qe = q_h * eg
        qk = jnp.sum(q_h * k_h, axis=-1, keepdims=True)
        z_h = z_ref[:, lo:hi]
        for r in range(rows):
            s = s_ref[r, h]
            w8 = jnp.where(sub == 0, w[r:r + 1], 0.0)
            q8 = jnp.where(sub == 0, qe[r:r + 1], 0.0)
            k8 = jnp.where(sub == 0, k_h[r:r + 1], 0.0)
            v_new = u[r:r + 1] - _dot(w8, s, HI)[0:1]
            o = _dot(q8, s, HI)[0:1] + qk[r:r + 1] * v_new
            vn8 = jnp.where(sub == 0, v_new, 0.0)
            s_out_ref[r, h] = s * eg[r:r + 1] + _dot_tn(k8, vn8, HI)
            o = o * lax.rsqrt(jnp.mean(o * o, axis=-1, keepdims=True) + EPS) * ng
            o_ref[r:r + 1, lo:hi] = o * _silu(z_h[r:r + 1])


def _delta_sample(qkv, buf, z, small, conv_w, alog_row, dt_row, norm_g, state, rows=8):
    n = qkv.shape[0]
    row = lambda i: (i, 0)
    const = lambda i: (0, 0)
    return pl.pallas_call(
        functools.partial(_sdelta_kernel, rows=rows),
        grid=(n // rows,),
        in_specs=[
            pl.BlockSpec((rows, 3 * DN_WIDTH), row),
            pl.BlockSpec((rows, CONV_W - 1, 3 * DN_WIDTH), lambda i: (i, 0, 0)),
            pl.BlockSpec((rows, DN_WIDTH), row),
            pl.BlockSpec((rows, 128), row),
            pl.BlockSpec((CONV_W, 3 * DN_WIDTH), const),
            pl.BlockSpec((1, 128), const),
            pl.BlockSpec((1, 128), const),
            pl.BlockSpec((1, DN_DIM), const),
            pl.BlockSpec((rows, DN_HEADS, DN_DIM, DN_DIM), lambda i: (i, 0, 0, 0)),
        ],
        out_specs=[
            pl.BlockSpec((rows, DN_WIDTH), row),
            pl.BlockSpec((rows, DN_HEADS, DN_DIM, DN_DIM), lambda i: (i, 0, 0, 0)),
        ],
        out_shape=[
            jax.ShapeDtypeStruct((n, DN_WIDTH), F32),
            jax.ShapeDtypeStruct(state.shape, F32),
        ],
        compiler_params=_cparams(("parallel",)),
        name="delta_sample",
    )(qkv, buf, z, small, conv_w, alog_row, dt_row, norm_g, state)


def _score_keys(s):
    s = jnp.where(s == 0.0, 0.0, s)
    k = pltpu.bitcast(s, I32)
    return jnp.where(k < 0, k ^ jnp.int32(0x7FFFFFFF), k)


SEARCH_ROUND = 4


def _alibi_slope(h):
    return 2.0 ** (-8.0 * (h + 1) / ATT_HEADS)


def _sublane_allreduce(x, op):
    for shift in (4, 2, 1):
        x = op(x, pltpu.roll(x, shift, axis=0))
    return x


def _attn_kernel(q_ref, iq_ref, sm_ref, k_ref, vt_ref, ik_ref, o_ref, keys, alibi, acc_sc, *, tq, ck):
    i = pl.program_id(1)
    nk = (i * tq + tq + ck - 1) // ck
    g8 = ck // 8
    t = i * tq + lax.broadcasted_iota(I32, (8, tq), 1)
    sub = lax.broadcasted_iota(I32, (ck, 1), 0)
    sub3 = lax.broadcasted_iota(I32, (g8, 8, 1), 0) * 8 + lax.broadcasted_iota(I32, (g8, 8, 1), 1)

    @pl.when((pl.program_id(0) == 0) & (i == 0))
    def _():
        for h in range(ATT_HEADS):
            alibi[h] = jnp.broadcast_to(sub.astype(F32) * (_alibi_slope(h) * LOG2E), (ck, tq))

    sm_t = sm_ref[...].T
    iw = [jnp.broadcast_to(sm_t[SM_IW + h:SM_IW + h + 1, :] * (IDX_HEADS ** -0.5 * IDX_DIM ** -0.5), (8, tq))
          for h in range(IDX_HEADS)]
    iq = iq_ref[...]

    def score_chunk(c, diagonal):
        off = pl.multiple_of(c * ck, ck)
        d = _dot_nt(ik_ref[pl.ds(off, ck), :], iq)
        s = jnp.maximum(d[:, 0:tq].reshape(g8, 8, tq), 0.0) * iw[0][None]
        for h in range(1, IDX_HEADS):
            s = s + jnp.maximum(d[:, h * tq:(h + 1) * tq].reshape(g8, 8, tq), 0.0) * iw[h][None]
        if diagonal:
            s = jnp.where(off + sub3 <= t[None], s, -jnp.inf)
        keys[c] = _score_keys(s)

    def score_full(c, carry):
        score_chunk(c, False)
        return carry

    lax.fori_loop(0, nk - 1, score_full, 0)
    score_chunk(nk - 1, True)

    kth = jnp.minimum(t + 1, TOPK)

    def count(pred_fn):
        def body(c, cnt):
            return cnt + jnp.sum(jnp.where(pred_fn(keys[c], c), 1, 0).astype(I32), axis=0)
        return _sublane_allreduce(lax.fori_loop(0, nk, body, jnp.zeros((8, tq), I32)), jnp.add)

    def search_cond(st):
        return (st[0] < 32) & (st[3] > 0)

    def search_round(st):
        it, thr, cnt_thr, _ = st
        for u in range(SEARCH_ROUND):
            cand = thr ^ jnp.left_shift(jnp.int32(1), 31 - u - it)
            cnt = count(lambda kc, c: kc >= cand[None])
            ok = cnt >= kth
            thr = jnp.where(ok, cand, thr)
            cnt_thr = jnp.where(ok, cnt, cnt_thr)
        return it + SEARCH_ROUND, thr, cnt_thr, jnp.max(jnp.where(cnt_thr != kth, 1, 0))

    _, thr, _, unresolved = lax.while_loop(
        search_cond, search_round,
        (jnp.int32(0), jnp.full((8, tq), -2 ** 31, I32), jnp.full((8, tq), 2 ** 30, I32), jnp.int32(1)))

    @pl.when(unresolved > 0)
    def _():
        need = kth - count(lambda kc, c: kc > thr[None])

        def pos_step(it, cut):
            cand = cut | jnp.left_shift(jnp.int32(1), 13 - it)
            cnt = count(lambda kc, c: (kc == thr[None]) & (c * ck + sub3 < cand[None]))
            return jnp.where(cnt < need, cand, cut)
        cut = lax.fori_loop(0, 14, pos_step, jnp.zeros((8, tq), I32))

        def drop(c, carry):
            kc = keys[c]
            keys[c] = jnp.where((kc == thr[None]) & (c * ck + sub3 > cut[None]), jnp.int32(-2 ** 31), kc)
            return carry
        lax.fori_loop(0, nk, drop, 0)

    acc_sc[...] = jnp.zeros_like(acc_sc)
    q = q_ref[...]
    qh = [q[:, h * ATT_DIM:(h + 1) * ATT_DIM] for h in range(ATT_HEADS)]
    heads = range(ATT_HEADS)

    def attend_chunk(c, carry):
        ms, ls = carry
        off = pl.multiple_of(c * ck, ck)
        selb = jnp.where(keys[c] >= thr[None], 0.0, NEG).reshape(ck, tq)
        kc = k_ref[pl.ds(off, ck), :]
        off_f = off.astype(F32)
        c_off = [off_f * (_alibi_slope(h) * LOG2E) for h in heads]
        zs = [(_dot_nt(kc[:, h * ATT_DIM:(h + 1) * ATT_DIM], qh[h]) * (ATT_DIM ** -0.5 * LOG2E)
               + (alibi[h] + selb)).reshape(g8, 8, tq) for h in heads]
        m_new = [jnp.maximum(ms[h], _sublane_allreduce(jnp.max(zs[h], axis=0), jnp.maximum) + c_off[h]) for h in heads]
        a = [jnp.exp2(ms[h] - m_new[h]) for h in heads]
        ps = [jnp.exp2(zs[h] - (m_new[h] - c_off[h])[None]) for h in heads]
        l_new = [a[h] * ls[h] + _sublane_allreduce(jnp.sum(ps[h], axis=0), jnp.add) for h in heads]
        for h in heads:
            pv = _dot(vt_ref[c, h * ATT_DIM:(h + 1) * ATT_DIM, :], ps[h].reshape(ck, tq).astype(BF16))
            acc_sc[h] = a[h][None] * acc_sc[h] + pv.reshape(ATT_DIM // 8, 8, tq)
        return tuple(m_new), tuple(l_new)

    init = (tuple(jnp.full((8, tq), NEG, F32) for _ in heads), tuple(jnp.zeros((8, tq), F32) for _ in heads))
    _, ls = lax.fori_loop(0, nk, attend_chunk, init)
    for h in heads:
        o_ref[:, h * ATT_DIM:(h + 1) * ATT_DIM] = (acc_sc[h] / ls[h][None]).reshape(ATT_DIM, tq).T


def _attn_prompt(q16, iq16, small, k16, vt16, ik16, nb, tq=128, ck=512):
    n = q16.shape[0]
    s = n // nb
    nq = s // tq
    row = lambda b, i: (b * nq + i, 0)
    full = lambda b, i: (b, 0)
    once = pl.Buffered(1)
    return pl.pallas_call(
        functools.partial(_attn_kernel, tq=tq, ck=ck),
        grid=(nb, nq),
        in_specs=[
            pl.BlockSpec((tq, ATT_WIDTH), row),
            pl.BlockSpec((IDX_HEADS * tq, IDX_DIM), row),
            pl.BlockSpec((tq, 128), row),
            pl.BlockSpec((s, ATT_WIDTH), full, pipeline_mode=once),
            pl.BlockSpec((None, s // ck, ATT_WIDTH, ck), lambda b, i: (b, 0, 0, 0), pipeline_mode=once),
            pl.BlockSpec((s, IDX_DIM), full, pipeline_mode=once),
        ],
        out_specs=pl.BlockSpec((tq, ATT_WIDTH), row),
        out_shape=jax.ShapeDtypeStruct((n, ATT_WIDTH), F32),
        scratch_shapes=[
            pltpu.VMEM((s // ck, ck // 8, 8, tq), I32),
            pltpu.VMEM((ATT_HEADS, ck, tq), F32),
            pltpu.VMEM((ATT_HEADS, ATT_DIM // 8, 8, tq), F32),
        ],
        compiler_params=_cparams(("arbitrary", "arbitrary")),
        name="attn_prompt",
    )(q16, iq16, small, k16, vt16, ik16)


PG = 8


def _sscore_kernel(pt_ref, iq_ref, iw_ref, ikn_ref, *refs):
    pages, o_ref = refs[:PG], refs[PG]
    j = pl.program_id(1)
    iq = iq_ref[...]
    iw = iw_ref[...] * (IDX_HEADS ** -0.5)

    @pl.when(j < pl.num_programs(1) - 1)
    def _():
        for u in range(PG):
            d = _dot_nt(iq, pages[u][...]) * (IDX_DIM ** -0.5)
            o_ref[:, u * PAGE:(u + 1) * PAGE] = jnp.sum(jnp.maximum(d, 0.0) * iw, axis=0, keepdims=True)

    @pl.when(j == pl.num_programs(1) - 1)
    def _():
        d = _dot_nt(iq, jnp.broadcast_to(ikn_ref[...], (8, IDX_DIM)))[:, 0:1] * (IDX_DIM ** -0.5)
        s_new = jnp.sum(jnp.maximum(d, 0.0) * iw, axis=0, keepdims=True)
        lane = lax.broadcasted_iota(I32, (1, PG * PAGE), 1)
        o_ref[...] = jnp.where(lane == 0, s_new, -jnp.inf)


def _sample_scores(layer, page_table, iq3, iw3, ik_new3, cache_idx_k):
    nbatch, n_pages = page_table.shape
    nj = n_pages // PG

    def page_map(u):
        return lambda b, j, pt: (layer, pt[b, jnp.minimum(j * PG + u, n_pages - 1)], 0, 0)

    grid_spec = pltpu.PrefetchScalarGridSpec(
        num_scalar_prefetch=1,
        grid=(nbatch, nj + 1),
        in_specs=[
            pl.BlockSpec((None, IDX_HEADS, IDX_DIM), lambda b, j, pt: (b, 0, 0)),
            pl.BlockSpec((None, IDX_HEADS, 1), lambda b, j, pt: (b, 0, 0)),
            pl.BlockSpec((None, 1, IDX_DIM), lambda b, j, pt: (b, 0, 0)),
        ] + [pl.BlockSpec((None, None, PAGE, IDX_DIM), page_map(u)) for u in range(PG)],
        out_specs=pl.BlockSpec((None, 1, PG * PAGE), lambda b, j, pt: (b, 0, j)),
    )
    return pl.pallas_call(
        _sscore_kernel,
        grid_spec=grid_spec,
        out_shape=jax.ShapeDtypeStruct((nbatch, 1, (nj + 1) * PG * PAGE), F32),
        compiler_params=_cparams(("arbitrary", "arbitrary")),
        name="sample_scores",
    )(page_table, iq3, iw3, ik_new3, *([cache_idx_k] * PG))


def _sselect_kernel(s_ref, o_ref):
    keys = _score_keys(s_ref[...])
    rows, width = keys.shape
    pos = lax.broadcasted_iota(I32, (1, width), 1)

    def cnt(m):
        return jnp.sum(jnp.where(m, 1, 0).astype(I32), axis=-1, keepdims=True)

    def search_step(it, thr):
        cand = thr ^ jnp.left_shift(jnp.int32(1), 31 - it)
        return jnp.where(cnt(keys >= cand) >= TOPK, cand, thr)

    thr = lax.fori_loop(0, 32, search_step, jnp.full((rows, 1), -2 ** 31, I32))
    need = TOPK - cnt(keys > thr)
    eq = keys == thr

    def pos_step(it, cut):
        cand = cut | jnp.left_shift(jnp.int32(1), 14 - it)
        return jnp.where(cnt(eq & (pos < cand)) < need, cand, cut)

    cut = lax.fori_loop(0, 15, pos_step, jnp.zeros((rows, 1), I32))
    sel = (keys > thr) | (eq & (pos <= cut))
    o_ref[...] = jnp.where(sel, 0.0, NEG)


def _sample_select(scores):
    return pl.pallas_call(
        _sselect_kernel,
        out_shape=jax.ShapeDtypeStruct(scores.shape, F32),
        compiler_params=pltpu.CompilerParams(vmem_limit_bytes=VMEM_LIMIT),
        name="sample_select",
    )(scores)


PAGE_ROWS = PAGE * ATT_HEADS


def _sattn_kernel(pt_ref, q_ref, kn_ref, vn_ref, selb_ref, seln_ref, *refs, past):
    kp, vp = refs[:PG], refs[PG:2 * PG]
    o_ref, lg_sc, l_sc, pn_sc, acc_sc = refs[2 * PG:2 * PG + 5]
    ph = pl.program_id(1)
    j = pl.program_id(2)
    scale = ATT_DIM ** -0.5
    q8 = q_ref[...]
    row = lax.broadcasted_iota(I32, (8, 1), 0)
    lane = lax.broadcasted_iota(I32, (1, PAGE_ROWS), 1)
    slope = jnp.exp2(-8.0 * (row + 1).astype(F32) / ATT_HEADS)
    own = jnp.bitwise_and(lane, ATT_HEADS - 1) == row
    tok = jnp.right_shift(lane, 2)

    @pl.when(ph == 0)
    def _():
        for u in range(PG):
            dist = (past - ((j * PG + u) * PAGE + tok)).astype(F32)
            lg = _dot_nt(q8, kp[u][...]) * scale - slope * dist + selb_ref[:, u * PAGE_ROWS:(u + 1) * PAGE_ROWS]
            lg_sc[j * PG + u] = jnp.where(own, lg, NEG)

    @pl.when((ph == 1) & (j == 0))
    def _():
        allg = lg_sc[...]
        lgn = jnp.sum(q8 * kn_ref[...], axis=-1, keepdims=True) * scale + seln_ref[:, 0:1]
        m = jnp.maximum(jnp.max(jnp.max(allg, axis=0), axis=-1, keepdims=True), lgn)
        p = jnp.exp(allg - m)
        lg_sc[...] = p
        pn = jnp.exp(lgn - m)
        l_sc[...] = jnp.sum(jnp.sum(p, axis=0), axis=-1, keepdims=True) + pn
        pn_sc[...] = pn
        acc_sc[...] = jnp.zeros_like(acc_sc)

    @pl.when(ph == 1)
    def _():
        acc = acc_sc[...]
        for u in range(PG):
            acc = acc + _dot(lg_sc[j * PG + u], vp[u][...])
        acc_sc[...] = acc

    @pl.when((ph == 1) & (j == pl.num_programs(2) - 1))
    def _():
        o_ref[...] = (acc_sc[...] + pn_sc[...] * vn_ref[...]) / l_sc[...]


def _sample_attend(layer, page_table, q8, kn8, vn8, selb4, selb, cache_k4, cache_v4):
    nbatch, n_pages = page_table.shape
    nj = n_pages // PG
    past = n_pages * PAGE

    def k_map(u):
        return lambda b, ph, j, pt: (layer, pt[b, jnp.where(ph == 0, j, nj - 1) * PG + u], 0, 0)

    def v_map(u):
        return lambda b, ph, j, pt: (layer, pt[b, jnp.where(ph == 0, 0, j) * PG + u], 0, 0)

    vec = pl.BlockSpec((None, 8, ATT_DIM), lambda b, ph, j, pt: (b, 0, 0))
    grid_spec = pltpu.PrefetchScalarGridSpec(
        num_scalar_prefetch=1,
        grid=(nbatch, 2, nj),
        in_specs=[
            vec, vec, vec,
            pl.BlockSpec((None, 1, PG * PAGE_ROWS), lambda b, ph, j, pt: (b, 0, jnp.where(ph == 0, j, nj - 1))),
            pl.BlockSpec((None, 1, 128), lambda b, ph, j, pt: (b, 0, past // 128)),
        ] + [pl.BlockSpec((None, None, PAGE_ROWS, ATT_DIM), k_map(u)) for u in range(PG)]
          + [pl.BlockSpec((None, None, PAGE_ROWS, ATT_DIM), v_map(u)) for u in range(PG)],
        out_specs=vec,
        scratch_shapes=[
            pltpu.VMEM((n_pages, 8, PAGE_ROWS), F32),
            pltpu.VMEM((8, 1), F32),
            pltpu.VMEM((8, 1), F32),
            pltpu.VMEM((8, ATT_DIM), F32),
        ],
    )
    return pl.pallas_call(
        functools.partial(_sattn_kernel, past=past),
        grid_spec=grid_spec,
        out_shape=jax.ShapeDtypeStruct((nbatch, 8, ATT_DIM), F32),
        compiler_params=_cparams(("arbitrary", "arbitrary", "arbitrary")),
        name="sample_attend",
    )(page_table, q8, kn8, vn8, selb4, selb, *([cache_k4] * PG), *([cache_v4] * PG))


def _mix_kernel(oa_ref, ob_ref, gates_ref, x_ref, gm_ref, wa_ref, wb_ref, wo_ref, g_ref, b_ref, o_ref):
    br_a = _dot(oa_ref[...].astype(BF16), wa_ref[...])
    br_b = _dot(ob_ref[...].astype(BF16), wb_ref[...])
    merged = _sigmoid(gates_ref[:, :D_MODEL]) * br_a + _sigmoid(gates_ref[:, D_MODEL:]) * br_b
    y = _dot(merged.astype(BF16), wo_ref[...])
    r = ALPHA * x_ref[...] + gm_ref[...] * y
    o_ref[...] = _layernorm(r, g_ref[...], b_ref[...])


def _mix(oa, ob, gates, x, gm, wa, wb, wo, ln_g, ln_b, tm):
    n = x.shape[0]
    nb, r, _ = gm.shape
    nt = n // tm
    tiles_per_b = nt // nb
    row = lambda i: (i, 0)
    const = lambda i: (0, 0)
    return pl.pallas_call(
        _mix_kernel,
        grid=(nt,),
        in_specs=[
            pl.BlockSpec((tm, DN_WIDTH), row),
            pl.BlockSpec((tm, ATT_WIDTH), row),
            pl.BlockSpec((tm, 2 * D_MODEL), row),
            pl.BlockSpec((tm, D_MODEL), row),
            pl.BlockSpec((None, r, D_MODEL), lambda i: (i // tiles_per_b, 0, 0)),
            pl.BlockSpec((DN_WIDTH, D_MODEL), const),
            pl.BlockSpec((ATT_WIDTH, D_MODEL), const),
            pl.BlockSpec((D_MODEL, D_MODEL), const),
            pl.BlockSpec((1, D_MODEL), const),
            pl.BlockSpec((1, D_MODEL), const),
        ],
        out_specs=pl.BlockSpec((tm, D_MODEL), row),
        out_shape=jax.ShapeDtypeStruct((n, D_MODEL), F32),
        compiler_params=_cparams(("parallel",)),
        name="mix_out",
    )(oa, ob, gates, x, gm, wa, wb, wo, ln_g, ln_b)


def _mod_kernel(x_ref, sc_ref, sh_ref, o_ref):
    o_ref[...] = (x_ref[...] * (1.0 + sc_ref[...]) + sh_ref[...]).astype(o_ref.dtype)


def _route_kernel(x_ref, sc_ref, sh_ref, wr_ref, o_ref, e_ref, g_ref):
    h = x_ref[...] * (1.0 + sc_ref[...]) + sh_ref[...]
    o_ref[...] = h.astype(o_ref.dtype)
    logits = _dot(h, wr_ref[...], HI)
    lane = lax.broadcasted_iota(I32, logits.shape, 1)
    logits = jnp.where(lane < N_EXPERTS, logits, -jnp.inf)
    e = jnp.exp(logits - jnp.max(logits, axis=-1, keepdims=True))
    p = e / jnp.sum(e, axis=-1, keepdims=True)
    p1 = jnp.max(p, axis=-1, keepdims=True)
    e1 = jnp.min(jnp.where(p == p1, lane, 128), axis=-1, keepdims=True)
    rest = jnp.where((lane == e1) | (lane >= N_EXPERTS), -1.0, p)
    p2 = jnp.max(rest, axis=-1, keepdims=True)
    e2 = jnp.min(jnp.where(rest == p2, lane, 128), axis=-1, keepdims=True)
    tot = p1 + p2
    e_ref[...] = jnp.concatenate([e1, e2], axis=1)
    g_ref[...] = jnp.concatenate([p1 / tot, p2 / tot], axis=1)


def _ffn_input(x, sc, sh, tm, w_router=None):
    n = x.shape[0]
    nb, r, _ = sc.shape
    nt = n // tm
    tiles_per_b = nt // nb
    row = lambda i: (i, 0)
    mod_spec = pl.BlockSpec((None, r, D_MODEL), lambda i: (i // tiles_per_b, 0, 0))
    in_specs = [pl.BlockSpec((tm, D_MODEL), row), mod_spec, mod_spec]
    if w_router is None:
        return pl.pallas_call(
            _mod_kernel, grid=(nt,), in_specs=in_specs,
            out_specs=pl.BlockSpec((tm, D_MODEL), row),
            out_shape=jax.ShapeDtypeStruct((n, D_MODEL), BF16),
            compiler_params=_cparams(("parallel",)), name="ffn_mod",
        )(x, sc, sh)
    wr = jnp.pad(w_router, ((0, 0), (0, 128 - N_EXPERTS)))
    return pl.pallas_call(
        _route_kernel, grid=(nt,),
        in_specs=in_specs + [pl.BlockSpec((D_MODEL, 128), lambda i: (0, 0))],
        out_specs=[pl.BlockSpec((tm, D_MODEL), row), pl.BlockSpec((tm, 2), row), pl.BlockSpec((tm, 2), row)],
        out_shape=[jax.ShapeDtypeStruct((n, D_MODEL), BF16), jax.ShapeDtypeStruct((n, 2), I32),
                   jax.ShapeDtypeStruct((n, 2), F32)],
        compiler_params=_cparams(("parallel",)), name="ffn_route",
    )(x, sc, sh, wr)


def _gffn_kernel(te_ref, nu_ref, x_ref, wg_ref, wu_ref, wd_ref, o_ref, acc):
    t = pl.program_id(0)
    f = pl.program_id(1)

    @pl.when(t < nu_ref[0])
    def _():
        x = x_ref[...]
        a = _dot(x, wg_ref[...].astype(BF16))
        b = _dot(x, wu_ref[...].astype(BF16))
        y = _dot((_silu(a) * b).astype(BF16), wd_ref[...].astype(BF16))

        @pl.when(f == 0)
        def _():
            acc[...] = y

        @pl.when(f > 0)
        def _():
            acc[...] += y

    @pl.when(f == pl.num_programs(1) - 1)
    def _():
        o_ref[...] = jnp.where(t < nu_ref[0], acc[...], 0.0)


def _grouped_ffn(x16, tile_expert, n_used, w_gate, w_up, w_down, tm, tf=256):
    n = x16.shape[0]
    nt = n // tm
    nf = D_FF // tf
    grid_spec = pltpu.PrefetchScalarGridSpec(
        num_scalar_prefetch=2,
        grid=(nt, nf),
        in_specs=[
            pl.BlockSpec((tm, D_MODEL), lambda t, f, te, nu: (t, 0)),
            pl.BlockSpec((None, D_MODEL, tf), lambda t, f, te, nu: (te[t], 0, f)),
            pl.BlockSpec((None, D_MODEL, tf), lambda t, f, te, nu: (te[t], 0, f)),
            pl.BlockSpec((None, tf, D_MODEL), lambda t, f, te, nu: (te[t], f, 0)),
        ],
        out_specs=pl.BlockSpec((tm, D_MODEL), lambda t, f, te, nu: (t, 0)),
        scratch_shapes=[pltpu.VMEM((tm, D_MODEL), F32)],
    )
    return pl.pallas_call(
        _gffn_kernel,
        grid_spec=grid_spec,
        out_shape=jax.ShapeDtypeStruct((n, D_MODEL), F32),
        compiler_params=_cparams(("arbitrary", "arbitrary")),
        name="grouped_ffn",
    )(tile_expert, n_used, x16, w_gate, w_up, w_down)


def _res_ln_kernel(x_ref, y_ref, gf_ref, g_ref, b_ref, o_ref):
    r = ALPHA * x_ref[...] + gf_ref[...] * y_ref[...]
    o_ref[...] = _layernorm(r, g_ref[...], b_ref[...])


def _res_ln2_kernel(x_ref, y0_ref, y1_ref, gt_ref, gf_ref, g_ref, b_ref, o_ref):
    gt = gt_ref[...]
    y = y0_ref[...] * gt[:, 0:1] + y1_ref[...] * gt[:, 1:2]
    r = ALPHA * x_ref[...] + gf_ref[...] * y
    o_ref[...] = _layernorm(r, g_ref[...], b_ref[...])


def _res_ln(x, ys, gate2, gf, ln_g, ln_b, tm):
    n = x.shape[0]
    nb, r, _ = gf.shape
    nt = n // tm
    tiles_per_b = nt // nb
    row = lambda i: (i, 0)
    const = lambda i: (0, 0)
    big = pl.BlockSpec((tm, D_MODEL), row)
    mod_spec = pl.BlockSpec((None, r, D_MODEL), lambda i: (i // tiles_per_b, 0, 0))
    vec = pl.BlockSpec((1, D_MODEL), const)
    if gate2 is None:
        kern, ins, specs = _res_ln_kernel, (x, ys[0], gf, ln_g, ln_b), [big, big, mod_spec, vec, vec]
    else:
        kern = _res_ln2_kernel
        ins = (x, ys[0], ys[1], gate2, gf, ln_g, ln_b)
        specs = [big, big, big, pl.BlockSpec((tm, 2), row), mod_spec, vec, vec]
    return pl.pallas_call(
        kern, grid=(nt,), in_specs=specs, out_specs=big,
        out_shape=jax.ShapeDtypeStruct((n, D_MODEL), F32),
        compiler_params=_cparams(("parallel",)), name="res_ln",
    )(*ins)


def _dense_ffn(x, sc, sh, gf, w_gate, w_up, w_down, ln_g, ln_b, tm_mod, tm_ffn):
    n = x.shape[0]
    h16 = _ffn_input(x, sc, sh, tm_mod)
    nt = n // tm_ffn
    y = _grouped_ffn(h16, jnp.zeros((nt,), I32), jnp.full((1,), nt, I32),
                     w_gate[None], w_up[None], w_down[None], tm_ffn)
    return _res_ln(x, (y,), None, gf, ln_g, ln_b, tm_mod)


def _moe_ffn(x, sc, sh, gf, w_router, w_gate, w_up, w_down, ln_g, ln_b, tm_mod, tm_e):
    n = x.shape[0]
    h16, top_e, gate2 = _ffn_input(x, sc, sh, tm_mod, w_router)
    n_assign = 2 * n
    flat_e = top_e.reshape(-1)
    order = jnp.argsort(flat_e).astype(I32)
    sorted_e = flat_e[order]
    counts = jnp.zeros((N_EXPERTS,), I32).at[flat_e].add(1)
    padded = (counts + tm_e - 1) // tm_e * tm_e
    pad_end = jnp.cumsum(padded)
    pad_start = pad_end - padded
    start = jnp.cumsum(counts) - counts
    dest = pad_start[sorted_e] + jnp.arange(n_assign, dtype=I32) - start[sorted_e]
    n_tiles = -(-n_assign // tm_e) + N_EXPERTS
    slot_token = jnp.full((n_tiles * tm_e,), n, I32).at[dest].set(order // 2)
    tile_expert = jnp.minimum(
        jnp.searchsorted(pad_end, jnp.arange(n_tiles, dtype=I32) * tm_e, side="right"), N_EXPERTS - 1).astype(I32)
    n_used = (pad_end[-1] // tm_e).astype(I32).reshape(1)
    h_pad = jnp.concatenate([h16, jnp.zeros((1, D_MODEL), h16.dtype)], axis=0)
    xb = h_pad[slot_token]
    yb = _grouped_ffn(xb, tile_expert, n_used, w_gate, w_up, w_down, tm_e)
    slot_of = jnp.zeros((n_assign,), I32).at[order].set(dest).reshape(n, 2)
    y0 = yb[slot_of[:, 0]]
    y1 = yb[slot_of[:, 1]]
    return _res_ln(x, (y0, y1), gate2, gf, ln_g, ln_b, tm_mod)


def kernel(x_prompt, x_sample, cache_k, cache_v, cache_idx_k, state_delta, state_conv, page_table, c_prompt, c_sample, w_ada, b_ada, w_in, conv_w, a_log, dt_bias, dn_norm_g, w_branch, w_out, ln1_g, ln1_b, ln2_g, ln2_b, w_ffn_gate, w_ffn_up, w_ffn_down, w_router, w_exp_gate, w_exp_up, w_exp_down):
    bp, seq, _ = x_prompt.shape
    bs = x_sample.shape[0]
    n_p = bp * seq
    n_phys = cache_k.shape[1]
    past = page_table.shape[1] * PAGE

    c_all = jnp.concatenate([c_prompt, c_sample], axis=0)
    c_rows = -(-c_all.shape[0] // 8) * 8
    c_all = jnp.pad(c_all, ((0, c_rows - c_all.shape[0]), (0, 0)))
    mods = _ada_all(c_all, w_ada, b_ada)

    cache_k4 = cache_k.reshape(DEPTH, n_phys, PAGE_ROWS, ATT_DIM)
    cache_v4 = cache_v.reshape(DEPTH, n_phys, PAGE_ROWS, ATT_DIM)
    ck = 512

    xp = x_prompt.reshape(n_p, D_MODEL)
    xs = x_sample.reshape(bs, D_MODEL)
    lane = jnp.arange(128)
    outs_p, outs_s = [], []
    for l in range(DEPTH):
        j = l // 2
        mod_p = [mods[l, :bp, u * D_MODEL:(u + 1) * D_MODEL].reshape(bp, 1, D_MODEL) for u in range(6)]
        mod_s = [mods[l, bp:bp + bs, u * D_MODEL:(u + 1) * D_MODEL].reshape(1, bs, D_MODEL) for u in range(6)]
        w_groups = _split_w_in(w_in[l])
        head_lane = jnp.clip(lane - SM_A, 0, DN_HEADS - 1)
        in_a = (lane >= SM_A) & (lane < SM_A + DN_HEADS)
        alog_row = jnp.where(in_a, a_log[l][head_lane], 0.0).reshape(1, 128)
        dt_row = jnp.where(in_a, dt_bias[l][head_lane], 0.0).reshape(1, 128)
        norm_g = dn_norm_g[l].reshape(1, DN_DIM)
        wa16 = w_branch[l, 0].astype(BF16)
        wb16 = w_branch[l, 1].astype(BF16)
        wo16 = w_out[l].astype(BF16)
        ln1 = (ln1_g[l].reshape(1, -1), ln1_b[l].reshape(1, -1))
        ln2 = (ln2_g[l].reshape(1, -1), ln2_b[l].reshape(1, -1))

        pr = _in_proj(xp, mod_p[1], mod_p[0], w_groups, tm=256)
        oa, s_new_p = _delta_prompt(pr["qkv_a"], pr["z"], pr["small"], conv_w[l], alog_row, dt_row, norm_g, bp)
        ik16 = pr["small"][:, SM_IK:SM_IK + IDX_DIM].astype(BF16)
        vt16 = jnp.swapaxes(pr["v_b16"].reshape(bp, seq // ck, ck, ATT_WIDTH), 2, 3)
        tq = 128
        iq_hm = jnp.swapaxes(pr["iq"].reshape(n_p // tq, tq, IDX_HEADS, IDX_DIM), 1, 2).reshape(-1, IDX_DIM)
        ob = _attn_prompt(pr["q_b"], iq_hm, pr["small"], pr["k_b16"], vt16, ik16, bp, tq=tq, ck=ck)
        x1 = _mix(oa, ob, pr["gates"], xp, mod_p[2], wa16, wb16, wo16, *ln1, tm=512)
        if l % 2 == 0:
            xp = _dense_ffn(x1, mod_p[4], mod_p[3], mod_p[5], w_ffn_gate[j], w_ffn_up[j], w_ffn_down[j], *ln2,
                            tm_mod=512, tm_ffn=1024)
        else:
            xp = _moe_ffn(x1, mod_p[4], mod_p[3], mod_p[5], w_router[j], w_exp_gate[j], w_exp_up[j], w_exp_down[j],
                          *ln2, tm_mod=512, tm_e=1024)
        conv_p = pr["qkv_a"].reshape(bp, seq, -1)[:, seq - (CONV_W - 1):, :]
        outs_p.append((conv_p, s_new_p, pr["k_b"].reshape(bp, seq, ATT_HEADS, ATT_DIM),
                       pr["v_b"].reshape(bp, seq, ATT_HEADS, ATT_DIM),
                       pr["small"][:, SM_IK:SM_IK + IDX_DIM].reshape(bp, seq, IDX_DIM)))

        sr = _in_proj(xs, mod_s[1], mod_s[0], w_groups, tm=bs)
        oa_s, s_new_s = _delta_sample(sr["qkv_a"], state_conv[l], sr["z"], sr["small"], conv_w[l], alog_row, dt_row,
                                      norm_g, state_delta[l])
        ik_s = sr["small"][:, SM_IK:SM_IK + IDX_DIM]
        iq3 = sr["iq"].astype(F32).reshape(bs, IDX_HEADS, IDX_DIM)
        iw3 = sr["small"][:, SM_IW:SM_IW + IDX_HEADS].reshape(bs, IDX_HEADS, 1)
        scores = _sample_scores(l, page_table, iq3, iw3, ik_s.reshape(bs, 1, IDX_DIM), cache_idx_k)
        selb = _sample_select(scores.reshape(bs, -1)).reshape(scores.shape)
        head_rows = lambda a: jnp.pad(a.astype(F32).reshape(bs, ATT_HEADS, ATT_DIM), ((0, 0), (0, 8 - ATT_HEADS), (0, 0)))
        selb4 = jnp.repeat(selb, ATT_HEADS, axis=-1)
        ob_s = _sample_attend(l, page_table, head_rows(sr["q_b"]), head_rows(sr["k_b"]), head_rows(sr["v_b"]),
                              selb4, selb, cache_k4, cache_v4)[:, :ATT_HEADS, :].reshape(bs, ATT_WIDTH)
        x1s = _mix(oa_s, ob_s, sr["gates"], xs, mod_s[2], wa16, wb16, wo16, *ln1, tm=bs)
        if l % 2 == 0:
            xs = _dense_ffn(x1s, mod_s[4], mod_s[3], mod_s[5], w_ffn_gate[j], w_ffn_up[j], w_ffn_down[j], *ln2,
                            tm_mod=bs, tm_ffn=bs)
        else:
            xs = _moe_ffn(x1s, mod_s[4], mod_s[3], mod_s[5], w_router[j], w_exp_gate[j], w_exp_up[j], w_exp_down[j],
                          *ln2, tm_mod=bs, tm_e=64)
        conv_s = jnp.concatenate([state_conv[l][:, 1:, :], sr["qkv_a"][:, None, :]], axis=1)
        outs_s.append((conv_s, s_new_s, sr["k_b"].reshape(bs, 1, ATT_HEADS, ATT_DIM),
                       sr["v_b"].reshape(bs, 1, ATT_HEADS, ATT_DIM), ik_s.reshape(bs, 1, IDX_DIM)))

    stack = lambda rows, u: jnp.stack([r[u] for r in rows])
    return (xp.reshape(bp, seq, D_MODEL), xs.reshape(bs, 1, D_MODEL),
            stack(outs_p, 2), stack(outs_p, 3), stack(outs_p, 4), stack(outs_p, 1), stack(outs_p, 0),
            stack(outs_s, 2), stack(outs_s, 3), stack(outs_s, 4), stack(outs_s, 1), stack(outs_s, 0))
```

```python
import functools

import jax
import jax.numpy as jnp
from jax import lax
from jax.experimental import pallas as pl
from jax.experimental.pallas import tpu as pltpu

F32 = jnp.float32
BF16 = jnp.bfloat16
I32 = jnp.int32
I16 = jnp.int16
HI = lax.Precision.HIGHEST

D_MODEL = 1024
DEPTH = 4
PAGE = 128
DN_HEADS = 4
DN_DIM = 128
DN_WIDTH = DN_HEADS * DN_DIM
CONV_W = 4
DN_CHUNK = 64
ATT_HEADS = 4
ATT_DIM = 128
ATT_WIDTH = ATT_HEADS * ATT_DIM
IDX_HEADS = 8
IDX_DIM = 64
TOPK = 256
D_FF = 2816
N_EXPERTS = 8
ALPHA = (2.0 * DEPTH) ** 0.25
EPS = 1e-5
NEG = -1e30
LOG2E = 1.4426950408889634

SM_IK = 0
SM_B = 64
SM_A = 68
SM_IW = 72

VMEM_LIMIT = 56 * 1024 * 1024


def _cparams(sem):
    return pltpu.CompilerParams(dimension_semantics=sem, vmem_limit_bytes=VMEM_LIMIT)


def _sigmoid(x):
    return jax.nn.sigmoid(x)


def _silu(x):
    return x * jax.nn.sigmoid(x)


def _softplus(x):
    return jnp.maximum(x, 0.0) + jnp.log1p(jnp.exp(-jnp.abs(x)))


def _dot(a, b, precision=None):
    return jnp.dot(a, b, precision=precision, preferred_element_type=F32)


def _dot_nt(a, b, precision=None):
    return lax.dot_general(a, b, (((1,), (1,)), ((), ())), precision=precision, preferred_element_type=F32)


def _dot_tn(a, b, precision=None):
    return lax.dot_general(a, b, (((0,), (0,)), ((), ())), precision=precision, preferred_element_type=F32)


def _split(a):
    hi = a.astype(BF16)
    return hi, (a - hi.astype(F32)).astype(BF16)


def _dot3(a, b):
    return _dot(a[0], b[0]) + (_dot(a[0], b[1]) + _dot(a[1], b[0]))


def _layernorm(r, g, b):
    mu = jnp.mean(r, axis=-1, keepdims=True)
    rc = r - mu
    var = jnp.mean(rc * rc, axis=-1, keepdims=True)
    return rc * lax.rsqrt(var + EPS) * g + b


def _ada_kernel(c_ref, w_ref, b_ref, o_ref):
    o_ref[...] = _dot(_silu(c_ref[...]), w_ref[...], HI) + b_ref[...]


def _ada_all(c, w_ada, b_ada):
    rows = c.shape[0]
    nj = w_ada.shape[2] // D_MODEL
    return pl.pallas_call(
        _ada_kernel,
        grid=(DEPTH, nj),
        in_specs=[
            pl.BlockSpec((rows, D_MODEL), lambda l, j: (0, 0)),
            pl.BlockSpec((None, D_MODEL, D_MODEL), lambda l, j: (l, 0, j)),
            pl.BlockSpec((None, 1, D_MODEL), lambda l, j: (l, 0, j)),
        ],
        out_specs=pl.BlockSpec((None, rows, D_MODEL), lambda l, j: (l, 0, j)),
        out_shape=jax.ShapeDtypeStruct((DEPTH, rows, nj * D_MODEL), F32),
        compiler_params=_cparams(("arbitrary", "arbitrary")),
        name="ada_mod",
    )(c, w_ada, b_ada.reshape(DEPTH, 1, -1))


IN_GROUPS = (
    ("qkv_a", 3 * DN_WIDTH, (F32,)),
    ("z", DN_WIDTH, (F32,)),
    ("q_b", ATT_WIDTH, (BF16,)),
    ("k_b", ATT_WIDTH, (F32, BF16)),
    ("v_b", ATT_WIDTH, (F32, BF16)),
    ("iq", IDX_HEADS * IDX_DIM, (BF16,)),
    ("gates", 2 * D_MODEL, (F32,)),
    ("small", 128, (F32,)),
)


def _split_w_in(w):
    o = 0
    parts = {}
    for name, width in (("qkv_a", 3 * DN_WIDTH), ("z", DN_WIDTH), ("b", DN_HEADS), ("a", DN_HEADS),
                        ("q_b", ATT_WIDTH), ("k_b", ATT_WIDTH), ("v_b", ATT_WIDTH),
                        ("iq", IDX_HEADS * IDX_DIM), ("ik", IDX_DIM), ("iw", IDX_HEADS), ("gates", 2 * D_MODEL)):
        parts[name] = w[:, o:o + width]
        o += width
    small = jnp.concatenate(
        [parts["ik"], parts["b"], parts["a"], parts["iw"],
         jnp.zeros((w.shape[0], 128 - IDX_DIM - 2 * DN_HEADS - IDX_HEADS), w.dtype)], axis=1)
    parts["small"] = small
    return [parts[name].astype(BF16) for name, _, _ in IN_GROUPS]


def _in_kernel(x_ref, sc_ref, sh_ref, *refs):
    n = len(IN_GROUPS)
    w_refs, o_refs = refs[:n], refs[n:]
    h = (x_ref[...] * (1.0 + sc_ref[...]) + sh_ref[...]).astype(BF16)
    k = 0
    for (name, width, dtypes), w_ref in zip(IN_GROUPS, w_refs):
        y = _dot(h, w_ref[...])
        for dt in dtypes:
            o_refs[k][...] = y.astype(dt)
            k += 1


def _in_proj(x, sc, sh, w_groups, tm):
    n = x.shape[0]
    nb, r, _ = sc.shape
    nt = n // tm
    tiles_per_b = nt // nb
    mod_spec = pl.BlockSpec((None, r, D_MODEL), lambda i: (i // tiles_per_b, 0, 0))
    in_specs = [pl.BlockSpec((tm, D_MODEL), lambda i: (i, 0)), mod_spec, mod_spec]
    for (name, width, _), w in zip(IN_GROUPS, w_groups):
        in_specs.append(pl.BlockSpec((D_MODEL, width), lambda i: (0, 0), pipeline_mode=pl.Buffered(1)))
    out_specs, out_shape = [], []
    for name, width, dtypes in IN_GROUPS:
        for dt in dtypes:
            out_specs.append(pl.BlockSpec((tm, width), lambda i: (i, 0)))
            out_shape.append(jax.ShapeDtypeStruct((n, width), dt))
    outs = pl.pallas_call(
        _in_kernel,
        grid=(nt,),
        in_specs=in_specs,
        out_specs=out_specs,
        out_shape=out_shape,
        compiler_params=_cparams(("parallel",)),
        name="in_proj",
    )(x, sc, sh, *w_groups)
    names = []
    for name, _, dtypes in IN_GROUPS:
        for dt in dtypes:
            names.append(name if dt == dtypes[0] else name + "16")
    return dict(zip(names, outs))


def _delta_kernel(qkv_ref, z_ref, sm_ref, cw_ref, alog_ref, dt_ref, ng_ref, o_ref, s_out_ref, xs, s_sc, *, tt):
    i = pl.program_id(1)
    nc = tt // DN_CHUNK
    c = DN_CHUNK

    @pl.when(i == 0)
    def _():
        xs[0:8, :] = jnp.zeros((8, 3 * DN_WIDTH), F32)
        s_sc[...] = jnp.zeros_like(s_sc)

    xs[8:8 + tt, :] = qkv_ref[...]
    cw = cw_ref[...]
    y = xs[8:8 + tt, :] * cw[3:4, :]
    for j in range(1, CONV_W):
        y = y + xs[8 - j:8 - j + tt, :] * cw[3 - j:4 - j, :]
    act = _silu(y)
    xs[0:8, :] = xs[tt:tt + 8, :]

    sm = sm_ref[...]
    beta_all = _sigmoid(sm)
    g_all = -jnp.exp(alog_ref[...]) * _softplus(sm + dt_ref[...])
    r_io = lax.broadcasted_iota(I32, (tt, tt), 0)
    c_io = lax.broadcasted_iota(I32, (tt, tt), 1)
    lblk = jnp.where((r_io // c == c_io // c) & (r_io >= c_io), 1.0, 0.0).astype(F32)
    gcum = _dot(lblk, g_all, HI)
    gcum_t = gcum.T

    ri = lax.broadcasted_iota(I32, (c, c), 0)
    ci = lax.broadcasted_iota(I32, (c, c), 1)
    incl = ri >= ci
    strict = ri > ci
    eye = jnp.where(ri == ci, 1.0, 0.0).astype(F32)
    ng = ng_ref[...]

    blocks = [(h, cc) for h in range(DN_HEADS) for cc in range(nc)]
    qs, ks, gis, egs, dmats, qks, nms, rhss = {}, {}, {}, {}, {}, {}, {}, {}
    for h in range(DN_HEADS):
        lo, hi = h * DN_DIM, (h + 1) * DN_DIM
        q_h = act[:, lo:hi]
        k_h = act[:, DN_WIDTH + lo:DN_WIDTH + hi]
        v_h = act[:, 2 * DN_WIDTH + lo:2 * DN_WIDTH + hi]
        q_h = q_h * lax.rsqrt(jnp.sum(q_h * q_h, axis=-1, keepdims=True) + 1e-6) * (DN_DIM ** -0.5)
        k_h = k_h * lax.rsqrt(jnp.sum(k_h * k_h, axis=-1, keepdims=True) + 1e-6)
        for cc in range(nc):
            r0, r1 = cc * c, (cc + 1) * c
            qc, kc, vc = q_h[r0:r1], k_h[r0:r1], v_h[r0:r1]
            gi = gcum[r0:r1, SM_A + h:SM_A + h + 1]
            gj = gcum_t[SM_A + h:SM_A + h + 1, r0:r1]
            bi = beta_all[r0:r1, SM_B + h:SM_B + h + 1]
            dmat = jnp.exp(jnp.where(incl, gi - gj, -jnp.inf))
            eg = jnp.exp(gi)
            key = (h, cc)
            qs[key], ks[key], gis[key], egs[key], dmats[key] = qc, kc, gi, eg, dmat
            qks[key] = _dot_nt(qc, kc) * dmat
            nms[key] = jnp.where(strict, -(bi * _dot_nt(kc, kc) * dmat), 0.0)
            rhss[key] = jnp.concatenate([vc * bi, kc * (bi * eg)], axis=1)

    tms = {key: eye + nms[key] for key in blocks}
    ps = {key: _split(nms[key]) for key in blocks}
    for _ in range(5):
        for key in blocks:
            p2 = _dot3(ps[key], ps[key])
            ps[key] = _split(p2)
            tms[key] = tms[key] + _dot3(_split(tms[key]), ps[key])
    sols = {key: _dot3(_split(tms[key]), _split(rhss[key])) for key in blocks}

    states = [s_sc[h] for h in range(DN_HEADS)]
    for cc in range(nc):
        r0, r1 = cc * c, (cc + 1) * c
        for h in range(DN_HEADS):
            lo, hi = h * DN_DIM, (h + 1) * DN_DIM
            key = (h, cc)
            s_h = states[h]
            gi, eg = gis[key], egs[key]
            u, w = sols[key][:, :DN_DIM], sols[key][:, DN_DIM:]
            v_new = u - _dot(w, s_h)
            o = _dot(qs[key] * eg, s_h) + _dot(qks[key], v_new)
            g_last = gi[c - 1:c, :]
            states[h] = s_h * jnp.exp(g_last) + _dot_tn(ks[key] * jnp.exp(g_last - gi), v_new)
            o = o * lax.rsqrt(jnp.mean(o * o, axis=-1, keepdims=True) + EPS) * ng
            o_ref[r0:r1, lo:hi] = o * _silu(z_ref[r0:r1, lo:hi])
    for h in range(DN_HEADS):
        s_sc[h] = states[h]

    @pl.when(i == pl.num_programs(1) - 1)
    def _():
        s_out_ref[...] = s_sc[...]


def _delta_prompt(qkv, z, small, conv_w, alog_row, dt_row, norm_g, nb, tt=256):
    n = qkv.shape[0]
    nt = n // nb // tt
    row = lambda b, i: (b * nt + i, 0)
    const = lambda b, i: (0, 0)
    return pl.pallas_call(
        functools.partial(_delta_kernel, tt=tt),
        grid=(nb, nt),
        in_specs=[
            pl.BlockSpec((tt, 3 * DN_WIDTH), row),
            pl.BlockSpec((tt, DN_WIDTH), row),
            pl.BlockSpec((tt, 128), row),
            pl.BlockSpec((CONV_W, 3 * DN_WIDTH), const),
            pl.BlockSpec((1, 128), const),
            pl.BlockSpec((1, 128), const),
            pl.BlockSpec((1, DN_DIM), const),
        ],
        out_specs=[
            pl.BlockSpec((tt, DN_WIDTH), row),
            pl.BlockSpec((None, DN_HEADS, DN_DIM, DN_DIM), lambda b, i: (b, 0, 0, 0)),
        ],
        out_shape=[
            jax.ShapeDtypeStruct((n, DN_WIDTH), F32),
            jax.ShapeDtypeStruct((nb, DN_HEADS, DN_DIM, DN_DIM), F32),
        ],
        scratch_shapes=[
            pltpu.VMEM((tt + 8, 3 * DN_WIDTH), F32),
            pltpu.VMEM((DN_HEADS, DN_DIM, DN_DIM), F32),
        ],
        compiler_params=_cparams(("arbitrary", "arbitrary")),
        name="delta_prompt",
    )(qkv, z, small, conv_w, alog_row, dt_row, norm_g)


def _sdelta_kernel(qkv_ref, buf_ref, z_ref, sm_ref, cw_ref, alog_ref, dt_ref, ng_ref, s_ref, o_ref, s_out_ref, *, rows):
    cw = cw_ref[...]
    y = qkv_ref[...] * cw[3:4, :]
    for j in range(CONV_W - 1):
        y = y + buf_ref[:, j, :] * cw[j:j + 1, :]
    act = _silu(y)
    sm = sm_ref[...]
    beta_all = _sigmoid(sm)
    g_all = -jnp.exp(alog_ref[...]) * _softplus(sm + dt_ref[...])
    ng = ng_ref[...]
    sub = lax.broadcasted_iota(I32, (8, DN_DIM), 0)
    for h in range(DN_HEADS):
        lo, hi = h * DN_DIM, (h + 1) * DN_DIM
        q_h = act[:, lo:hi]
        k_h = act[:, DN_WIDTH + lo:DN_WIDTH + hi]
        v_h = act[:, 2 * DN_WIDTH + lo:2 * DN_WIDTH + hi]
        q_h = q_h * lax.rsqrt(jnp.sum(q_h * q_h, axis=-1, keepdims=True) + 1e-6) * (DN_DIM ** -0.5)
        k_h = k_h * lax.rsqrt(jnp.sum(k_h * k_h, axis=-1, keepdims=True) + 1e-6)
        g = g_all[:, SM_A + h:SM_A + h + 1]
        b = beta_all[:, SM_B + h:SM_B + h + 1]
        eg = jnp.exp(g)
        u = v_h * b
        w = k_h * (b * eg)
        qe = q_h * eg
        qk = jnp.sum(q_h * k_h, axis=-1, keepdims=True)
        z_h = z_ref[:, lo:hi]
        for r in range(rows):
            s = s_ref[r, h]
            w8 = jnp.where(sub == 0, w[r:r + 1], 0.0)
            q8 = jnp.where(sub == 0, qe[r:r + 1], 0.0)
            k8 = jnp.where(sub == 0, k_h[r:r + 1], 0.0)
            v_new = u[r:r + 1] - _dot(w8, s, HI)[0:1]
            o = _dot(q8, s, HI)[0:1] + qk[r:r + 1] * v_new
            vn8 = jnp.where(sub == 0, v_new, 0.0)
            s_out_ref[r, h] = s * eg[r:r + 1] + _dot_tn(k8, vn8, HI)
            o = o * lax.rsqrt(jnp.mean(o * o, axis=-1, keepdims=True) + EPS) * ng
            o_ref[r:r + 1, lo:hi] = o * _silu(z_h[r:r + 1])


def _delta_sample(qkv, buf, z, small, conv_w, alog_row, dt_row, norm_g, layer, state, rows=8):
    n = qkv.shape[0]
    row = lambda i: (i, 0)
    const = lambda i: (0, 0)
    return pl.pallas_call(
        functools.partial(_sdelta_kernel, rows=rows),
        grid=(n // rows,),
        in_specs=[
            pl.BlockSpec((rows, 3 * DN_WIDTH), row),
            pl.BlockSpec((rows, CONV_W - 1, 3 * DN_WIDTH), lambda i: (i, 0, 0)),
            pl.BlockSpec((rows, DN_WIDTH), row),
            pl.BlockSpec((rows, 128), row),
            pl.BlockSpec((CONV_W, 3 * DN_WIDTH), const),
            pl.BlockSpec((1, 128), const),
            pl.BlockSpec((1, 128), const),
            pl.BlockSpec((1, DN_DIM), const),
            pl.BlockSpec((None, rows, DN_HEADS, DN_DIM, DN_DIM), lambda i: (layer, i, 0, 0, 0)),
        ],
        out_specs=[
            pl.BlockSpec((rows, DN_WIDTH), row),
            pl.BlockSpec((rows, DN_HEADS, DN_DIM, DN_DIM), lambda i: (i, 0, 0, 0)),
        ],
        out_shape=[
            jax.ShapeDtypeStruct((n, DN_WIDTH), F32),
            jax.ShapeDtypeStruct(state.shape[1:], F32),
        ],
        compiler_params=_cparams(("parallel",)),
        name="delta_sample",
    )(qkv, buf, z, small, conv_w, alog_row, dt_row, norm_g, state)


def _score_keys(s):
    s = jnp.where(s == 0.0, 0.0, s)
    k = pltpu.bitcast(s, I32)
    return jnp.where(k < 0, k ^ jnp.int32(0x7FFFFFFF), k)


SEARCH_ROUND = 4


def _alibi_slope(h):
    return 2.0 ** (-8.0 * (h + 1) / ATT_HEADS)


def _sublane_allreduce(x, op):
    for shift in (4, 2, 1):
        x = op(x, pltpu.roll(x, shift, axis=0))
    return x


def _attn_kernel(q_ref, iq_ref, sm_ref, k_ref, vt_ref, ik_ref, o_ref, keys, keys16, alibi, acc_sc, *, tq, ck):
    i = pl.program_id(1)
    nk = (i * tq + tq + ck - 1) // ck
    g8 = ck // 8
    g16 = ck // 16
    t = i * tq + lax.broadcasted_iota(I32, (8, tq), 1)
    sub = lax.broadcasted_iota(I32, (ck, 1), 0)
    sub3 = lax.broadcasted_iota(I32, (g8, 8, 1), 0) * 8 + lax.broadcasted_iota(I32, (g8, 8, 1), 1)

    @pl.when((pl.program_id(0) == 0) & (i == 0))
    def _():
        for h in range(ATT_HEADS):
            alibi[h] = jnp.broadcast_to(sub.astype(F32) * (_alibi_slope(h) * LOG2E), (ck, tq))

    sm_t = sm_ref[...].T
    iw = [jnp.broadcast_to(sm_t[SM_IW + h:SM_IW + h + 1, :] * (IDX_HEADS ** -0.5 * IDX_DIM ** -0.5), (8, tq))
          for h in range(IDX_HEADS)]
    iq = iq_ref[...]

    def score_chunk(c, diagonal):
        off = pl.multiple_of(c * ck, ck)
        d = _dot_nt(ik_ref[pl.ds(off, ck), :], iq)
        s = jnp.maximum(d[:, 0:tq].reshape(g8, 8, tq), 0.0) * iw[0][None]
        for h in range(1, IDX_HEADS):
            s = s + jnp.maximum(d[:, h * tq:(h + 1) * tq].reshape(g8, 8, tq), 0.0) * iw[h][None]
        if diagonal:
            s = jnp.where(off + sub3 <= t[None], s, -jnp.inf)
        k32 = _score_keys(s)
        keys[c] = k32
        keys16[c] = jnp.right_shift(k32, 16).reshape(ck, tq).astype(I16).reshape(g16, 16, tq)

    def score_full(c, carry):
        score_chunk(c, False)
        return carry

    lax.fori_loop(0, nk - 1, score_full, 0)
    score_chunk(nk - 1, True)

    kth = jnp.minimum(t + 1, TOPK)

    def count(pred_fn):
        def body(c, cnt):
            return cnt + jnp.sum(jnp.where(pred_fn(keys[c], c), 1, 0).astype(I32), axis=0)
        return _sublane_allreduce(lax.fori_loop(0, nk, body, jnp.zeros((8, tq), I32)), jnp.add)

    def count16(pred_fn):
        def body(c, cnt):
            part = jnp.where(pred_fn(keys16[c]), jnp.int16(1), jnp.int16(0))
            tiles = [part[g] for g in range(g16)]
            while len(tiles) > 1:
                tiles = [tiles[a] + tiles[a + 1] for a in range(0, len(tiles), 2)]
            return cnt + tiles[0]
        c32 = lax.fori_loop(0, nk, body, jnp.zeros((16, tq), I16)).astype(I32)
        return _sublane_allreduce(c32[0:8] + c32[8:16], jnp.add)

    def rep16(v):
        return jnp.concatenate([v, v], axis=0).astype(I16)

    def search16(want):
        def cond(st):
            return (st[0] < 16) & (st[3] > 0)

        def round_(st):
            it, thr, cnt_thr, _ = st
            for u in range(SEARCH_ROUND):
                cand = thr + jnp.left_shift(jnp.int32(1), 15 - u - it)
                cand16 = rep16(cand)[None]
                cnt = count16(lambda kc: kc >= cand16)
                ok = cnt >= want
                thr = jnp.where(ok, cand, thr)
                cnt_thr = jnp.where(ok, cnt, cnt_thr)
            return it + SEARCH_ROUND, thr, cnt_thr, jnp.max(jnp.where(cnt_thr != want, 1, 0))

        _, thr, _, unresolved = lax.while_loop(
            cond, round_,
            (jnp.int32(0), jnp.full((8, tq), -2 ** 15, I32), jnp.full((8, tq), 2 ** 30, I32), jnp.int32(1)))
        return thr, unresolved

    thr_hi, unresolved_hi = search16(kth)

    def refine(_):
        thr_hi16 = rep16(thr_hi)[None]
        want_lo = kth - count16(lambda kc: kc > thr_hi16)

        def pack_low(c, carry):
            k32 = keys[c]
            low = jnp.bitwise_and(k32, 0xFFFF) - 2 ** 15
            low = jnp.where(jnp.right_shift(k32, 16) == thr_hi[None], low, -2 ** 15)
            keys16[c] = low.reshape(ck, tq).astype(I16).reshape(g16, 16, tq)
            return carry
        lax.fori_loop(0, nk, pack_low, 0)
        thr_lo, unresolved_lo = search16(want_lo)
        return jnp.left_shift(thr_hi, 16) + (thr_lo + 2 ** 15), unresolved_lo

    thr, unresolved = lax.cond(unresolved_hi > 0, refine, lambda _: (jnp.left_shift(thr_hi, 16), jnp.int32(0)), 0)

    @pl.when(unresolved > 0)
    def _():
        need = kth - count(lambda kc, c: kc > thr[None])

        def pos_step(it, cut):
            cand = cut | jnp.left_shift(jnp.int32(1), 13 - it)
            cnt = count(lambda kc, c: (kc == thr[None]) & (c * ck + sub3 < cand[None]))
            return jnp.where(cnt < need, cand, cut)
        cut = lax.fori_loop(0, 14, pos_step, jnp.zeros((8, tq), I32))

        def drop(c, carry):
            kc = keys[c]
            keys[c] = jnp.where((kc == thr[None]) & (c * ck + sub3 > cut[None]), jnp.int32(-2 ** 31), kc)
            return carry
        lax.fori_loop(0, nk, drop, 0)

    acc_sc[...] = jnp.zeros_like(acc_sc)
    q = q_ref[...]
    qh = [q[:, h * ATT_DIM:(h + 1) * ATT_DIM] for h in range(ATT_HEADS)]
    heads = range(ATT_HEADS)

    def attend_chunk(c, carry):
        ms, ls = carry
        off = pl.multiple_of(c * ck, ck)
        selb = jnp.where(keys[c] >= thr[None], 0.0, NEG).reshape(ck, tq)
        kc = k_ref[pl.ds(off, ck), :]
        off_f = off.astype(F32)
        c_off = [off_f * (_alibi_slope(h) * LOG2E) for h in heads]
        zs = [(_dot_nt(kc[:, h * ATT_DIM:(h + 1) * ATT_DIM], qh[h]) * (ATT_DIM ** -0.5 * LOG2E)
               + (alibi[h] + selb)).reshape(g8, 8, tq) for h in heads]
        m_new = [jnp.maximum(ms[h], _sublane_allreduce(jnp.max(zs[h], axis=0), jnp.maximum) + c_off[h]) for h in heads]
        a = [jnp.exp2(ms[h] - m_new[h]) for h in heads]
        ps = [jnp.exp2(zs[h] - (m_new[h] - c_off[h])[None]) for h in heads]
        l_new = [a[h] * ls[h] + _sublane_allreduce(jnp.sum(ps[h], axis=0), jnp.add) for h in heads]
        for h in heads:
            pv = _dot(vt_ref[c, h * ATT_DIM:(h + 1) * ATT_DIM, :], ps[h].reshape(ck, tq).astype(BF16))
            acc_sc[h] = a[h][None] * acc_sc[h] + pv.reshape(ATT_DIM // 8, 8, tq)
        return tuple(m_new), tuple(l_new)

    init = (tuple(jnp.full((8, tq), NEG, F32) for _ in heads), tuple(jnp.zeros((8, tq), F32) for _ in heads))
    _, ls = lax.fori_loop(0, nk, attend_chunk, init)
    for h in heads:
        o_ref[:, h * ATT_DIM:(h + 1) * ATT_DIM] = (acc_sc[h] / ls[h][None]).reshape(ATT_DIM, tq).T


def _attn_prompt(q16, iq16, small, k16, vt16, ik16, nb, tq=128, ck=512):
    n = q16.shape[0]
    s = n // nb
    nq = s // tq
    row = lambda b, i: (b * nq + i, 0)
    full = lambda b, i: (b, 0)
    once = pl.Buffered(1)
    return pl.pallas_call(
        functools.partial(_attn_kernel, tq=tq, ck=ck),
        grid=(nb, nq),
        in_specs=[
            pl.BlockSpec((tq, ATT_WIDTH), row),
            pl.BlockSpec((IDX_HEADS * tq, IDX_DIM), row),
            pl.BlockSpec((tq, 128), row),
            pl.BlockSpec((s, ATT_WIDTH), full, pipeline_mode=once),
            pl.BlockSpec((None, s // ck, ATT_WIDTH, ck), lambda b, i: (b, 0, 0, 0), pipeline_mode=once),
            pl.BlockSpec((s, IDX_DIM), full, pipeline_mode=once),
        ],
        out_specs=pl.BlockSpec((tq, ATT_WIDTH), row),
        out_shape=jax.ShapeDtypeStruct((n, ATT_WIDTH), F32),
        scratch_shapes=[
            pltpu.VMEM((s // ck, ck // 8, 8, tq), I32),
            pltpu.VMEM((s // ck, ck // 16, 16, tq), I16),
            pltpu.VMEM((ATT_HEADS, ck, tq), F32),
            pltpu.VMEM((ATT_HEADS, ATT_DIM // 8, 8, tq), F32),
        ],
        compiler_params=_cparams(("arbitrary", "arbitrary")),
        name="attn_prompt",
    )(q16, iq16, small, k16, vt16, ik16)


PG = 8


def _sscore_kernel(pt_ref, iq_ref, iw_ref, ikn_ref, *refs):
    pages, o_ref = refs[:PG], refs[PG]
    j = pl.program_id(1)
    iq = iq_ref[...]
    iw = iw_ref[...] * (IDX_HEADS ** -0.5)

    @pl.when(j < pl.num_programs(1) - 1)
    def _():
        for u in range(PG):
            d = _dot(iq, pages[u][...]) * (IDX_DIM ** -0.5)
            o_ref[:, u * PAGE:(u + 1) * PAGE] = jnp.sum(jnp.maximum(d, 0.0) * iw, axis=0, keepdims=True)

    @pl.when(j == pl.num_programs(1) - 1)
    def _():
        d = _dot_nt(iq, jnp.broadcast_to(ikn_ref[...], (8, IDX_DIM)))[:, 0:1] * (IDX_DIM ** -0.5)
        s_new = jnp.sum(jnp.maximum(d, 0.0) * iw, axis=0, keepdims=True)
        lane = lax.broadcasted_iota(I32, (1, PG * PAGE), 1)
        o_ref[...] = jnp.where(lane == 0, s_new, -jnp.inf)


def _sample_scores(layer, page_table, iq3, iw3, ik_new3, cache_idx_t):
    nbatch, n_pages = page_table.shape
    nj = n_pages // PG

    def page_map(u):
        return lambda b, j, pt: (layer, pt[b, jnp.minimum(j * PG + u, n_pages - 1)], 0, 0)

    grid_spec = pltpu.PrefetchScalarGridSpec(
        num_scalar_prefetch=1,
        grid=(nbatch, nj + 1),
        in_specs=[
            pl.BlockSpec((None, IDX_HEADS, IDX_DIM), lambda b, j, pt: (b, 0, 0)),
            pl.BlockSpec((None, IDX_HEADS, 1), lambda b, j, pt: (b, 0, 0)),
            pl.BlockSpec((None, 1, IDX_DIM), lambda b, j, pt: (b, 0, 0)),
        ] + [pl.BlockSpec((None, None, IDX_DIM, PAGE), page_map(u)) for u in range(PG)],
        out_specs=pl.BlockSpec((None, 1, PG * PAGE), lambda b, j, pt: (b, 0, j)),
    )
    return pl.pallas_call(
        _sscore_kernel,
        grid_spec=grid_spec,
        out_shape=jax.ShapeDtypeStruct((nbatch, 1, (nj + 1) * PG * PAGE), F32),
        compiler_params=_cparams(("arbitrary", "arbitrary")),
        name="sample_scores",
    )(page_table, iq3, iw3, ik_new3, *([cache_idx_t] * PG))


def _sselect_kernel(s_ref, o_ref):
    keys = _score_keys(s_ref[...])
    rows, width = keys.shape
    pos = lax.broadcasted_iota(I32, (1, width), 1)

    def cnt(m):
        return jnp.sum(jnp.where(m, 1, 0).astype(I32), axis=-1, keepdims=True)

    def search_step(it, thr):
        cand = thr ^ jnp.left_shift(jnp.int32(1), 31 - it)
        return jnp.where(cnt(keys >= cand) >= TOPK, cand, thr)

    thr = lax.fori_loop(0, 32, search_step, jnp.full((rows, 1), -2 ** 31, I32))
    need = TOPK - cnt(keys > thr)
    eq = keys == thr

    def pos_step(it, cut):
        cand = cut | jnp.left_shift(jnp.int32(1), 14 - it)
        return jnp.where(cnt(eq & (pos < cand)) < need, cand, cut)

    cut = lax.fori_loop(0, 15, pos_step, jnp.zeros((rows, 1), I32))
    sel = (keys > thr) | (eq & (pos <= cut))
    o_ref[...] = jnp.where(sel, 0.0, NEG)


def _sample_select(scores):
    return pl.pallas_call(
        _sselect_kernel,
        out_shape=jax.ShapeDtypeStruct(scores.shape, F32),
        compiler_params=pltpu.CompilerParams(vmem_limit_bytes=VMEM_LIMIT),
        name="sample_select",
    )(scores)


PAGE_ROWS = PAGE * ATT_HEADS


def _sattn_kernel(pt_ref, q_ref, kn_ref, vn_ref, selb_ref, seln_ref, *refs, past):
    kp, vp = refs[:PG], refs[PG:2 * PG]
    o_ref, lg_sc, l_sc, pn_sc, acc_sc = refs[2 * PG:2 * PG + 5]
    ph = pl.program_id(1)
    j = pl.program_id(2)
    scale = ATT_DIM ** -0.5
    q8 = q_ref[...]
    row = lax.broadcasted_iota(I32, (8, 1), 0)
    lane = lax.broadcasted_iota(I32, (1, PAGE_ROWS), 1)
    slope = jnp.exp2(-8.0 * (row + 1).astype(F32) / ATT_HEADS)
    own = jnp.bitwise_and(lane, ATT_HEADS - 1) == row
    tok = jnp.right_shift(lane, 2)

    @pl.when(ph == 0)
    def _():
        for u in range(PG):
            dist = (past - ((j * PG + u) * PAGE + tok)).astype(F32)
            lg = _dot_nt(q8, kp[u][...]) * scale - slope * dist + selb_ref[:, u * PAGE_ROWS:(u + 1) * PAGE_ROWS]
            lg_sc[j * PG + u] = jnp.where(own, lg, NEG)

    @pl.when((ph == 1) & (j == 0))
    def _():
        allg = lg_sc[...]
        lgn = jnp.sum(q8 * kn_ref[...], axis=-1, keepdims=True) * scale + seln_ref[:, 0:1]
        m = jnp.maximum(jnp.max(jnp.max(allg, axis=0), axis=-1, keepdims=True), lgn)
        p = jnp.exp(allg - m)
        lg_sc[...] = p
        pn = jnp.exp(lgn - m)
        l_sc[...] = jnp.sum(jnp.sum(p, axis=0), axis=-1, keepdims=True) + pn
        pn_sc[...] = pn
        acc_sc[...] = jnp.zeros_like(acc_sc)

    @pl.when(ph == 1)
    def _():
        acc = acc_sc[...]
        for u in range(PG):
            acc = acc + _dot(lg_sc[j * PG + u], vp[u][...])
        acc_sc[...] = acc

    @pl.when((ph == 1) & (j == pl.num_programs(2) - 1))
    def _():
        o_ref[...] = (acc_sc[...] + pn_sc[...] * vn_ref[...]) / l_sc[...]


def _sample_attend(layer, page_table, q8, kn8, vn8, selb4, selb, cache_k4, cache_v4):
    nbatch, n_pages = page_table.shape
    nj = n_pages // PG
    past = n_pages * PAGE

    def k_map(u):
        return lambda b, ph, j, pt: (layer, pt[b, jnp.where(ph == 0, j, nj - 1) * PG + u], 0, 0)

    def v_map(u):
        return lambda b, ph, j, pt: (layer, pt[b, jnp.where(ph == 0, 0, j) * PG + u], 0, 0)

    vec = pl.BlockSpec((None, 8, ATT_DIM), lambda b, ph, j, pt: (b, 0, 0))
    grid_spec = pltpu.PrefetchScalarGridSpec(
        num_scalar_prefetch=1,
        grid=(nbatch, 2, nj),
        in_specs=[
            vec, vec, vec,
            pl.BlockSpec((None, 1, PG * PAGE_ROWS), lambda b, ph, j, pt: (b, 0, jnp.where(ph == 0, j, nj - 1))),
            pl.BlockSpec((None, 1, 128), lambda b, ph, j, pt: (b, 0, past // 128)),
        ] + [pl.BlockSpec((None, None, PAGE_ROWS, ATT_DIM), k_map(u)) for u in range(PG)]
          + [pl.BlockSpec((None, None, PAGE_ROWS, ATT_DIM), v_map(u)) for u in range(PG)],
        out_specs=vec,
        scratch_shapes=[
            pltpu.VMEM((n_pages, 8, PAGE_ROWS), F32),
            pltpu.VMEM((8, 1), F32),
            pltpu.VMEM((8, 1), F32),
            pltpu.VMEM((8, ATT_DIM), F32),
        ],
    )
    return pl.pallas_call(
        functools.partial(_sattn_kernel, past=past),
        grid_spec=grid_spec,
        out_shape=jax.ShapeDtypeStruct((nbatch, 8, ATT_DIM), F32),
        compiler_params=_cparams(("arbitrary", "arbitrary", "arbitrary")),
        name="sample_attend",
    )(page_table, q8, kn8, vn8, selb4, selb, *([cache_k4] * PG), *([cache_v4] * PG))


def _mix_kernel(oa_ref, ob_ref, gates_ref, x_ref, gm_ref, wa_ref, wb_ref, wo_ref, g_ref, b_ref, o_ref):
    br_a = _dot(oa_ref[...].astype(BF16), wa_ref[...])
    br_b = _dot(ob_ref[...].astype(BF16), wb_ref[...])
    merged = _sigmoid(gates_ref[:, :D_MODEL]) * br_a + _sigmoid(gates_ref[:, D_MODEL:]) * br_b
    y = _dot(merged.astype(BF16), wo_ref[...])
    r = ALPHA * x_ref[...] + gm_ref[...] * y
    o_ref[...] = _layernorm(r, g_ref[...], b_ref[...])


def _mix(oa, ob, gates, x, gm, wa, wb, wo, ln_g, ln_b, tm):
    n = x.shape[0]
    nb, r, _ = gm.shape
    nt = n // tm
    tiles_per_b = nt // nb
    row = lambda i: (i, 0)
    const = lambda i: (0, 0)
    return pl.pallas_call(
        _mix_kernel,
        grid=(nt,),
        in_specs=[
            pl.BlockSpec((tm, DN_WIDTH), row),
            pl.BlockSpec((tm, ATT_WIDTH), row),
            pl.BlockSpec((tm, 2 * D_MODEL), row),
            pl.BlockSpec((tm, D_MODEL), row),
            pl.BlockSpec((None, r, D_MODEL), lambda i: (i // tiles_per_b, 0, 0)),
            pl.BlockSpec((DN_WIDTH, D_MODEL), const),
            pl.BlockSpec((ATT_WIDTH, D_MODEL), const),
            pl.BlockSpec((D_MODEL, D_MODEL), const),
            pl.BlockSpec((1, D_MODEL), const),
            pl.BlockSpec((1, D_MODEL), const),
        ],
        out_specs=pl.BlockSpec((tm, D_MODEL), row),
        out_shape=jax.ShapeDtypeStruct((n, D_MODEL), F32),
        compiler_params=_cparams(("parallel",)),
        name="mix_out",
    )(oa, ob, gates, x, gm, wa, wb, wo, ln_g, ln_b)


def _route_kernel(x_ref, sc_ref, sh_ref, wr_ref, o_ref, e_ref, g_ref):
    h = x_ref[...] * (1.0 + sc_ref[...]) + sh_ref[...]
    o_ref[...] = h.astype(o_ref.dtype)
    logits = _dot(h, wr_ref[...], HI)
    lane = lax.broadcasted_iota(I32, logits.shape, 1)
    logits = jnp.where(lane < N_EXPERTS, logits, -jnp.inf)
    e = jnp.exp(logits - jnp.max(logits, axis=-1, keepdims=True))
    p = e / jnp.sum(e, axis=-1, keepdims=True)
    p1 = jnp.max(p, axis=-1, keepdims=True)
    e1 = jnp.min(jnp.where(p == p1, lane, 128), axis=-1, keepdims=True)
    rest = jnp.where((lane == e1) | (lane >= N_EXPERTS), -1.0, p)
    p2 = jnp.max(rest, axis=-1, keepdims=True)
    e2 = jnp.min(jnp.where(rest == p2, lane, 128), axis=-1, keepdims=True)
    tot = p1 + p2
    e_ref[...] = jnp.concatenate([e1, e2], axis=1)
    g_ref[...] = jnp.concatenate([p1 / tot, p2 / tot], axis=1)


def _ffn_route(x, sc, sh, tm, w_router):
    n = x.shape[0]
    nb, r, _ = sc.shape
    nt = n // tm
    tiles_per_b = nt // nb
    row = lambda i: (i, 0)
    mod_spec = pl.BlockSpec((None, r, D_MODEL), lambda i: (i // tiles_per_b, 0, 0))
    in_specs = [pl.BlockSpec((tm, D_MODEL), row), mod_spec, mod_spec]
    wr = jnp.pad(w_router, ((0, 0), (0, 128 - N_EXPERTS)))
    return pl.pallas_call(
        _route_kernel, grid=(nt,),
        in_specs=in_specs + [pl.BlockSpec((D_MODEL, 128), lambda i: (0, 0))],
        out_specs=[pl.BlockSpec((tm, D_MODEL), row), pl.BlockSpec((tm, 2), row), pl.BlockSpec((tm, 2), row)],
        out_shape=[jax.ShapeDtypeStruct((n, D_MODEL), F32), jax.ShapeDtypeStruct((n, 2), I32),
                   jax.ShapeDtypeStruct((n, 2), F32)],
        compiler_params=_cparams(("parallel",)), name="ffn_route",
    )(x, sc, sh, wr)


def _gffn_kernel(te_ref, nu_ref, x_ref, *refs, modulate):
    if modulate:
        sc_ref, sh_ref = refs[:2]
        refs = refs[2:]
    wg_ref, wu_ref, wd_ref, o_ref, acc, x16 = refs
    t = pl.program_id(0)
    f = pl.program_id(1)

    @pl.when((t < nu_ref[0]) & (f == 0))
    def _():
        x = x_ref[...]
        if modulate:
            x = x * (1.0 + sc_ref[...]) + sh_ref[...]
        x16[...] = x.astype(BF16)

    @pl.when(t < nu_ref[0])
    def _():
        x = x16[...]
        a = _dot(x, wg_ref[...].astype(BF16))
        b = _dot(x, wu_ref[...].astype(BF16))
        y = _dot((_silu(a) * b).astype(BF16), wd_ref[...].astype(BF16))

        @pl.when(f == 0)
        def _():
            acc[...] = y

        @pl.when(f > 0)
        def _():
            acc[...] += y

    @pl.when(f == pl.num_programs(1) - 1)
    def _():
        o_ref[...] = jnp.where(t < nu_ref[0], acc[...], 0.0)


def _grouped_ffn(x, tile_expert, n_used, layer, w_gate, w_up, w_down, tm, mod=None, tf=256):
    n = x.shape[0]
    nt = n // tm
    nf = D_FF // tf
    mod_specs, mod_args = [], []
    if mod is not None:
        nb, r, _ = mod[0].shape
        tiles_per_b = nt // nb
        mod_specs = [pl.BlockSpec((None, r, D_MODEL), lambda t, f, te, nu: (t // tiles_per_b, 0, 0))] * 2
        mod_args = list(mod)
    grid_spec = pltpu.PrefetchScalarGridSpec(
        num_scalar_prefetch=2,
        grid=(nt, nf),
        in_specs=[pl.BlockSpec((tm, D_MODEL), lambda t, f, te, nu: (t, 0))] + mod_specs + [
            pl.BlockSpec((None, None, D_MODEL, tf), lambda t, f, te, nu: (layer, te[t], 0, f)),
            pl.BlockSpec((None, None, D_MODEL, tf), lambda t, f, te, nu: (layer, te[t], 0, f)),
            pl.BlockSpec((None, None, tf, D_MODEL), lambda t, f, te, nu: (layer, te[t], f, 0)),
        ],
        out_specs=pl.BlockSpec((tm, D_MODEL), lambda t, f, te, nu: (t, 0)),
        scratch_shapes=[pltpu.VMEM((tm, D_MODEL), F32), pltpu.VMEM((tm, D_MODEL), BF16)],
    )
    return pl.pallas_call(
        functools.partial(_gffn_kernel, modulate=mod is not None),
        grid_spec=grid_spec,
        out_shape=jax.ShapeDtypeStruct((n, D_MODEL), F32),
        compiler_params=_cparams(("arbitrary", "arbitrary")),
        name="grouped_ffn",
    )(tile_expert, n_used, x, *mod_args, w_gate, w_up, w_down)


def _res_ln_kernel(x_ref, y_ref, gf_ref, g_ref, b_ref, o_ref):
    r = ALPHA * x_ref[...] + gf_ref[...] * y_ref[...]
    o_ref[...] = _layernorm(r, g_ref[...], b_ref[...])


def _res_ln2_kernel(x_ref, y0_ref, y1_ref, gt_ref, gf_ref, g_ref, b_ref, o_ref):
    gt = gt_ref[...]
    y = y0_ref[...] * gt[:, 0:1] + y1_ref[...] * gt[:, 1:2]
    r = ALPHA * x_ref[...] + gf_ref[...] * y
    o_ref[...] = _layernorm(r, g_ref[...], b_ref[...])


def _res_ln(x, ys, gate2, gf, ln_g, ln_b, tm):
    n = x.shape[0]
    nb, r, _ = gf.shape
    nt = n // tm
    tiles_per_b = nt // nb
    row = lambda i: (i, 0)
    const = lambda i: (0, 0)
    big = pl.BlockSpec((tm, D_MODEL), row)
    mod_spec = pl.BlockSpec((None, r, D_MODEL), lambda i: (i // tiles_per_b, 0, 0))
    vec = pl.BlockSpec((1, D_MODEL), const)
    if gate2 is None:
        kern, ins, specs = _res_ln_kernel, (x, ys[0], gf, ln_g, ln_b), [big, big, mod_spec, vec, vec]
    else:
        kern = _res_ln2_kernel
        ins = (x, ys[0], ys[1], gate2, gf, ln_g, ln_b)
        specs = [big, big, big, pl.BlockSpec((tm, 2), row), mod_spec, vec, vec]
    return pl.pallas_call(
        kern, grid=(nt,), in_specs=specs, out_specs=big,
        out_shape=jax.ShapeDtypeStruct((n, D_MODEL), F32),
        compiler_params=_cparams(("parallel",)), name="res_ln",
    )(*ins)


def _dense_ffn(x, sc, sh, gf, layer, w_gate, w_up, w_down, ln_g, ln_b, tm_mod, tm_ffn):
    n = x.shape[0]
    nt = n // tm_ffn
    y = _grouped_ffn(x, jnp.zeros((nt,), I32), jnp.full((1,), nt, I32), layer,
                     w_gate[:, None], w_up[:, None], w_down[:, None], tm_ffn, mod=(sc, sh))
    return _res_ln(x, (y,), None, gf, ln_g, ln_b, tm_mod)


def _moe_ffn(x, sc, sh, gf, w_router, layer, w_gate, w_up, w_down, ln_g, ln_b, tm_mod, tm_e):
    n = x.shape[0]
    h16, top_e, gate2 = _ffn_route(x, sc, sh, tm_mod, w_router)
    n_assign = 2 * n
    flat_e = top_e.reshape(-1)
    order = jnp.argsort(flat_e).astype(I32)
    sorted_e = flat_e[order]
    counts = jnp.zeros((N_EXPERTS,), I32).at[flat_e].add(1)
    padded = (counts + tm_e - 1) // tm_e * tm_e
    pad_end = jnp.cumsum(padded)
    pad_start = pad_end - padded
    start = jnp.cumsum(counts) - counts
    dest = pad_start[sorted_e] + jnp.arange(n_assign, dtype=I32) - start[sorted_e]
    n_tiles = -(-n_assign // tm_e) + N_EXPERTS
    slot_token = jnp.full((n_tiles * tm_e,), n, I32).at[dest].set(order // 2)
    tile_expert = jnp.minimum(
        jnp.searchsorted(pad_end, jnp.arange(n_tiles, dtype=I32) * tm_e, side="right"), N_EXPERTS - 1).astype(I32)
    n_used = (pad_end[-1] // tm_e).astype(I32).reshape(1)
    h_pad = jnp.concatenate([h16, jnp.zeros((1, D_MODEL), h16.dtype)], axis=0)
    xb = h_pad[slot_token]
    yb = _grouped_ffn(xb, tile_expert, n_used, layer, w_gate, w_up, w_down, tm_e)
    slot_of = jnp.zeros((n_assign,), I32).at[order].set(dest).reshape(n, 2)
    y0 = yb[slot_of[:, 0]]
    y1 = yb[slot_of[:, 1]]
    return _res_ln(x, (y0, y1), gate2, gf, ln_g, ln_b, tm_mod)


def kernel(x_prompt, x_sample, cache_k, cache_v, cache_idx_k, state_delta, state_conv, page_table, c_prompt, c_sample, w_ada, b_ada, w_in, conv_w, a_log, dt_bias, dn_norm_g, w_branch, w_out, ln1_g, ln1_b, ln2_g, ln2_b, w_ffn_gate, w_ffn_up, w_ffn_down, w_router, w_exp_gate, w_exp_up, w_exp_down):
    bp, seq, _ = x_prompt.shape
    bs = x_sample.shape[0]
    n_p = bp * seq
    n_phys = cache_k.shape[1]
    past = page_table.shape[1] * PAGE

    c_all = jnp.concatenate([c_prompt, c_sample], axis=0)
    c_rows = -(-c_all.shape[0] // 8) * 8
    c_all = jnp.pad(c_all, ((0, c_rows - c_all.shape[0]), (0, 0)))
    mods = _ada_all(c_all, w_ada, b_ada)

    cache_k4 = cache_k.reshape(DEPTH, n_phys, PAGE_ROWS, ATT_DIM)
    cache_v4 = cache_v.reshape(DEPTH, n_phys, PAGE_ROWS, ATT_DIM)
    cache_idx_t = jnp.swapaxes(cache_idx_k, 2, 3)
    ck = 512

    xp = x_prompt.reshape(n_p, D_MODEL)
    xs = x_sample.reshape(bs, D_MODEL)
    lane = jnp.arange(128)
    outs_p, outs_s = [], []
    for l in range(DEPTH):
        j = l // 2
        mod_p = [mods[l, :bp, u * D_MODEL:(u + 1) * D_MODEL].reshape(bp, 1, D_MODEL) for u in range(6)]
        mod_s = [mods[l, bp:bp + bs, u * D_MODEL:(u + 1) * D_MODEL].reshape(1, bs, D_MODEL) for u in range(6)]
        w_groups = _split_w_in(w_in[l])
        head_lane = jnp.clip(lane - SM_A, 0, DN_HEADS - 1)
        in_a = (lane >= SM_A) & (lane < SM_A + DN_HEADS)
        alog_row = jnp.where(in_a, a_log[l][head_lane], 0.0).reshape(1, 128)
        dt_row = jnp.where(in_a, dt_bias[l][head_lane], 0.0).reshape(1, 128)
        norm_g = dn_norm_g[l].reshape(1, DN_DIM)
        wa16 = w_branch[l, 0].astype(BF16)
        wb16 = w_branch[l, 1].astype(BF16)
        wo16 = w_out[l].astype(BF16)
        ln1 = (ln1_g[l].reshape(1, -1), ln1_b[l].reshape(1, -1))
        ln2 = (ln2_g[l].reshape(1, -1), ln2_b[l].reshape(1, -1))

        pr = _in_proj(xp, mod_p[1], mod_p[0], w_groups, tm=256)
        oa, s_new_p = _delta_prompt(pr["qkv_a"], pr["z"], pr["small"], conv_w[l], alog_row, dt_row, norm_g, bp)
        ik16 = pr["small"][:, SM_IK:SM_IK + IDX_DIM].astype(BF16)
        vt16 = jnp.swapaxes(pr["v_b16"].reshape(bp, seq // ck, ck, ATT_WIDTH), 2, 3)
        tq = 128
        iq_hm = jnp.swapaxes(pr["iq"].reshape(n_p // tq, tq, IDX_HEADS, IDX_DIM), 1, 2).reshape(-1, IDX_DIM)
        ob = _attn_prompt(pr["q_b"], iq_hm, pr["small"], pr["k_b16"], vt16, ik16, bp, tq=tq, ck=ck)
        x1 = _mix(oa, ob, pr["gates"], xp, mod_p[2], wa16, wb16, wo16, *ln1, tm=512)
        if l % 2 == 0:
            xp = _dense_ffn(x1, mod_p[4], mod_p[3], mod_p[5], j, w_ffn_gate, w_ffn_up, w_ffn_down, *ln2,
                            tm_mod=512, tm_ffn=1024)
        else:
            xp = _moe_ffn(x1, mod_p[4], mod_p[3], mod_p[5], w_router[j], j, w_exp_gate, w_exp_up, w_exp_down,
                          *ln2, tm_mod=512, tm_e=1024)
        conv_p = pr["qkv_a"].reshape(bp, seq, -1)[:, seq - (CONV_W - 1):, :]
        outs_p.append((conv_p, s_new_p, pr["k_b"].reshape(bp, seq, ATT_HEADS, ATT_DIM),
                       pr["v_b"].reshape(bp, seq, ATT_HEADS, ATT_DIM),
                       pr["small"][:, SM_IK:SM_IK + IDX_DIM].reshape(bp, seq, IDX_DIM)))

        sr = _in_proj(xs, mod_s[1], mod_s[0], w_groups, tm=bs)
        oa_s, s_new_s = _delta_sample(sr["qkv_a"], state_conv[l], sr["z"], sr["small"], conv_w[l], alog_row, dt_row,
                                      norm_g, l, state_delta)
        ik_s = sr["small"][:, SM_IK:SM_IK + IDX_DIM]
        iq3 = sr["iq"].astype(F32).reshape(bs, IDX_HEADS, IDX_DIM)
        iw3 = sr["small"][:, SM_IW:SM_IW + IDX_HEADS].reshape(bs, IDX_HEADS, 1)
        scores = _sample_scores(l, page_table, iq3, iw3, ik_s.reshape(bs, 1, IDX_DIM), cache_idx_t)
        selb = _sample_select(scores.reshape(bs, -1)).reshape(scores.shape)
        head_rows = lambda a: jnp.pad(a.astype(F32).reshape(bs, ATT_HEADS, ATT_DIM), ((0, 0), (0, 8 - ATT_HEADS), (0, 0)))
        selb4 = jnp.repeat(selb, ATT_HEADS, axis=-1)
        ob_s = _sample_attend(l, page_table, head_rows(sr["q_b"]), head_rows(sr["k_b"]), head_rows(sr["v_b"]),
                              selb4, selb, cache_k4, cache_v4)[:, :ATT_HEADS, :].reshape(bs, ATT_WIDTH)
        x1s = _mix(oa_s, ob_s, sr["gates"], xs, mod_s[2], wa16, wb16, wo16, *ln1, tm=bs)
        if l % 2 == 0:
            xs = _dense_ffn(x1s, mod_s[4], mod_s[3], mod_s[5], j, w_ffn_gate, w_ffn_up, w_ffn_down, *ln2,
                            tm_mod=bs, tm_ffn=bs)
        else:
            xs = _moe_ffn(x1s, mod_s[4], mod_s[3], mod_s[5], w_router[j], j, w_exp_gate, w_exp_up, w_exp_down,
                          *ln2, tm_mod=bs, tm_e=64)
        conv_s = jnp.concatenate([state_conv[l][:, 1:, :], sr["qkv_a"][:, None, :]], axis=1)
        outs_s.append((conv_s, s_new_s, sr["k_b"].reshape(bs, 1, ATT_HEADS, ATT_DIM),
                       sr["v_b"].reshape(bs, 1, ATT_HEADS, ATT_DIM), ik_s.reshape(bs, 1, IDX_DIM)))

    stack = lambda rows, u: jnp.stack([r[u] for r in rows])
    return (xp.reshape(bp, seq, D_MODEL), xs.reshape(bs, 1, D_MODEL),
            stack(outs_p, 2), stack(outs_p, 3), stack(outs_p, 4), stack(outs_p, 1), stack(outs_p, 0),
            stack(outs_s, 2), stack(outs_s, 3), stack(outs_s, 4), stack(outs_s, 1), stack(outs_s, 0))
```

```python
import functools

import jax
import jax.numpy as jnp
from jax import lax
from jax.experimental import pallas as pl
from jax.experimental.pallas import tpu as pltpu

F32 = jnp.float32
BF16 = jnp.bfloat16
I32 = jnp.int32
HI = lax.Precision.HIGHEST

D_MODEL = 1024
DEPTH = 4
PAGE = 128
DN_HEADS = 4
DN_DIM = 128
DN_WIDTH = DN_HEADS * DN_DIM
CONV_W = 4
DN_CHUNK = 64
ATT_HEADS = 4
ATT_DIM = 128
ATT_WIDTH = ATT_HEADS * ATT_DIM
IDX_HEADS = 8
IDX_DIM = 64
TOPK = 256
D_FF = 2816
N_EXPERTS = 8
ALPHA = (2.0 * DEPTH) ** 0.25
EPS = 1e-5
NEG = -1e30
LOG2E = 1.4426950408889634

SM_IK = 0
SM_B = 64
SM_A = 68
SM_IW = 72

VMEM_LIMIT = 56 * 1024 * 1024


def _cparams(sem):
    return pltpu.CompilerParams(dimension_semantics=sem, vmem_limit_bytes=VMEM_LIMIT)


def _sigmoid(x):
    return jax.nn.sigmoid(x)


def _silu(x):
    return x * jax.nn.sigmoid(x)


def _softplus(x):
    return jnp.maximum(x, 0.0) + jnp.log1p(jnp.exp(-jnp.abs(x)))


def _dot(a, b, precision=None):
    return jnp.dot(a, b, precision=precision, preferred_element_type=F32)


def _dot_nt(a, b, precision=None):
    return lax.dot_general(a, b, (((1,), (1,)), ((), ())), precision=precision, preferred_element_type=F32)


def _dot_tn(a, b, precision=None):
    return lax.dot_general(a, b, (((0,), (0,)), ((), ())), precision=precision, preferred_element_type=F32)


def _split(a):
    hi = a.astype(BF16)
    return hi, (a - hi.astype(F32)).astype(BF16)


def _dot3(a, b):
    return _dot(a[0], b[0]) + (_dot(a[0], b[1]) + _dot(a[1], b[0]))


def _layernorm(r, g, b):
    mu = jnp.mean(r, axis=-1, keepdims=True)
    rc = r - mu
    var = jnp.mean(rc * rc, axis=-1, keepdims=True)
    return rc * lax.rsqrt(var + EPS) * g + b


def _ada_kernel(c_ref, w_ref, b_ref, o_ref):
    o_ref[...] = _dot(_silu(c_ref[...]), w_ref[...], HI) + b_ref[...]


def _ada_all(c, w_ada, b_ada):
    rows = c.shape[0]
    nj = w_ada.shape[2] // D_MODEL
    return pl.pallas_call(
        _ada_kernel,
        grid=(DEPTH, nj),
        in_specs=[
            pl.BlockSpec((rows, D_MODEL), lambda l, j: (0, 0)),
            pl.BlockSpec((None, D_MODEL, D_MODEL), lambda l, j: (l, 0, j)),
            pl.BlockSpec((None, 1, D_MODEL), lambda l, j: (l, 0, j)),
        ],
        out_specs=pl.BlockSpec((None, rows, D_MODEL), lambda l, j: (l, 0, j)),
        out_shape=jax.ShapeDtypeStruct((DEPTH, rows, nj * D_MODEL), F32),
        compiler_params=_cparams(("arbitrary", "arbitrary")),
        name="ada_mod",
    )(c, w_ada, b_ada.reshape(DEPTH, 1, -1))


IN_GROUPS = (
    ("qkv_a", 3 * DN_WIDTH, (F32,)),
    ("z", DN_WIDTH, (F32,)),
    ("q_b", ATT_WIDTH, (BF16,)),
    ("k_b", ATT_WIDTH, (F32, BF16)),
    ("v_b", ATT_WIDTH, (F32, BF16)),
    ("iq", IDX_HEADS * IDX_DIM, (BF16,)),
    ("gates", 2 * D_MODEL, (F32,)),
    ("small", 128, (F32,)),
)


HEAD_MAJOR_ROWS = (("k_b", F32), ("v_b", F32))


def _split_w_in(w):
    o = 0
    parts = {}
    for name, width in (("qkv_a", 3 * DN_WIDTH), ("z", DN_WIDTH), ("b", DN_HEADS), ("a", DN_HEADS),
                        ("q_b", ATT_WIDTH), ("k_b", ATT_WIDTH), ("v_b", ATT_WIDTH),
                        ("iq", IDX_HEADS * IDX_DIM), ("ik", IDX_DIM), ("iw", IDX_HEADS), ("gates", 2 * D_MODEL)):
        parts[name] = w[:, o:o + width]
        o += width
    small = jnp.concatenate(
        [parts["ik"], parts["b"], parts["a"], parts["iw"],
         jnp.zeros((w.shape[0], 128 - IDX_DIM - 2 * DN_HEADS - IDX_HEADS), w.dtype)], axis=1)
    parts["small"] = small
    return [parts[name].astype(BF16) for name, _, _ in IN_GROUPS]


def _in_kernel(x_ref, sc_ref, sh_ref, *refs):
    n = len(IN_GROUPS)
    w_refs, o_refs = refs[:n], refs[n:]
    h = (x_ref[...] * (1.0 + sc_ref[...]) + sh_ref[...]).astype(BF16)
    k = 0
    for (name, width, dtypes), w_ref in zip(IN_GROUPS, w_refs):
        y = _dot(h, w_ref[...])
        for dt in dtypes:
            if (name, dt) in HEAD_MAJOR_ROWS:
                rows = y.shape[0]
                for hd in range(ATT_HEADS):
                    o_refs[k][pl.ds(hd, rows, stride=ATT_HEADS), :] = y[:, hd * ATT_DIM:(hd + 1) * ATT_DIM]
            else:
                o_refs[k][...] = y.astype(dt)
            k += 1


def _in_proj(x, sc, sh, w_groups, tm):
    n = x.shape[0]
    nb, r, _ = sc.shape
    nt = n // tm
    tiles_per_b = nt // nb
    mod_spec = pl.BlockSpec((None, r, D_MODEL), lambda i: (i // tiles_per_b, 0, 0))
    in_specs = [pl.BlockSpec((tm, D_MODEL), lambda i: (i, 0)), mod_spec, mod_spec]
    for (name, width, _), w in zip(IN_GROUPS, w_groups):
        in_specs.append(pl.BlockSpec((D_MODEL, width), lambda i: (0, 0), pipeline_mode=pl.Buffered(1)))
    out_specs, out_shape = [], []
    for name, width, dtypes in IN_GROUPS:
        for dt in dtypes:
            if (name, dt) in HEAD_MAJOR_ROWS:
                out_specs.append(pl.BlockSpec((tm * ATT_HEADS, ATT_DIM), lambda i: (i, 0)))
                out_shape.append(jax.ShapeDtypeStruct((n * ATT_HEADS, ATT_DIM), dt))
            else:
                out_specs.append(pl.BlockSpec((tm, width), lambda i: (i, 0)))
                out_shape.append(jax.ShapeDtypeStruct((n, width), dt))
    outs = pl.pallas_call(
        _in_kernel,
        grid=(nt,),
        in_specs=in_specs,
        out_specs=out_specs,
        out_shape=out_shape,
        compiler_params=_cparams(("parallel",)),
        name="in_proj",
    )(x, sc, sh, *w_groups)
    names = []
    for name, _, dtypes in IN_GROUPS:
        for dt in dtypes:
            names.append(name if dt == dtypes[0] else name + "16")
    return dict(zip(names, outs))


def _delta_kernel(qkv_ref, z_ref, sm_ref, cw_ref, alog_ref, dt_ref, ng_ref, o_ref, s_out_ref, xs, s_sc, *, tt):
    i = pl.program_id(1)
    nc = tt // DN_CHUNK
    c = DN_CHUNK

    @pl.when(i == 0)
    def _():
        xs[0:8, :] = jnp.zeros((8, 3 * DN_WIDTH), F32)
        s_sc[...] = jnp.zeros_like(s_sc)

    xs[8:8 + tt, :] = qkv_ref[...]
    cw = cw_ref[...]
    y = xs[8:8 + tt, :] * cw[3:4, :]
    for j in range(1, CONV_W):
        y = y + xs[8 - j:8 - j + tt, :] * cw[3 - j:4 - j, :]
    act = _silu(y)
    xs[0:8, :] = xs[tt:tt + 8, :]

    sm = sm_ref[...]
    beta_all = _sigmoid(sm)
    g_all = -jnp.exp(alog_ref[...]) * _softplus(sm + dt_ref[...])
    r_io = lax.broadcasted_iota(I32, (tt, tt), 0)
    c_io = lax.broadcasted_iota(I32, (tt, tt), 1)
    lblk = jnp.where((r_io // c == c_io // c) & (r_io >= c_io), 1.0, 0.0).astype(F32)
    gcum = _dot(lblk, g_all, HI)
    gcum_t = gcum.T

    ri = lax.broadcasted_iota(I32, (c, c), 0)
    ci = lax.broadcasted_iota(I32, (c, c), 1)
    incl = ri >= ci
    strict = ri > ci
    eye = jnp.where(ri == ci, 1.0, 0.0).astype(F32)
    ng = ng_ref[...]

    blocks = [(h, cc) for h in range(DN_HEADS) for cc in range(nc)]
    qs, ks, gis, egs, dmats, qks, nms, rhss = {}, {}, {}, {}, {}, {}, {}, {}
    for h in range(DN_HEADS):
        lo, hi = h * DN_DIM, (h + 1) * DN_DIM
        q_h = act[:, lo:hi]
        k_h = act[:, DN_WIDTH + lo:DN_WIDTH + hi]
        v_h = act[:, 2 * DN_WIDTH + lo:2 * DN_WIDTH + hi]
        q_h = q_h * lax.rsqrt(jnp.sum(q_h * q_h, axis=-1, keepdims=True) + 1e-6) * (DN_DIM ** -0.5)
        k_h = k_h * lax.rsqrt(jnp.sum(k_h * k_h, axis=-1, keepdims=True) + 1e-6)
        for cc in range(nc):
            r0, r1 = cc * c, (cc + 1) * c
            qc, kc, vc = q_h[r0:r1], k_h[r0:r1], v_h[r0:r1]
            gi = gcum[r0:r1, SM_A + h:SM_A + h + 1]
            gj = gcum_t[SM_A + h:SM_A + h + 1, r0:r1]
            bi = beta_all[r0:r1, SM_B + h:SM_B + h + 1]
            dmat = jnp.exp(jnp.where(incl, gi - gj, -jnp.inf))
            eg = jnp.exp(gi)
            key = (h, cc)
            qs[key], ks[key], gis[key], egs[key], dmats[key] = qc, kc, gi, eg, dmat
            qks[key] = _dot_nt(qc, kc) * dmat
            nms[key] = jnp.where(strict, -(bi * _dot_nt(kc, kc) * dmat), 0.0)
            rhss[key] = jnp.concatenate([vc * bi, kc * (bi * eg)], axis=1)

    tms = {key: eye + nms[key] for key in blocks}
    ps = {key: _split(nms[key]) for key in blocks}
    for _ in range(5):
        for key in blocks:
            p2 = _dot3(ps[key], ps[key])
            ps[key] = _split(p2)
            tms[key] = tms[key] + _dot3(_split(tms[key]), ps[key])
    sols = {key: _dot3(_split(tms[key]), _split(rhss[key])) for key in blocks}

    states = [s_sc[h] for h in range(DN_HEADS)]
    for cc in range(nc):
        r0, r1 = cc * c, (cc + 1) * c
        for h in range(DN_HEADS):
            lo, hi = h * DN_DIM, (h + 1) * DN_DIM
            key = (h, cc)
            s_h = states[h]
            gi, eg = gis[key], egs[key]
            u, w = sols[key][:, :DN_DIM], sols[key][:, DN_DIM:]
            v_new = u - _dot(w, s_h)
            o = _dot(qs[key] * eg, s_h) + _dot(qks[key], v_new)
            g_last = gi[c - 1:c, :]
            states[h] = s_h * jnp.exp(g_last) + _dot_tn(ks[key] * jnp.exp(g_last - gi), v_new)
            o = o * lax.rsqrt(jnp.mean(o * o, axis=-1, keepdims=True) + EPS) * ng
            o_ref[r0:r1, lo:hi] = o * _silu(z_ref[r0:r1, lo:hi])
    for h in range(DN_HEADS):
        s_sc[h] = states[h]

    @pl.when(i == pl.num_programs(1) - 1)
    def _():
        s_out_ref[...] = s_sc[...]


def _delta_prompt(qkv, z, small, conv_w, alog_row, dt_row, norm_g, nb, tt=256):
    n = qkv.shape[0]
    nt = n // nb // tt
    row = lambda b, i: (b * nt + i, 0)
    const = lambda b, i: (0, 0)
    return pl.pallas_call(
        functools.partial(_delta_kernel, tt=tt),
        grid=(nb, nt),
        in_specs=[
            pl.BlockSpec((tt, 3 * DN_WIDTH), row),
            pl.BlockSpec((tt, DN_WIDTH), row),
            pl.BlockSpec((tt, 128), row),
            pl.BlockSpec((CONV_W, 3 * DN_WIDTH), const),
            pl.BlockSpec((1, 128), const),
            pl.BlockSpec((1, 128), const),
            pl.BlockSpec((1, DN_DIM), const),
        ],
        out_specs=[
            pl.BlockSpec((tt, DN_WIDTH), row),
            pl.BlockSpec((None, DN_HEADS, DN_DIM, DN_DIM), lambda b, i: (b, 0, 0, 0)),
        ],
        out_shape=[
            jax.ShapeDtypeStruct((n, DN_WIDTH), F32),
            jax.ShapeDtypeStruct((nb, DN_HEADS, DN_DIM, DN_DIM), F32),
        ],
        scratch_shapes=[
            pltpu.VMEM((tt + 8, 3 * DN_WIDTH), F32),
            pltpu.VMEM((DN_HEADS, DN_DIM, DN_DIM), F32),
        ],
        compiler_params=_cparams(("arbitrary", "arbitrary")),
        name="delta_prompt",
    )(qkv, z, small, conv_w, alog_row, dt_row, norm_g)


def _sdelta_kernel(qkv_ref, buf_ref, z_ref, sm_ref, cw_ref, alog_ref, dt_ref, ng_ref, s_ref, o_ref, s_out_ref, *, rows):
    cw = cw_ref[...]
    y = qkv_ref[...] * cw[3:4, :]
    for j in range(CONV_W - 1):
        y = y + buf_ref[:, j, :] * cw[j:j + 1, :]
    act = _silu(y)
    sm = sm_ref[...]
    beta_all = _sigmoid(sm)
    g_all = -jnp.exp(alog_ref[...]) * _softplus(sm + dt_ref[...])
    ng = ng_ref[...]
    sub = lax.broadcasted_iota(I32, (8, DN_DIM), 0)
    for h in range(DN_HEADS):
        lo, hi = h * DN_DIM, (h + 1) * DN_DIM
        q_h = act[:, lo:hi]
        k_h = act[:, DN_WIDTH + lo:DN_WIDTH + hi]
        v_h = act[:, 2 * DN_WIDTH + lo:2 * DN_WIDTH + hi]
        q_h = q_h * lax.rsqrt(jnp.sum(q_h * q_h, axis=-1, keepdims=True) + 1e-6) * (DN_DIM ** -0.5)
        k_h = k_h * lax.rsqrt(jnp.sum(k_h * k_h, axis=-1, keepdims=True) + 1e-6)
        g = g_all[:, SM_A + h:SM_A + h + 1]
        b = beta_all[:, SM_B + h:SM_B + h + 1]
        eg = jnp.exp(g)
        u = v_h * b
        w = k_h * (b * eg)
        qe = q_h * eg
        qk = jnp.sum(q_h * k_h, axis=-1, keepdims=True)
        z_h = z_ref[:, lo:hi]
        for r in range(rows):
            s = s_ref[r, h]
            w8 = jnp.where(sub == 0, w[r:r + 1], 0.0)
            q8 = jnp.where(sub == 0, qe[r:r + 1], 0.0)
            k8 = jnp.where(sub == 0, k_h[r:r + 1], 0.0)
            v_new = u[r:r + 1] - _dot(w8, s, HI)[0:1]
            o = _dot(q8, s, HI)[0:1] + qk[r:r + 1] * v_new
            vn8 = jnp.where(sub == 0, v_new, 0.0)
            s_out_ref[r, h] = s * eg[r:r + 1] + _dot_tn(k8, vn8, HI)
            o = o * lax.rsqrt(jnp.mean(o * o, axis=-1, keepdims=True) + EPS) * ng
            o_ref[r:r + 1, lo:hi] = o * _silu(z_h[r:r + 1])


def _delta_sample(qkv, buf, z, small, conv_w, alog_row, dt_row, norm_g, layer, state, rows=8):
    n = qkv.shape[0]
    row = lambda i: (i, 0)
    const = lambda i: (0, 0)
    return pl.pallas_call(
        functools.partial(_sdelta_kernel, rows=rows),
        grid=(n // rows,),
        in_specs=[
            pl.BlockSpec((rows, 3 * DN_WIDTH), row),
            pl.BlockSpec((rows, CONV_W - 1, 3 * DN_WIDTH), lambda i: (i, 0, 0)),
            pl.BlockSpec((rows, DN_WIDTH), row),
            pl.BlockSpec((rows, 128), row),
            pl.BlockSpec((CONV_W, 3 * DN_WIDTH), const),
            pl.BlockSpec((1, 128), const),
            pl.BlockSpec((1, 128), const),
            pl.BlockSpec((1, DN_DIM), const),
            pl.BlockSpec((None, rows, DN_HEADS, DN_DIM, DN_DIM), lambda i: (layer, i, 0, 0, 0)),
        ],
        out_specs=[
            pl.BlockSpec((rows, DN_WIDTH), row),
            pl.BlockSpec((rows, DN_HEADS, DN_DIM, DN_DIM), lambda i: (i, 0, 0, 0)),
        ],
        out_shape=[
            jax.ShapeDtypeStruct((n, DN_WIDTH), F32),
            jax.ShapeDtypeStruct(state.shape[1:], F32),
        ],
        compiler_params=_cparams(("parallel",)),
        name="delta_sample",
    )(qkv, buf, z, small, conv_w, alog_row, dt_row, norm_g, state)


def _score_keys(s):
    s = jnp.where(s == 0.0, 0.0, s)
    k = pltpu.bitcast(s, I32)
    return jnp.where(k < 0, k ^ jnp.int32(0x7FFFFFFF), k)


SEARCH_ROUND = 4


def _alibi_slope(h):
    return 2.0 ** (-8.0 * (h + 1) / ATT_HEADS)


def _sublane_allreduce(x, op):
    for shift in (4, 2, 1):
        x = op(x, pltpu.roll(x, shift, axis=0))
    return x


def _attn_kernel(q_ref, iq_ref, sm_ref, k_ref, vt_ref, ik_ref, o_ref, keys, alibi, acc_sc, *, tq, ck):
    i = pl.program_id(1)
    nk = (i * tq + tq + ck - 1) // ck
    g8 = ck // 8
    t = i * tq + lax.broadcasted_iota(I32, (8, tq), 1)
    sub = lax.broadcasted_iota(I32, (ck, 1), 0)
    sub3 = lax.broadcasted_iota(I32, (g8, 8, 1), 0) * 8 + lax.broadcasted_iota(I32, (g8, 8, 1), 1)

    @pl.when((pl.program_id(0) == 0) & (i == 0))
    def _():
        for h in range(ATT_HEADS):
            alibi[h] = jnp.broadcast_to(sub.astype(F32) * (_alibi_slope(h) * LOG2E), (ck, tq))

    sm_t = sm_ref[...].T
    iw = [jnp.broadcast_to(sm_t[SM_IW + h:SM_IW + h + 1, :] * (IDX_HEADS ** -0.5 * IDX_DIM ** -0.5), (8, tq))
          for h in range(IDX_HEADS)]
    iq = iq_ref[...]

    def score_chunk(c, diagonal):
        off = pl.multiple_of(c * ck, ck)
        d = _dot_nt(ik_ref[pl.ds(off, ck), :], iq)
        s = jnp.maximum(d[:, 0:tq].reshape(g8, 8, tq), 0.0) * iw[0][None]
        for h in range(1, IDX_HEADS):
            s = s + jnp.maximum(d[:, h * tq:(h + 1) * tq].reshape(g8, 8, tq), 0.0) * iw[h][None]
        if diagonal:
            s = jnp.where(off + sub3 <= t[None], s, -jnp.inf)
        keys[c] = _score_keys(s)

    def score_full(c, carry):
        score_chunk(c, False)
        return carry

    lax.fori_loop(0, nk - 1, score_full, 0)
    score_chunk(nk - 1, True)

    kth = jnp.minimum(t + 1, TOPK)

    def count(pred_fn):
        def body(c, cnt):
            return cnt + jnp.sum(jnp.where(pred_fn(keys[c], c), 1, 0).astype(I32), axis=0)
        return _sublane_allreduce(lax.fori_loop(0, nk, body, jnp.zeros((8, tq), I32)), jnp.add)

    def search_cond(st):
        return (st[0] < 32) & (st[3] > 0)

    def search_round(st):
        it, thr, cnt_thr, _ = st
        for u in range(SEARCH_ROUND):
            cand = thr ^ jnp.left_shift(jnp.int32(1), 31 - u - it)
            cnt = count(lambda kc, c: kc >= cand[None])
            ok = cnt >= kth
            thr = jnp.where(ok, cand, thr)
            cnt_thr = jnp.where(ok, cnt, cnt_thr)
        return it + SEARCH_ROUND, thr, cnt_thr, jnp.max(jnp.where(cnt_thr != kth, 1, 0))

    _, thr, _, unresolved = lax.while_loop(
        search_cond, search_round,
        (jnp.int32(0), jnp.full((8, tq), -2 ** 31, I32), jnp.full((8, tq), 2 ** 30, I32), jnp.int32(1)))

    @pl.when(unresolved > 0)
    def _():
        need = kth - count(lambda kc, c: kc > thr[None])

        def pos_step(it, cut):
            cand = cut | jnp.left_shift(jnp.int32(1), 13 - it)
            cnt = count(lambda kc, c: (kc == thr[None]) & (c * ck + sub3 < cand[None]))
            return jnp.where(cnt < need, cand, cut)
        cut = lax.fori_loop(0, 14, pos_step, jnp.zeros((8, tq), I32))

        def drop(c, carry):
            kc = keys[c]
            keys[c] = jnp.where((kc == thr[None]) & (c * ck + sub3 > cut[None]), jnp.int32(-2 ** 31), kc)
            return carry
        lax.fori_loop(0, nk, drop, 0)

    acc_sc[...] = jnp.zeros_like(acc_sc)
    q = q_ref[...]
    qh = [q[:, h * ATT_DIM:(h + 1) * ATT_DIM] for h in range(ATT_HEADS)]
    heads = range(ATT_HEADS)

    def attend_chunk(c, carry):
        ms, ls = carry
        off = pl.multiple_of(c * ck, ck)
        selb = jnp.where(keys[c] >= thr[None], 0.0, NEG).reshape(ck, tq)
        kc = k_ref[pl.ds(off, ck), :]
        off_f = off.astype(F32)
        c_off = [off_f * (_alibi_slope(h) * LOG2E) for h in heads]
        zs = [(_dot_nt(kc[:, h * ATT_DIM:(h + 1) * ATT_DIM], qh[h]) * (ATT_DIM ** -0.5 * LOG2E)
               + (alibi[h] + selb)).reshape(g8, 8, tq) for h in heads]
        m_new = [jnp.maximum(ms[h], _sublane_allreduce(jnp.max(zs[h], axis=0), jnp.maximum) + c_off[h]) for h in heads]
        a = [jnp.exp2(ms[h] - m_new[h]) for h in heads]
        ps = [jnp.exp2(zs[h] - (m_new[h] - c_off[h])[None]) for h in heads]
        l_new = [a[h] * ls[h] + _sublane_allreduce(jnp.sum(ps[h], axis=0), jnp.add) for h in heads]
        for h in heads:
            pv = _dot(vt_ref[c, h * ATT_DIM:(h + 1) * ATT_DIM, :], ps[h].reshape(ck, tq).astype(BF16))
            acc_sc[h] = a[h][None] * acc_sc[h] + pv.reshape(ATT_DIM // 8, 8, tq)
        return tuple(m_new), tuple(l_new)

    init = (tuple(jnp.full((8, tq), NEG, F32) for _ in heads), tuple(jnp.zeros((8, tq), F32) for _ in heads))
    _, ls = lax.fori_loop(0, nk, attend_chunk, init)
    for h in heads:
        o_ref[:, h * ATT_DIM:(h + 1) * ATT_DIM] = (acc_sc[h] / ls[h][None]).reshape(ATT_DIM, tq).T


def _attn_prompt(q16, iq16, small, k16, vt16, ik16, nb, tq=128, ck=512):
    n = q16.shape[0]
    s = n // nb
    nq = s // tq
    row = lambda b, i: (b * nq + i, 0)
    full = lambda b, i: (b, 0)
    once = pl.Buffered(1)
    return pl.pallas_call(
        functools.partial(_attn_kernel, tq=tq, ck=ck),
        grid=(nb, nq),
        in_specs=[
            pl.BlockSpec((tq, ATT_WIDTH), row),
            pl.BlockSpec((IDX_HEADS * tq, IDX_DIM), row),
            pl.BlockSpec((tq, 128), row),
            pl.BlockSpec((s, ATT_WIDTH), full, pipeline_mode=once),
            pl.BlockSpec((None, s // ck, ATT_WIDTH, ck), lambda b, i: (b, 0, 0, 0), pipeline_mode=once),
            pl.BlockSpec((s, IDX_DIM), full, pipeline_mode=once),
        ],
        out_specs=pl.BlockSpec((tq, ATT_WIDTH), row),
        out_shape=jax.ShapeDtypeStruct((n, ATT_WIDTH), F32),
        scratch_shapes=[
            pltpu.VMEM((s // ck, ck // 8, 8, tq), I32),
            pltpu.VMEM((ATT_HEADS, ck, tq), F32),
            pltpu.VMEM((ATT_HEADS, ATT_DIM // 8, 8, tq), F32),
        ],
        compiler_params=_cparams(("arbitrary", "arbitrary")),
        name="attn_prompt",
    )(q16, iq16, small, k16, vt16, ik16)


PG = 8


def _sscore_kernel(pt_ref, iq_ref, iw_ref, ikn_ref, *refs):
    pages, o_ref = refs[:PG], refs[PG]
    j = pl.program_id(1)
    iq = iq_ref[...]
    iw = iw_ref[...] * (IDX_HEADS ** -0.5)

    @pl.when(j < pl.num_programs(1) - 1)
    def _():
        for u in range(PG):
            d = _dot(iq, pages[u][...]) * (IDX_DIM ** -0.5)
            o_ref[:, u * PAGE:(u + 1) * PAGE] = jnp.sum(jnp.maximum(d, 0.0) * iw, axis=0, keepdims=True)

    @pl.when(j == pl.num_programs(1) - 1)
    def _():
        d = _dot_nt(iq, jnp.broadcast_to(ikn_ref[...], (8, IDX_DIM)))[:, 0:1] * (IDX_DIM ** -0.5)
        s_new = jnp.sum(jnp.maximum(d, 0.0) * iw, axis=0, keepdims=True)
        lane = lax.broadcasted_iota(I32, (1, PG * PAGE), 1)
        o_ref[...] = jnp.where(lane == 0, s_new, -jnp.inf)


def _sample_scores(layer, page_table, iq3, iw3, ik_new3, cache_idx_t):
    nbatch, n_pages = page_table.shape
    nj = n_pages // PG

    def page_map(u):
        return lambda b, j, pt: (layer, pt[b, jnp.minimum(j * PG + u, n_pages - 1)], 0, 0)

    grid_spec = pltpu.PrefetchScalarGridSpec(
        num_scalar_prefetch=1,
        grid=(nbatch, nj + 1),
        in_specs=[
            pl.BlockSpec((None, IDX_HEADS, IDX_DIM), lambda b, j, pt: (b, 0, 0)),
            pl.BlockSpec((None, IDX_HEADS, 1), lambda b, j, pt: (b, 0, 0)),
            pl.BlockSpec((None, 1, IDX_DIM), lambda b, j, pt: (b, 0, 0)),
        ] + [pl.BlockSpec((None, None, IDX_DIM, PAGE), page_map(u)) for u in range(PG)],
        out_specs=pl.BlockSpec((None, 1, PG * PAGE), lambda b, j, pt: (b, 0, j)),
    )
    return pl.pallas_call(
        _sscore_kernel,
        grid_spec=grid_spec,
        out_shape=jax.ShapeDtypeStruct((nbatch, 1, (nj + 1) * PG * PAGE), F32),
        compiler_params=_cparams(("arbitrary", "arbitrary")),
        name="sample_scores",
    )(page_table, iq3, iw3, ik_new3, *([cache_idx_t] * PG))


def _sselect_kernel(s_ref, o_ref):
    keys = _score_keys(s_ref[...])
    rows, width = keys.shape
    pos = lax.broadcasted_iota(I32, (1, width), 1)

    def cnt(m):
        return jnp.sum(jnp.where(m, 1, 0).astype(I32), axis=-1, keepdims=True)

    def search_step(it, thr):
        cand = thr ^ jnp.left_shift(jnp.int32(1), 31 - it)
        return jnp.where(cnt(keys >= cand) >= TOPK, cand, thr)

    thr = lax.fori_loop(0, 32, search_step, jnp.full((rows, 1), -2 ** 31, I32))
    need = TOPK - cnt(keys > thr)
    eq = keys == thr

    def pos_step(it, cut):
        cand = cut | jnp.left_shift(jnp.int32(1), 14 - it)
        return jnp.where(cnt(eq & (pos < cand)) < need, cand, cut)

    cut = lax.fori_loop(0, 15, pos_step, jnp.zeros((rows, 1), I32))
    sel = (keys > thr) | (eq & (pos <= cut))
    o_ref[...] = jnp.where(sel, 0.0, NEG)


def _sample_select(scores):
    return pl.pallas_call(
        _sselect_kernel,
        out_shape=jax.ShapeDtypeStruct(scores.shape, F32),
        compiler_params=pltpu.CompilerParams(vmem_limit_bytes=VMEM_LIMIT),
        name="sample_select",
    )(scores)


PAGE_ROWS = PAGE * ATT_HEADS


def _sattn_kernel(pt_ref, q_ref, kn_ref, vn_ref, selb_ref, seln_ref, *refs, past):
    kp, vp = refs[:PG], refs[PG:2 * PG]
    o_ref, lg_sc, l_sc, pn_sc, acc_sc = refs[2 * PG:2 * PG + 5]
    ph = pl.program_id(1)
    j = pl.program_id(2)
    scale = ATT_DIM ** -0.5
    q8 = q_ref[...]
    row = lax.broadcasted_iota(I32, (8, 1), 0)
    lane = lax.broadcasted_iota(I32, (1, PAGE_ROWS), 1)
    slope = jnp.exp2(-8.0 * (row + 1).astype(F32) / ATT_HEADS)
    own = jnp.bitwise_and(lane, ATT_HEADS - 1) == row
    tok = jnp.right_shift(lane, 2)

    @pl.when(ph == 0)
    def _():
        for u in range(PG):
            dist = (past - ((j * PG + u) * PAGE + tok)).astype(F32)
            lg = _dot_nt(q8, kp[u][...]) * scale - slope * dist + selb_ref[:, u * PAGE_ROWS:(u + 1) * PAGE_ROWS]
            lg_sc[j * PG + u] = jnp.where(own, lg, NEG)

    @pl.when((ph == 1) & (j == 0))
    def _():
        allg = lg_sc[...]
        lgn = jnp.sum(q8 * kn_ref[...], axis=-1, keepdims=True) * scale + seln_ref[:, 0:1]
        m = jnp.maximum(jnp.max(jnp.max(allg, axis=0), axis=-1, keepdims=True), lgn)
        p = jnp.exp(allg - m)
        lg_sc[...] = p
        pn = jnp.exp(lgn - m)
        l_sc[...] = jnp.sum(jnp.sum(p, axis=0), axis=-1, keepdims=True) + pn
        pn_sc[...] = pn
        acc_sc[...] = jnp.zeros_like(acc_sc)

    @pl.when(ph == 1)
    def _():
        acc = acc_sc[...]
        for u in range(PG):
            acc = acc + _dot(lg_sc[j * PG + u], vp[u][...])
        acc_sc[...] = acc

    @pl.when((ph == 1) & (j == pl.num_programs(2) - 1))
    def _():
        o_ref[...] = (acc_sc[...] + pn_sc[...] * vn_ref[...]) / l_sc[...]


def _sample_attend(layer, page_table, q8, kn8, vn8, selb4, selb, cache_k4, cache_v4):
    nbatch, n_pages = page_table.shape
    nj = n_pages // PG
    past = n_pages * PAGE

    def k_map(u):
        return lambda b, ph, j, pt: (layer, pt[b, jnp.where(ph == 0, j, nj - 1) * PG + u], 0, 0)

    def v_map(u):
        return lambda b, ph, j, pt: (layer, pt[b, jnp.where(ph == 0, 0, j) * PG + u], 0, 0)

    vec = pl.BlockSpec((None, 8, ATT_DIM), lambda b, ph, j, pt: (b, 0, 0))
    grid_spec = pltpu.PrefetchScalarGridSpec(
        num_scalar_prefetch=1,
        grid=(nbatch, 2, nj),
        in_specs=[
            vec, vec, vec,
            pl.BlockSpec((None, 1, PG * PAGE_ROWS), lambda b, ph, j, pt: (b, 0, jnp.where(ph == 0, j, nj - 1))),
            pl.BlockSpec((None, 1, 128), lambda b, ph, j, pt: (b, 0, past // 128)),
        ] + [pl.BlockSpec((None, None, PAGE_ROWS, ATT_DIM), k_map(u)) for u in range(PG)]
          + [pl.BlockSpec((None, None, PAGE_ROWS, ATT_DIM), v_map(u)) for u in range(PG)],
        out_specs=vec,
        scratch_shapes=[
            pltpu.VMEM((n_pages, 8, PAGE_ROWS), F32),
            pltpu.VMEM((8, 1), F32),
            pltpu.VMEM((8, 1), F32),
            pltpu.VMEM((8, ATT_DIM), F32),
        ],
    )
    return pl.pallas_call(
        functools.partial(_sattn_kernel, past=past),
        grid_spec=grid_spec,
        out_shape=jax.ShapeDtypeStruct((nbatch, 8, ATT_DIM), F32),
        compiler_params=_cparams(("arbitrary", "arbitrary", "arbitrary")),
        name="sample_attend",
    )(page_table, q8, kn8, vn8, selb4, selb, *([cache_k4] * PG), *([cache_v4] * PG))


def _mix_kernel(oa_ref, ob_ref, gates_ref, x_ref, gm_ref, wa_ref, wb_ref, wo_ref, g_ref, b_ref, o_ref):
    br_a = _dot(oa_ref[...].astype(BF16), wa_ref[...])
    br_b = _dot(ob_ref[...].astype(BF16), wb_ref[...])
    merged = _sigmoid(gates_ref[:, :D_MODEL]) * br_a + _sigmoid(gates_ref[:, D_MODEL:]) * br_b
    y = _dot(merged.astype(BF16), wo_ref[...])
    r = ALPHA * x_ref[...] + gm_ref[...] * y
    o_ref[...] = _layernorm(r, g_ref[...], b_ref[...])


def _mix(oa, ob, gates, x, gm, wa, wb, wo, ln_g, ln_b, tm):
    n = x.shape[0]
    nb, r, _ = gm.shape
    nt = n // tm
    tiles_per_b = nt // nb
    row = lambda i: (i, 0)
    const = lambda i: (0, 0)
    return pl.pallas_call(
        _mix_kernel,
        grid=(nt,),
        in_specs=[
            pl.BlockSpec((tm, DN_WIDTH), row),
            pl.BlockSpec((tm, ATT_WIDTH), row),
            pl.BlockSpec((tm, 2 * D_MODEL), row),
            pl.BlockSpec((tm, D_MODEL), row),
            pl.BlockSpec((None, r, D_MODEL), lambda i: (i // tiles_per_b, 0, 0)),
            pl.BlockSpec((DN_WIDTH, D_MODEL), const),
            pl.BlockSpec((ATT_WIDTH, D_MODEL), const),
            pl.BlockSpec((D_MODEL, D_MODEL), const),
            pl.BlockSpec((1, D_MODEL), const),
            pl.BlockSpec((1, D_MODEL), const),
        ],
        out_specs=pl.BlockSpec((tm, D_MODEL), row),
        out_shape=jax.ShapeDtypeStruct((n, D_MODEL), F32),
        compiler_params=_cparams(("parallel",)),
        name="mix_out",
    )(oa, ob, gates, x, gm, wa, wb, wo, ln_g, ln_b)


def _route_kernel(x_ref, sc_ref, sh_ref, wr_ref, o_ref, e_ref, g_ref):
    h = x_ref[...] * (1.0 + sc_ref[...]) + sh_ref[...]
    o_ref[...] = h.astype(o_ref.dtype)
    logits = _dot(h, wr_ref[...], HI)
    lane = lax.broadcasted_iota(I32, logits.shape, 1)
    logits = jnp.where(lane < N_EXPERTS, logits, -jnp.inf)
    e = jnp.exp(logits - jnp.max(logits, axis=-1, keepdims=True))
    p = e / jnp.sum(e, axis=-1, keepdims=True)
    p1 = jnp.max(p, axis=-1, keepdims=True)
    e1 = jnp.min(jnp.where(p == p1, lane, 128), axis=-1, keepdims=True)
    rest = jnp.where((lane == e1) | (lane >= N_EXPERTS), -1.0, p)
    p2 = jnp.max(rest, axis=-1, keepdims=True)
    e2 = jnp.min(jnp.where(rest == p2, lane, 128), axis=-1, keepdims=True)
    tot = p1 + p2
    e_ref[...] = jnp.concatenate([e1, e2], axis=1)
    g_ref[...] = jnp.concatenate([p1 / tot, p2 / tot], axis=1)


def _ffn_route(x, sc, sh, tm, w_router):
    n = x.shape[0]
    nb, r, _ = sc.shape
    nt = n // tm
    tiles_per_b = nt // nb
    row = lambda i: (i, 0)
    mod_spec = pl.BlockSpec((None, r, D_MODEL), lambda i: (i // tiles_per_b, 0, 0))
    in_specs = [pl.BlockSpec((tm, D_MODEL), row), mod_spec, mod_spec]
    wr = jnp.pad(w_router, ((0, 0), (0, 128 - N_EXPERTS)))
    return pl.pallas_call(
        _route_kernel, grid=(nt,),
        in_specs=in_specs + [pl.BlockSpec((D_MODEL, 128), lambda i: (0, 0))],
        out_specs=[pl.BlockSpec((tm, D_MODEL), row), pl.BlockSpec((tm, 2), row), pl.BlockSpec((tm, 2), row)],
        out_shape=[jax.ShapeDtypeStruct((n, D_MODEL), F32), jax.ShapeDtypeStruct((n, 2), I32),
                   jax.ShapeDtypeStruct((n, 2), F32)],
        compiler_params=_cparams(("parallel",)), name="ffn_route",
    )(x, sc, sh, wr)


def _gffn_kernel(te_ref, nu_ref, x_ref, *refs, modulate):
    if modulate:
        sc_ref, sh_ref = refs[:2]
        refs = refs[2:]
    wg_ref, wu_ref, wd_ref, o_ref, acc, x16 = refs
    t = pl.program_id(0)
    f = pl.program_id(1)

    @pl.when((t < nu_ref[0]) & (f == 0))
    def _():
        x = x_ref[...]
        if modulate:
            x = x * (1.0 + sc_ref[...]) + sh_ref[...]
        x16[...] = x.astype(BF16)

    @pl.when(t < nu_ref[0])
    def _():
        x = x16[...]
        a = _dot(x, wg_ref[...].astype(BF16))
        b = _dot(x, wu_ref[...].astype(BF16))
        y = _dot((_silu(a) * b).astype(BF16), wd_ref[...].astype(BF16))

        @pl.when(f == 0)
        def _():
            acc[...] = y

        @pl.when(f > 0)
        def _():
            acc[...] += y

    @pl.when(f == pl.num_programs(1) - 1)
    def _():
        o_ref[...] = jnp.where(t < nu_ref[0], acc[...], 0.0)


def _grouped_ffn(x, tile_expert, n_used, layer, w_gate, w_up, w_down, tm, mod=None, tf=256):
    n = x.shape[0]
    nt = n // tm
    nf = D_FF // tf
    mod_specs, mod_args = [], []
    if mod is not None:
        nb, r, _ = mod[0].shape
        tiles_per_b = nt // nb
        mod_specs = [pl.BlockSpec((None, r, D_MODEL), lambda t, f, te, nu: (t // tiles_per_b, 0, 0))] * 2
        mod_args = list(mod)
    grid_spec = pltpu.PrefetchScalarGridSpec(
        num_scalar_prefetch=2,
        grid=(nt, nf),
        in_specs=[pl.BlockSpec((tm, D_MODEL), lambda t, f, te, nu: (t, 0))] + mod_specs + [
            pl.BlockSpec((None, None, D_MODEL, tf), lambda t, f, te, nu: (layer, te[t], 0, f)),
            pl.BlockSpec((None, None, D_MODEL, tf), lambda t, f, te, nu: (layer, te[t], 0, f)),
            pl.BlockSpec((None, None, tf, D_MODEL), lambda t, f, te, nu: (layer, te[t], f, 0)),
        ],
        out_specs=pl.BlockSpec((tm, D_MODEL), lambda t, f, te, nu: (t, 0)),
        scratch_shapes=[pltpu.VMEM((tm, D_MODEL), F32), pltpu.VMEM((tm, D_MODEL), BF16)],
    )
    return pl.pallas_call(
        functools.partial(_gffn_kernel, modulate=mod is not None),
        grid_spec=grid_spec,
        out_shape=jax.ShapeDtypeStruct((n, D_MODEL), F32),
        compiler_params=_cparams(("arbitrary", "arbitrary")),
        name="grouped_ffn",
    )(tile_expert, n_used, x, *mod_args, w_gate, w_up, w_down)


def _res_ln_kernel(x_ref, y_ref, gf_ref, g_ref, b_ref, o_ref):
    r = ALPHA * x_ref[...] + gf_ref[...] * y_ref[...]
    o_ref[...] = _layernorm(r, g_ref[...], b_ref[...])


def _res_ln2_kernel(x_ref, y0_ref, y1_ref, gt_ref, gf_ref, g_ref, b_ref, o_ref):
    gt = gt_ref[...]
    y = y0_ref[...] * gt[:, 0:1] + y1_ref[...] * gt[:, 1:2]
    r = ALPHA * x_ref[...] + gf_ref[...] * y
    o_ref[...] = _layernorm(r, g_ref[...], b_ref[...])


def _res_ln(x, ys, gate2, gf, ln_g, ln_b, tm):
    n = x.shape[0]
    nb, r, _ = gf.shape
    nt = n // tm
    tiles_per_b = nt // nb
    row = lambda i: (i, 0)
    const = lambda i: (0, 0)
    big = pl.BlockSpec((tm, D_MODEL), row)
    mod_spec = pl.BlockSpec((None, r, D_MODEL), lambda i: (i // tiles_per_b, 0, 0))
    vec = pl.BlockSpec((1, D_MODEL), const)
    if gate2 is None:
        kern, ins, specs = _res_ln_kernel, (x, ys[0], gf, ln_g, ln_b), [big, big, mod_spec, vec, vec]
    else:
        kern = _res_ln2_kernel
        ins = (x, ys[0], ys[1], gate2, gf, ln_g, ln_b)
        specs = [big, big, big, pl.BlockSpec((tm, 2), row), mod_spec, vec, vec]
    return pl.pallas_call(
        kern, grid=(nt,), in_specs=specs, out_specs=big,
        out_shape=jax.ShapeDtypeStruct((n, D_MODEL), F32),
        compiler_params=_cparams(("parallel",)), name="res_ln",
    )(*ins)


def _dense_ffn(x, sc, sh, gf, layer, w_gate, w_up, w_down, ln_g, ln_b, tm_mod, tm_ffn):
    n = x.shape[0]
    nt = n // tm_ffn
    y = _grouped_ffn(x, jnp.zeros((nt,), I32), jnp.full((1,), nt, I32), layer,
                     w_gate[:, None], w_up[:, None], w_down[:, None], tm_ffn, mod=(sc, sh))
    return _res_ln(x, (y,), None, gf, ln_g, ln_b, tm_mod)


def _moe_ffn(groups, w_router, layer, w_gate, w_up, w_down, ln_g, ln_b, tm_e):
    routed = [_ffn_route(x, sc, sh, tm, w_router) for x, sc, sh, _, tm in groups]
    h = jnp.concatenate([r[0] for r in routed], axis=0)
    top_e = jnp.concatenate([r[1] for r in routed], axis=0)
    n = h.shape[0]
    n_assign = 2 * n
    flat_e = top_e.reshape(-1)
    experts = jnp.arange(N_EXPERTS, dtype=I32)
    onehot = flat_e[:, None] == experts[None, :]
    seen = jnp.cumsum(onehot.astype(I32), axis=0)
    counts = seen[-1]
    padded = (counts + tm_e - 1) // tm_e * tm_e
    pad_end = jnp.cumsum(padded)
    pad_start = pad_end - padded
    start = jnp.cumsum(counts) - counts
    pick = lambda table: jnp.sum(jnp.where(onehot, table, 0), axis=1)
    slot_of = (pick(pad_start[None, :]) + pick(seen) - 1).reshape(n, 2)
    n_tiles = -(-n_assign // tm_e) + N_EXPERTS
    tile_expert = jnp.minimum(
        jnp.searchsorted(pad_end, jnp.arange(n_tiles, dtype=I32) * tm_e, side="right"), N_EXPERTS - 1).astype(I32)
    n_used = (pad_end[-1] // tm_e).astype(I32).reshape(1)
    order = jnp.argsort(flat_e).astype(I32)
    within = jnp.arange(n_tiles * tm_e, dtype=I32).reshape(n_tiles, tm_e) - pad_start[tile_expert][:, None]
    live = within < counts[tile_expert][:, None]
    src = jnp.clip(start[tile_expert][:, None] + within, 0, n_assign - 1)
    slot_token = jnp.where(live, order[src.reshape(-1)].reshape(n_tiles, tm_e) // 2, 0).reshape(-1)
    yb = _grouped_ffn(h[slot_token], tile_expert, n_used, layer, w_gate, w_up, w_down, tm_e)
    outs, lo = [], 0
    for (x, _, _, gf, tm), r in zip(groups, routed):
        rows = slot_of[lo:lo + x.shape[0]]
        outs.append(_res_ln(x, (yb[rows[:, 0]], yb[rows[:, 1]]), r[2], gf, ln_g, ln_b, tm))
        lo += x.shape[0]
    return outs


def kernel(x_prompt, x_sample, cache_k, cache_v, cache_idx_k, state_delta, state_conv, page_table, c_prompt, c_sample, w_ada, b_ada, w_in, conv_w, a_log, dt_bias, dn_norm_g, w_branch, w_out, ln1_g, ln1_b, ln2_g, ln2_b, w_ffn_gate, w_ffn_up, w_ffn_down, w_router, w_exp_gate, w_exp_up, w_exp_down):
    bp, seq, _ = x_prompt.shape
    bs = x_sample.shape[0]
    n_p = bp * seq
    n_phys = cache_k.shape[1]
    past = page_table.shape[1] * PAGE

    c_all = jnp.concatenate([c_prompt, c_sample], axis=0)
    c_rows = -(-c_all.shape[0] // 8) * 8
    c_all = jnp.pad(c_all, ((0, c_rows - c_all.shape[0]), (0, 0)))
    mods = _ada_all(c_all, w_ada, b_ada)

    cache_k4 = cache_k.reshape(DEPTH, n_phys, PAGE_ROWS, ATT_DIM)
    cache_v4 = cache_v.reshape(DEPTH, n_phys, PAGE_ROWS, ATT_DIM)
    cache_idx_t = jnp.swapaxes(cache_idx_k, 2, 3)
    ck = 512

    xp = x_prompt.reshape(n_p, D_MODEL)
    xs = x_sample.reshape(bs, D_MODEL)
    lane = jnp.arange(128)
    outs_p, outs_s = [], []
    for l in range(DEPTH):
        j = l // 2
        mod_p = [mods[l, :bp, u * D_MODEL:(u + 1) * D_MODEL].reshape(bp, 1, D_MODEL) for u in range(6)]
        mod_s = [mods[l, bp:bp + bs, u * D_MODEL:(u + 1) * D_MODEL].reshape(1, bs, D_MODEL) for u in range(6)]
        w_groups = _split_w_in(w_in[l])
        head_lane = jnp.clip(lane - SM_A, 0, DN_HEADS - 1)
        in_a = (lane >= SM_A) & (lane < SM_A + DN_HEADS)
        alog_row = jnp.where(in_a, a_log[l][head_lane], 0.0).reshape(1, 128)
        dt_row = jnp.where(in_a, dt_bias[l][head_lane], 0.0).reshape(1, 128)
        norm_g = dn_norm_g[l].reshape(1, DN_DIM)
        wa16 = w_branch[l, 0].astype(BF16)
        wb16 = w_branch[l, 1].astype(BF16)
        wo16 = w_out[l].astype(BF16)
        ln1 = (ln1_g[l].reshape(1, -1), ln1_b[l].reshape(1, -1))
        ln2 = (ln2_g[l].reshape(1, -1), ln2_b[l].reshape(1, -1))

        pr = _in_proj(xp, mod_p[1], mod_p[0], w_groups, tm=256)
        oa, s_new_p = _delta_prompt(pr["qkv_a"], pr["z"], pr["small"], conv_w[l], alog_row, dt_row, norm_g, bp)
        ik16 = pr["small"][:, SM_IK:SM_IK + IDX_DIM].astype(BF16)
        vt16 = jnp.swapaxes(pr["v_b16"].reshape(bp, seq // ck, ck, ATT_WIDTH), 2, 3)
        tq = 256
        iq_hm = jnp.swapaxes(pr["iq"].reshape(n_p // tq, tq, IDX_HEADS, IDX_DIM), 1, 2).reshape(-1, IDX_DIM)
        ob = _attn_prompt(pr["q_b"], iq_hm, pr["small"], pr["k_b16"], vt16, ik16, bp, tq=tq, ck=ck)
        x1 = _mix(oa, ob, pr["gates"], xp, mod_p[2], wa16, wb16, wo16, *ln1, tm=512)
        conv_p =pr["qkv_a"].reshape(bp, seq, -1)[:, seq - (CONV_W - 1):, :]
        outs_p.append((conv_p, s_new_p, pr["k_b"].reshape(bp, seq, ATT_HEADS, ATT_DIM),
                       pr["v_b"].reshape(bp, seq, ATT_HEADS, ATT_DIM),
                       pr["small"][:, SM_IK:SM_IK + IDX_DIM].reshape(bp, seq, IDX_DIM)))

        sr = _in_proj(xs, mod_s[1], mod_s[0], w_groups, tm=bs)
        oa_s, s_new_s = _delta_sample(sr["qkv_a"], state_conv[l], sr["z"], sr["small"], conv_w[l], alog_row, dt_row,
                                      norm_g, l, state_delta)
        ik_s = sr["small"][:, SM_IK:SM_IK + IDX_DIM]
        iq3 = sr["iq"].astype(F32).reshape(bs, IDX_HEADS, IDX_DIM)
        iw3 = sr["small"][:, SM_IW:SM_IW + IDX_HEADS].reshape(bs, IDX_HEADS, 1)
        scores = _sample_scores(l, page_table, iq3, iw3, ik_s.reshape(bs, 1, IDX_DIM), cache_idx_t)
        selb = _sample_select(scores.reshape(bs, -1)).reshape(scores.shape)
        head_rows = lambda a: jnp.pad(a.astype(F32).reshape(bs, ATT_HEADS, ATT_DIM), ((0, 0), (0, 8 - ATT_HEADS), (0, 0)))
        selb4 = jnp.repeat(selb, ATT_HEADS, axis=-1)
        ob_s = _sample_attend(l, page_table, head_rows(sr["q_b"]), head_rows(sr["k_b"]), head_rows(sr["v_b"]),
                              selb4, selb, cache_k4, cache_v4)[:, :ATT_HEADS, :].reshape(bs, ATT_WIDTH)
        x1s = _mix(oa_s, ob_s, sr["gates"], xs, mod_s[2], wa16, wb16, wo16, *ln1, tm=bs)

        if l % 2 == 0:
            xp = _dense_ffn(x1, mod_p[4], mod_p[3], mod_p[5], j, w_ffn_gate, w_ffn_up, w_ffn_down, *ln2,
                            tm_mod=512, tm_ffn=1024)
            xs = _dense_ffn(x1s, mod_s[4], mod_s[3], mod_s[5], j, w_ffn_gate, w_ffn_up, w_ffn_down, *ln2,
                            tm_mod=bs, tm_ffn=bs)
        else:
            xp, xs = _moe_ffn([(x1, mod_p[4], mod_p[3], mod_p[5], 512), (x1s, mod_s[4], mod_s[3], mod_s[5], bs)],
                              w_router[j], j, w_exp_gate, w_exp_up, w_exp_down, *ln2, tm_e=1024)
        conv_s =jnp.concatenate([state_conv[l][:, 1:, :], sr["qkv_a"][:, None, :]], axis=1)
        outs_s.append((conv_s, s_new_s, sr["k_b"].reshape(bs, 1, ATT_HEADS, ATT_DIM),
                       sr["v_b"].reshape(bs, 1, ATT_HEADS, ATT_DIM), ik_s.reshape(bs, 1, IDX_DIM)))

    stack = lambda rows, u: jnp.stack([r[u] for r in rows])
    return (xp.reshape(bp, seq, D_MODEL), xs.reshape(bs, 1, D_MODEL),
            stack(outs_p, 2), stack(outs_p, 3), stack(outs_p, 4), stack(outs_p, 1), stack(outs_p, 0),
            stack(outs_s, 2), stack(outs_s, 3), stack(outs_s, 4), stack(outs_s, 1), stack(outs_s, 0))
```

```python
import functools

import jax
import jax.numpy as jnp
from jax import lax
from jax.experimental import pallas as pl
from jax.experimental.pallas import tpu as pltpu

F32 = jnp.float32
BF16 = jnp.bfloat16
I32 = jnp.int32
HI = lax.Precision.HIGHEST

D_MODEL = 1024
DEPTH = 4
PAGE = 128
DN_HEADS = 4
DN_DIM = 128
DN_WIDTH = DN_HEADS * DN_DIM
CONV_W = 4
DN_CHUNK = 64
ATT_HEADS = 4
ATT_DIM = 128
ATT_WIDTH = ATT_HEADS * ATT_DIM
IDX_HEADS = 8
IDX_DIM = 64
TOPK = 256
D_FF = 2816
N_EXPERTS = 8
ALPHA = (2.0 * DEPTH) ** 0.25
EPS = 1e-5
NEG = -1e30
LOG2E = 1.4426950408889634

SM_IK = 0
SM_B = 64
SM_A = 68
SM_IW = 72

VMEM_LIMIT = 56 * 1024 * 1024


def _cparams(sem):
    return pltpu.CompilerParams(dimension_semantics=sem, vmem_limit_bytes=VMEM_LIMIT)


def _sigmoid(x):
    return jax.nn.sigmoid(x)


def _silu(x):
    return x * jax.nn.sigmoid(x)


def _softplus(x):
    return jnp.maximum(x, 0.0) + jnp.log1p(jnp.exp(-jnp.abs(x)))


def _dot(a, b, precision=None):
    return jnp.dot(a, b, precision=precision, preferred_element_type=F32)


def _dot_nt(a, b, precision=None):
    return lax.dot_general(a, b, (((1,), (1,)), ((), ())), precision=precision, preferred_element_type=F32)


def _dot_tn(a, b, precision=None):
    return lax.dot_general(a, b, (((0,), (0,)), ((), ())), precision=precision, preferred_element_type=F32)


def _split(a):
    hi = a.astype(BF16)
    return hi, (a - hi.astype(F32)).astype(BF16)


def _dot3(a, b):
    return _dot(a[0], b[0]) + (_dot(a[0], b[1]) + _dot(a[1], b[0]))


def _layernorm(r, g, b):
    mu = jnp.mean(r, axis=-1, keepdims=True)
    rc = r - mu
    var = jnp.mean(rc * rc, axis=-1, keepdims=True)
    return rc * lax.rsqrt(var + EPS) * g + b


def _ada_kernel(c_ref, w_ref, b_ref, o_ref):
    o_ref[...] = _dot(_silu(c_ref[...]), w_ref[...], HI) + b_ref[...]


def _ada_all(c, w_ada, b_ada):
    rows = c.shape[0]
    nj = w_ada.shape[2] // D_MODEL
    return pl.pallas_call(
        _ada_kernel,
        grid=(DEPTH, nj),
        in_specs=[
            pl.BlockSpec((rows, D_MODEL), lambda l, j: (0, 0)),
            pl.BlockSpec((None, D_MODEL, D_MODEL), lambda l, j: (l, 0, j)),
            pl.BlockSpec((None, 1, D_MODEL), lambda l, j: (l, 0, j)),
        ],
        out_specs=pl.BlockSpec((None, rows, D_MODEL), lambda l, j: (l, 0, j)),
        out_shape=jax.ShapeDtypeStruct((DEPTH, rows, nj * D_MODEL), F32),
        compiler_params=_cparams(("arbitrary", "arbitrary")),
        name="ada_mod",
    )(c, w_ada, b_ada.reshape(DEPTH, 1, -1))


IN_GROUPS = (
    ("qkv_a", 3 * DN_WIDTH, (F32,)),
    ("z", DN_WIDTH, (F32,)),
    ("q_b", ATT_WIDTH, (BF16,)),
    ("k_b", ATT_WIDTH, (F32, BF16)),
    ("v_b", ATT_WIDTH, (F32, BF16)),
    ("iq", IDX_HEADS * IDX_DIM, (BF16,)),
    ("gates", 2 * D_MODEL, (F32,)),
    ("small", 128, (F32,)),
)


HEAD_MAJOR_ROWS = (("k_b", F32), ("v_b", F32))


def _split_w_in(w):
    o = 0
    parts = {}
    for name, width in (("qkv_a", 3 * DN_WIDTH), ("z", DN_WIDTH), ("b", DN_HEADS), ("a", DN_HEADS),
                        ("q_b", ATT_WIDTH), ("k_b", ATT_WIDTH), ("v_b", ATT_WIDTH),
                        ("iq", IDX_HEADS * IDX_DIM), ("ik", IDX_DIM), ("iw", IDX_HEADS), ("gates", 2 * D_MODEL)):
        parts[name] = w[:, o:o + width]
        o += width
    small = jnp.concatenate(
        [parts["ik"], parts["b"], parts["a"], parts["iw"],
         jnp.zeros((w.shape[0], 128 - IDX_DIM - 2 * DN_HEADS - IDX_HEADS), w.dtype)], axis=1)
    parts["small"] = small
    return [parts[name].astype(BF16) for name, _, _ in IN_GROUPS]


def _in_kernel(x_ref, sc_ref, sh_ref, *refs):
    n = len(IN_GROUPS)
    w_refs, o_refs = refs[:n], refs[n:]
    h = (x_ref[...] * (1.0 + sc_ref[...]) + sh_ref[...]).astype(BF16)
    k = 0
    for (name, width, dtypes), w_ref in zip(IN_GROUPS, w_refs):
        y = _dot(h, w_ref[...])
        for dt in dtypes:
            if (name, dt) in HEAD_MAJOR_ROWS:
                rows = y.shape[0]
                for hd in range(ATT_HEADS):
                    o_refs[k][pl.ds(hd, rows, stride=ATT_HEADS), :] = y[:, hd * ATT_DIM:(hd + 1) * ATT_DIM]
            else:
                o_refs[k][...] = y.astype(dt)
            k += 1


def _in_proj(x, sc, sh, w_groups, tm):
    n = x.shape[0]
    nb, r, _ = sc.shape
    nt = n // tm
    tiles_per_b = nt // nb
    mod_spec = pl.BlockSpec((None, r, D_MODEL), lambda i: (i // tiles_per_b, 0, 0))
    in_specs = [pl.BlockSpec((tm, D_MODEL), lambda i: (i, 0)), mod_spec, mod_spec]
    for (name, width, _), w in zip(IN_GROUPS, w_groups):
        in_specs.append(pl.BlockSpec((D_MODEL, width), lambda i: (0, 0), pipeline_mode=pl.Buffered(1)))
    out_specs, out_shape = [], []
    for name, width, dtypes in IN_GROUPS:
        for dt in dtypes:
            if (name, dt) in HEAD_MAJOR_ROWS:
                out_specs.append(pl.BlockSpec((tm * ATT_HEADS, ATT_DIM), lambda i: (i, 0)))
                out_shape.append(jax.ShapeDtypeStruct((n * ATT_HEADS, ATT_DIM), dt))
            else:
                out_specs.append(pl.BlockSpec((tm, width), lambda i: (i, 0)))
                out_shape.append(jax.ShapeDtypeStruct((n, width), dt))
    outs = pl.pallas_call(
        _in_kernel,
        grid=(nt,),
        in_specs=in_specs,
        out_specs=out_specs,
        out_shape=out_shape,
        compiler_params=_cparams(("parallel",)),
        name="in_proj",
    )(x, sc, sh, *w_groups)
    names = []
    for name, _, dtypes in IN_GROUPS:
        for dt in dtypes:
            names.append(name if dt == dtypes[0] else name + "16")
    return dict(zip(names, outs))


def _delta_kernel(qkv_ref, z_ref, sm_ref, cw_ref, alog_ref, dt_ref, ng_ref, o_ref, s_out_ref, xs, s_sc, *, tt):
    i = pl.program_id(1)
    nc = tt // DN_CHUNK
    c = DN_CHUNK

    @pl.when(i == 0)
    def _():
        xs[0:8, :] = jnp.zeros((8, 3 * DN_WIDTH), F32)
        s_sc[...] = jnp.zeros_like(s_sc)

    xs[8:8 + tt, :] = qkv_ref[...]
    cw = cw_ref[...]
    y = xs[8:8 + tt, :] * cw[3:4, :]
    for j in range(1, CONV_W):
        y = y + xs[8 - j:8 - j + tt, :] * cw[3 - j:4 - j, :]
    act = _silu(y)
    xs[0:8, :] = xs[tt:tt + 8, :]

    sm = sm_ref[...]
    beta_all = _sigmoid(sm)
    g_all = -jnp.exp(alog_ref[...]) * _softplus(sm + dt_ref[...])
    r_io = lax.broadcasted_iota(I32, (tt, tt), 0)
    c_io = lax.broadcasted_iota(I32, (tt, tt), 1)
    lblk = jnp.where((r_io // c == c_io // c) & (r_io >= c_io), 1.0, 0.0).astype(F32)
    gcum = _dot(lblk, g_all, HI)
    gcum_t = gcum.T

    ri = lax.broadcasted_iota(I32, (c, c), 0)
    ci = lax.broadcasted_iota(I32, (c, c), 1)
    incl = ri >= ci
    strict = ri > ci
    eye = jnp.where(ri == ci, 1.0, 0.0).astype(F32)
    ng = ng_ref[...]

    blocks = [(h, cc) for h in range(DN_HEADS) for cc in range(nc)]
    qs, ks, gis, egs, dmats, qks, nms, rhss = {}, {}, {}, {}, {}, {}, {}, {}
    for h in range(DN_HEADS):
        lo, hi = h * DN_DIM, (h + 1) * DN_DIM
        q_h = act[:, lo:hi]
        k_h = act[:, DN_WIDTH + lo:DN_WIDTH + hi]
        v_h = act[:, 2 * DN_WIDTH + lo:2 * DN_WIDTH + hi]
        q_h = q_h * lax.rsqrt(jnp.sum(q_h * q_h, axis=-1, keepdims=True) + 1e-6) * (DN_DIM ** -0.5)
        k_h = k_h * lax.rsqrt(jnp.sum(k_h * k_h, axis=-1, keepdims=True) + 1e-6)
        for cc in range(nc):
            r0, r1 = cc * c, (cc + 1) * c
            qc, kc, vc = q_h[r0:r1], k_h[r0:r1], v_h[r0:r1]
            gi = gcum[r0:r1, SM_A + h:SM_A + h + 1]
            gj = gcum_t[SM_A + h:SM_A + h + 1, r0:r1]
            bi = beta_all[r0:r1, SM_B + h:SM_B + h + 1]
            dmat = jnp.exp(jnp.where(incl, gi - gj, -jnp.inf))
            eg = jnp.exp(gi)
            key = (h, cc)
            qs[key], ks[key], gis[key], egs[key], dmats[key] = qc, kc, gi, eg, dmat
            qks[key] = _dot_nt(qc, kc) * dmat
            nms[key] = jnp.where(strict, -(bi * _dot_nt(kc, kc) * dmat), 0.0)
            rhss[key] = jnp.concatenate([vc * bi, kc * (bi * eg)], axis=1)

    tms = {key: eye + nms[key] for key in blocks}
    ps = {key: _split(nms[key]) for key in blocks}
    for _ in range(5):
        for key in blocks:
            p2 = _dot3(ps[key], ps[key])
            ps[key] = _split(p2)
            tms[key] = tms[key] + _dot3(_split(tms[key]), ps[key])
    sols = {key: _dot3(_split(tms[key]), _split(rhss[key])) for key in blocks}

    states = [s_sc[h] for h in range(DN_HEADS)]
    for cc in range(nc):
        r0, r1 = cc * c, (cc + 1) * c
        for h in range(DN_HEADS):
            lo, hi = h * DN_DIM, (h + 1) * DN_DIM
            key = (h, cc)
            s_h = states[h]
            gi, eg = gis[key], egs[key]
            u, w = sols[key][:, :DN_DIM], sols[key][:, DN_DIM:]
            v_new = u - _dot(w, s_h)
            o = _dot(qs[key] * eg, s_h) + _dot(qks[key], v_new)
            g_last = gi[c - 1:c, :]
            states[h] = s_h * jnp.exp(g_last) + _dot_tn(ks[key] * jnp.exp(g_last - gi), v_new)
            o = o * lax.rsqrt(jnp.mean(o * o, axis=-1, keepdims=True) + EPS) * ng
            o_ref[r0:r1, lo:hi] = o * _silu(z_ref[r0:r1, lo:hi])
    for h in range(DN_HEADS):
        s_sc[h] = states[h]

    @pl.when(i == pl.num_programs(1) - 1)
    def _():
        s_out_ref[...] = s_sc[...]


def _delta_prompt(qkv, z, small, conv_w, alog_row, dt_row, norm_g, nb, tt=256):
    n = qkv.shape[0]
    nt = n // nb // tt
    row = lambda b, i: (b * nt + i, 0)
    const = lambda b, i: (0, 0)
    return pl.pallas_call(
        functools.partial(_delta_kernel, tt=tt),
        grid=(nb, nt),
        in_specs=[
            pl.BlockSpec((tt, 3 * DN_WIDTH), row),
            pl.BlockSpec((tt, DN_WIDTH), row),
            pl.BlockSpec((tt, 128), row),
            pl.BlockSpec((CONV_W, 3 * DN_WIDTH), const),
            pl.BlockSpec((1, 128), const),
            pl.BlockSpec((1, 128), const),
            pl.BlockSpec((1, DN_DIM), const),
        ],
        out_specs=[
            pl.BlockSpec((tt, DN_WIDTH), row),
            pl.BlockSpec((None, DN_HEADS, DN_DIM, DN_DIM), lambda b, i: (b, 0, 0, 0)),
        ],
        out_shape=[
            jax.ShapeDtypeStruct((n, DN_WIDTH), F32),
            jax.ShapeDtypeStruct((nb, DN_HEADS, DN_DIM, DN_DIM), F32),
        ],
        scratch_shapes=[
            pltpu.VMEM((tt + 8, 3 * DN_WIDTH), F32),
            pltpu.VMEM((DN_HEADS, DN_DIM, DN_DIM), F32),
        ],
        compiler_params=_cparams(("arbitrary", "arbitrary")),
        name="delta_prompt",
    )(qkv, z, small, conv_w, alog_row, dt_row, norm_g)


def _sdelta_kernel(qkv_ref, buf_ref, z_ref, sm_ref, cw_ref, alog_ref, dt_ref, ng_ref, s_ref, o_ref, s_out_ref, *, rows):
    cw = cw_ref[...]
    y = qkv_ref[...] * cw[3:4, :]
    for j in range(CONV_W - 1):
        y = y + buf_ref[:, j, :] * cw[j:j + 1, :]
    act = _silu(y)
    sm = sm_ref[...]
    beta_all = _sigmoid(sm)
    g_all = -jnp.exp(alog_ref[...]) * _softplus(sm + dt_ref[...])
    ng = ng_ref[...]
    sub = lax.broadcasted_iota(I32, (8, DN_DIM), 0)
    for h in range(DN_HEADS):
        lo, hi = h * DN_DIM, (h + 1) * DN_DIM
        q_h = act[:, lo:hi]
        k_h = act[:, DN_WIDTH + lo:DN_WIDTH + hi]
        v_h = act[:, 2 * DN_WIDTH + lo:2 * DN_WIDTH + hi]
        q_h = q_h * lax.rsqrt(jnp.sum(q_h * q_h, axis=-1, keepdims=True) + 1e-6) * (DN_DIM ** -0.5)
        k_h = k_h * lax.rsqrt(jnp.sum(k_h * k_h, axis=-1, keepdims=True) + 1e-6)
        g = g_all[:, SM_A + h:SM_A + h + 1]
        b = beta_all[:, SM_B + h:SM_B + h + 1]
        eg = jnp.exp(g)
        u = v_h * b
        w = k_h * (b * eg)
        qe = q_h * eg
        qk = jnp.sum(q_h * k_h, axis=-1, keepdims=True)
        z_h = z_ref[:, lo:hi]
        for r in range(rows):
            s = s_ref[r, h]
            w8 = jnp.where(sub == 0, w[r:r + 1], 0.0)
            q8 = jnp.where(sub == 0, qe[r:r + 1], 0.0)
            k8 = jnp.where(sub == 0, k_h[r:r + 1], 0.0)
            v_new = u[r:r + 1] - _dot(w8, s, HI)[0:1]
            o = _dot(q8, s, HI)[0:1] + qk[r:r + 1] * v_new
            vn8 = jnp.where(sub == 0, v_new, 0.0)
            s_out_ref[r, h] = s * eg[r:r + 1] + _dot_tn(k8, vn8, HI)
            o = o * lax.rsqrt(jnp.mean(o * o, axis=-1, keepdims=True) + EPS) * ng
            o_ref[r:r + 1, lo:hi] = o * _silu(z_h[r:r + 1])


def _delta_sample(qkv, buf, z, small, conv_w, alog_row, dt_row, norm_g, layer, state, rows=8):
    n = qkv.shape[0]
    row = lambda i: (i, 0)
    const = lambda i: (0, 0)
    return pl.pallas_call(
        functools.partial(_sdelta_kernel, rows=rows),
        grid=(n // rows,),
        in_specs=[
            pl.BlockSpec((rows, 3 * DN_WIDTH), row),
            pl.BlockSpec((rows, CONV_W - 1, 3 * DN_WIDTH), lambda i: (i, 0, 0)),
            pl.BlockSpec((rows, DN_WIDTH), row),
            pl.BlockSpec((rows, 128), row),
            pl.BlockSpec((CONV_W, 3 * DN_WIDTH), const),
            pl.BlockSpec((1, 128), const),
            pl.BlockSpec((1, 128), const),
            pl.BlockSpec((1, DN_DIM), const),
            pl.BlockSpec((None, rows, DN_HEADS, DN_DIM, DN_DIM), lambda i: (layer, i, 0, 0, 0)),
        ],
        out_specs=[
            pl.BlockSpec((rows, DN_WIDTH), row),
            pl.BlockSpec((rows, DN_HEADS, DN_DIM, DN_DIM), lambda i: (i, 0, 0, 0)),
        ],
        out_shape=[
            jax.ShapeDtypeStruct((n, DN_WIDTH), F32),
            jax.ShapeDtypeStruct(state.shape[1:], F32),
        ],
        compiler_params=_cparams(("parallel",)),
        name="delta_sample",
    )(qkv, buf, z, small, conv_w, alog_row, dt_row, norm_g, state)


def _score_keys(s):
    s = jnp.where(s == 0.0, 0.0, s)
    k = pltpu.bitcast(s, I32)
    return jnp.where(k < 0, k ^ jnp.int32(0x7FFFFFFF), k)


SEARCH_ROUND = 4


def _alibi_slope(h):
    return 2.0 ** (-8.0 * (h + 1) / ATT_HEADS)


def _sublane_allreduce(x, op):
    for shift in (4, 2, 1):
        x = op(x, pltpu.roll(x, shift, axis=0))
    return x


def _attn_kernel(q_ref, iq_ref, sm_ref, k_ref, vt_ref, ik_ref, o_ref, keys, alibi, acc_sc, *, tq, ck):
    i = pl.program_id(1)
    nk = (i * tq + tq + ck - 1) // ck
    g8 = ck // 8
    t = i * tq + lax.broadcasted_iota(I32, (8, tq), 1)
    sub = lax.broadcasted_iota(I32, (ck, 1), 0)
    sub3 = lax.broadcasted_iota(I32, (g8, 8, 1), 0) * 8 + lax.broadcasted_iota(I32, (g8, 8, 1), 1)

    @pl.when((pl.program_id(0) == 0) & (i == 0))
    def _():
        for h in range(ATT_HEADS):
            alibi[h] = jnp.broadcast_to(sub.astype(F32) * (_alibi_slope(h) * LOG2E), (ck, tq))

    sm_t = sm_ref[...].T
    iw = [jnp.broadcast_to(sm_t[SM_IW + h:SM_IW + h + 1, :] * (IDX_HEADS ** -0.5 * IDX_DIM ** -0.5), (8, tq))
          for h in range(IDX_HEADS)]
    iq = iq_ref[...]

    def score_chunk(c, diagonal):
        off = pl.multiple_of(c * ck, ck)
        d = _dot_nt(ik_ref[pl.ds(off, ck), :], iq)
        s = jnp.maximum(d[:, 0:tq].reshape(g8, 8, tq), 0.0) * iw[0][None]
        for h in range(1, IDX_HEADS):
            s = s + jnp.maximum(d[:, h * tq:(h + 1) * tq].reshape(g8, 8, tq), 0.0) * iw[h][None]
        if diagonal:
            s = jnp.where(off + sub3 <= t[None], s, -jnp.inf)
        keys[c] = _score_keys(s)

    def score_full(c, carry):
        score_chunk(c, False)
        return carry

    lax.fori_loop(0, nk - 1, score_full, 0)
    score_chunk(nk - 1, True)

    kth = jnp.minimum(t + 1, TOPK)

    def count(pred_fn):
        def body(c, cnt):
            return cnt + jnp.sum(jnp.where(pred_fn(keys[c], c), 1, 0).astype(I32), axis=0)
        return _sublane_allreduce(lax.fori_loop(0, nk, body, jnp.zeros((8, tq), I32)), jnp.add)

    def search_cond(st):
        return (st[0] < 32) & (st[3] > 0)

    def search_round(st):
        it, thr, cnt_thr, _ = st
        for u in range(SEARCH_ROUND):
            cand = thr ^ jnp.left_shift(jnp.int32(1), 31 - u - it)
            cnt = count(lambda kc, c: kc >= cand[None])
            ok = cnt >= kth
            thr = jnp.where(ok, cand, thr)
            cnt_thr = jnp.where(ok, cnt, cnt_thr)
        return it + SEARCH_ROUND, thr, cnt_thr, jnp.max(jnp.where(cnt_thr != kth, 1, 0))

    _, thr, _, unresolved = lax.while_loop(
        search_cond, search_round,
        (jnp.int32(0), jnp.full((8, tq), -2 ** 31, I32), jnp.full((8, tq), 2 ** 30, I32), jnp.int32(1)))

    @pl.when(unresolved > 0)
    def _():
        need = kth - count(lambda kc, c: kc > thr[None])

        def pos_step(it, cut):
            cand = cut | jnp.left_shift(jnp.int32(1), 13 - it)
            cnt = count(lambda kc, c: (kc == thr[None]) & (c * ck + sub3 < cand[None]))
            return jnp.where(cnt < need, cand, cut)
        cut = lax.fori_loop(0, 14, pos_step, jnp.zeros((8, tq), I32))

        def drop(c, carry):
            kc = keys[c]
            keys[c] = jnp.where((kc == thr[None]) & (c * ck + sub3 > cut[None]), jnp.int32(-2 ** 31), kc)
            return carry
        lax.fori_loop(0, nk, drop, 0)

    acc_sc[...] = jnp.zeros_like(acc_sc)
    q = q_ref[...]
    qh = [q[:, h * ATT_DIM:(h + 1) * ATT_DIM] for h in range(ATT_HEADS)]
    heads = range(ATT_HEADS)

    def attend_chunk(c, carry):
        ms, ls = carry
        off = pl.multiple_of(c * ck, ck)
        selb = jnp.where(keys[c] >= thr[None], 0.0, NEG).reshape(ck, tq)
        kc = k_ref[pl.ds(off, ck), :]
        off_f = off.astype(F32)
        c_off = [off_f * (_alibi_slope(h) * LOG2E) for h in heads]
        zs = [(_dot_nt(kc[:, h * ATT_DIM:(h + 1) * ATT_DIM], qh[h]) * (ATT_DIM ** -0.5 * LOG2E)
               + (alibi[h] + selb)).reshape(g8, 8, tq) for h in heads]
        m_new = [jnp.maximum(ms[h], _sublane_allreduce(jnp.max(zs[h], axis=0), jnp.maximum) + c_off[h]) for h in heads]
        a = [jnp.exp2(ms[h] - m_new[h]) for h in heads]
        ps = [jnp.exp2(zs[h] - (m_new[h] - c_off[h])[None]) for h in heads]
        l_new = [a[h] * ls[h] + _sublane_allreduce(jnp.sum(ps[h], axis=0), jnp.add) for h in heads]
        for h in heads:
            pv = _dot(vt_ref[c, h * ATT_DIM:(h + 1) * ATT_DIM, :], ps[h].reshape(ck, tq).astype(BF16))
            acc_sc[h] = a[h][None] * acc_sc[h] + pv.reshape(ATT_DIM // 8, 8, tq)
        return tuple(m_new), tuple(l_new)

    init = (tuple(jnp.full((8, tq), NEG, F32) for _ in heads), tuple(jnp.zeros((8, tq), F32) for _ in heads))
    _, ls = lax.fori_loop(0, nk, attend_chunk, init)
    for h in heads:
        o_ref[:, h * ATT_DIM:(h + 1) * ATT_DIM] = (acc_sc[h] / ls[h][None]).reshape(ATT_DIM, tq).T


def _attn_prompt(q16, iq16, small, k16, vt16, ik16, nb, tq=128, ck=512):
    n = q16.shape[0]
    s = n // nb
    nq = s // tq
    row = lambda b, i: (b * nq + i, 0)
    full = lambda b, i: (b, 0)
    once = pl.Buffered(1)
    return pl.pallas_call(
        functools.partial(_attn_kernel, tq=tq, ck=ck),
        grid=(nb, nq),
        in_specs=[
            pl.BlockSpec((tq, ATT_WIDTH), row),
            pl.BlockSpec((IDX_HEADS * tq, IDX_DIM), row),
            pl.BlockSpec((tq, 128), row),
            pl.BlockSpec((s, ATT_WIDTH), full, pipeline_mode=once),
            pl.BlockSpec((None, s // ck, ATT_WIDTH, ck), lambda b, i: (b, 0, 0, 0), pipeline_mode=once),
            pl.BlockSpec((s, IDX_DIM), full, pipeline_mode=once),
        ],
        out_specs=pl.BlockSpec((tq, ATT_WIDTH), row),
        out_shape=jax.ShapeDtypeStruct((n, ATT_WIDTH), F32),
        scratch_shapes=[
            pltpu.VMEM((s // ck, ck // 8, 8, tq), I32),
            pltpu.VMEM((ATT_HEADS, ck, tq), F32),
            pltpu.VMEM((ATT_HEADS, ATT_DIM // 8, 8, tq), F32),
        ],
        compiler_params=_cparams(("arbitrary", "arbitrary")),
        name="attn_prompt",
    )(q16, iq16, small, k16, vt16, ik16)


PG = 8
SCORE_PAGES = 32


def _sscore_kernel(pt_ref, iq_ref, iw_ref, ikn_ref, *refs, pgs):
    pages, o_ref = refs[:pgs], refs[pgs]
    j = pl.program_id(1)
    iq = iq_ref[...]
    iw = iw_ref[...] * (IDX_HEADS ** -0.5)

    @pl.when(j < pl.num_programs(1) - 1)
    def _():
        for u in range(pgs):
            d = _dot(iq, pages[u][...]) * (IDX_DIM ** -0.5)
            o_ref[:, u * PAGE:(u + 1) * PAGE] = jnp.sum(jnp.maximum(d, 0.0) * iw, axis=0, keepdims=True)

    @pl.when(j == pl.num_programs(1) - 1)
    def _():
        d = _dot_nt(iq, jnp.broadcast_to(ikn_ref[...], (8, IDX_DIM)))[:, 0:1] * (IDX_DIM ** -0.5)
        s_new = jnp.sum(jnp.maximum(d, 0.0) * iw, axis=0, keepdims=True)
        lane = lax.broadcasted_iota(I32, (1, pgs * PAGE), 1)
        o_ref[...] = jnp.where(lane == 0, s_new, -jnp.inf)


def _sample_scores(layer, page_table, iq3, iw3, ik_new3, cache_idx_t):
    nbatch, n_pages = page_table.shape
    pgs = min(SCORE_PAGES, n_pages)
    nj = n_pages // pgs

    def page_map(u):
        return lambda b, j, pt: (layer, pt[b, jnp.minimum(j * pgs + u, n_pages - 1)], 0, 0)

    grid_spec = pltpu.PrefetchScalarGridSpec(
        num_scalar_prefetch=1,
        grid=(nbatch, nj + 1),
        in_specs=[
            pl.BlockSpec((None, IDX_HEADS, IDX_DIM), lambda b, j, pt: (b, 0, 0)),
            pl.BlockSpec((None, IDX_HEADS, 1), lambda b, j, pt: (b, 0, 0)),
            pl.BlockSpec((None, 1, IDX_DIM), lambda b, j, pt: (b, 0, 0)),
        ] + [pl.BlockSpec((None, None, IDX_DIM, PAGE), page_map(u)) for u in range(pgs)],
        out_specs=pl.BlockSpec((None, 1, pgs * PAGE), lambda b, j, pt: (b, 0, j)),
    )
    return pl.pallas_call(
        functools.partial(_sscore_kernel, pgs=pgs),
        grid_spec=grid_spec,
        out_shape=jax.ShapeDtypeStruct((nbatch, 1, (nj + 1) * pgs * PAGE), F32),
        compiler_params=_cparams(("arbitrary", "arbitrary")),
        name="sample_scores",
    )(page_table, iq3, iw3, ik_new3, *([cache_idx_t] * pgs))


def _sselect_kernel(s_ref, o_ref):
    keys = _score_keys(s_ref[...])
    rows, width = keys.shape
    pos = lax.broadcasted_iota(I32, (1, width), 1)

    def cnt(m):
        return jnp.sum(jnp.where(m, 1, 0).astype(I32), axis=-1, keepdims=True)

    def search_step(it, thr):
        cand = thr ^ jnp.left_shift(jnp.int32(1), 31 - it)
        return jnp.where(cnt(keys >= cand) >= TOPK, cand, thr)

    thr = lax.fori_loop(0, 32, search_step, jnp.full((rows, 1), -2 ** 31, I32))
    need = TOPK - cnt(keys > thr)
    eq = keys == thr

    def pos_step(it, cut):
        cand = cut | jnp.left_shift(jnp.int32(1), 14 - it)
        return jnp.where(cnt(eq & (pos < cand)) < need, cand, cut)

    cut = lax.fori_loop(0, 15, pos_step, jnp.zeros((rows, 1), I32))
    sel = (keys > thr) | (eq & (pos <= cut))
    o_ref[...] = jnp.where(sel, 0.0, NEG)


def _sample_select(scores):
    return pl.pallas_call(
        _sselect_kernel,
        out_shape=jax.ShapeDtypeStruct(scores.shape, F32),
        compiler_params=pltpu.CompilerParams(vmem_limit_bytes=VMEM_LIMIT),
        name="sample_select",
    )(scores)


PAGE_ROWS = PAGE * ATT_HEADS


def _sattn_kernel(pt_ref, q_ref, kn_ref, vn_ref, selb_ref, seln_ref, *refs, past):
    kp, vp = refs[:PG], refs[PG:2 * PG]
    o_ref, lg_sc, l_sc, pn_sc, acc_sc = refs[2 * PG:2 * PG + 5]
    ph = pl.program_id(1)
    j = pl.program_id(2)
    scale = ATT_DIM ** -0.5
    q8 = q_ref[...]
    row = lax.broadcasted_iota(I32, (8, 1), 0)
    lane = lax.broadcasted_iota(I32, (1, PAGE_ROWS), 1)
    slope = jnp.exp2(-8.0 * (row + 1).astype(F32) / ATT_HEADS)
    own = jnp.bitwise_and(lane, ATT_HEADS - 1) == row
    tok = jnp.right_shift(lane, 2)

    @pl.when(ph == 0)
    def _():
        for u in range(PG):
            dist = (past - ((j * PG + u) * PAGE + tok)).astype(F32)
            lg = _dot_nt(q8, kp[u][...]) * scale - slope * dist + selb_ref[:, u * PAGE_ROWS:(u + 1) * PAGE_ROWS]
            lg_sc[j * PG + u] = jnp.where(own, lg, NEG)

    @pl.when((ph == 1) & (j == 0))
    def _():
        allg = lg_sc[...]
        lgn = jnp.sum(q8 * kn_ref[...], axis=-1, keepdims=True) * scale + seln_ref[:, 0:1]
        m = jnp.maximum(jnp.max(jnp.max(allg, axis=0), axis=-1, keepdims=True), lgn)
        p = jnp.exp(allg - m)
        lg_sc[...] = p
        pn = jnp.exp(lgn - m)
        l_sc[...] = jnp.sum(jnp.sum(p, axis=0), axis=-1, keepdims=True) + pn
        pn_sc[...] = pn
        acc_sc[...] = jnp.zeros_like(acc_sc)

    @pl.when(ph == 1)
    def _():
        acc = acc_sc[...]
        for u in range(PG):
            acc = acc + _dot(lg_sc[j * PG + u], vp[u][...])
        acc_sc[...] = acc

    @pl.when((ph == 1) & (j == pl.num_programs(2) - 1))
    def _():
        o_ref[...] = (acc_sc[...] + pn_sc[...] * vn_ref[...]) / l_sc[...]


def _sample_attend(layer, page_table, q8, kn8, vn8, selb4, selb, cache_k4, cache_v4):
    nbatch, n_pages = page_table.shape
    nj = n_pages // PG
    past = n_pages * PAGE

    def k_map(u):
        return lambda b, ph, j, pt: (layer, pt[b, jnp.where(ph == 0, j, nj - 1) * PG + u], 0, 0)

    def v_map(u):
        return lambda b, ph, j, pt: (layer, pt[b, jnp.where(ph == 0, 0, j) * PG + u], 0, 0)

    vec = pl.BlockSpec((None, 8, ATT_DIM), lambda b, ph, j, pt: (b, 0, 0))
    grid_spec = pltpu.PrefetchScalarGridSpec(
        num_scalar_prefetch=1,
        grid=(nbatch, 2, nj),
        in_specs=[
            vec, vec, vec,
            pl.BlockSpec((None, 1, PG * PAGE_ROWS), lambda b, ph, j, pt: (b, 0, jnp.where(ph == 0, j, nj - 1))),
            pl.BlockSpec((None, 1, 128), lambda b, ph, j, pt: (b, 0, past // 128)),
        ] + [pl.BlockSpec((None, None, PAGE_ROWS, ATT_DIM), k_map(u)) for u in range(PG)]
          + [pl.BlockSpec((None, None, PAGE_ROWS, ATT_DIM), v_map(u)) for u in range(PG)],
        out_specs=vec,
        scratch_shapes=[
            pltpu.VMEM((n_pages, 8, PAGE_ROWS), F32),
            pltpu.VMEM((8, 1), F32),
            pltpu.VMEM((8, 1), F32),
            pltpu.VMEM((8, ATT_DIM), F32),
        ],
    )
    return pl.pallas_call(
        functools.partial(_sattn_kernel, past=past),
        grid_spec=grid_spec,
        out_shape=jax.ShapeDtypeStruct((nbatch, 8, ATT_DIM), F32),
        compiler_params=_cparams(("arbitrary", "arbitrary", "arbitrary")),
        name="sample_attend",
    )(page_table, q8, kn8, vn8, selb4, selb, *([cache_k4] * PG), *([cache_v4] * PG))


def _mix_kernel(oa_ref, ob_ref, gates_ref, x_ref, gm_ref, wa_ref, wb_ref, wo_ref, g_ref, b_ref, o_ref):
    br_a = _dot(oa_ref[...].astype(BF16), wa_ref[...])
    br_b = _dot(ob_ref[...].astype(BF16), wb_ref[...])
    merged = _sigmoid(gates_ref[:, :D_MODEL]) * br_a + _sigmoid(gates_ref[:, D_MODEL:]) * br_b
    y = _dot(merged.astype(BF16), wo_ref[...])
    r = ALPHA * x_ref[...] + gm_ref[...] * y
    o_ref[...] = _layernorm(r, g_ref[...], b_ref[...])


def _mix(oa, ob, gates, x, gm, wa, wb, wo, ln_g, ln_b, tm):
    n = x.shape[0]
    nb, r, _ = gm.shape
    nt = n // tm
    tiles_per_b = nt // nb
    row = lambda i: (i, 0)
    const = lambda i: (0, 0)
    return pl.pallas_call(
        _mix_kernel,
        grid=(nt,),
        in_specs=[
            pl.BlockSpec((tm, DN_WIDTH), row),
            pl.BlockSpec((tm, ATT_WIDTH), row),
            pl.BlockSpec((tm, 2 * D_MODEL), row),
            pl.BlockSpec((tm, D_MODEL), row),
            pl.BlockSpec((None, r, D_MODEL), lambda i: (i // tiles_per_b, 0, 0)),
            pl.BlockSpec((DN_WIDTH, D_MODEL), const),
            pl.BlockSpec((ATT_WIDTH, D_MODEL), const),
            pl.BlockSpec((D_MODEL, D_MODEL), const),
            pl.BlockSpec((1, D_MODEL), const),
            pl.BlockSpec((1, D_MODEL), const),
        ],
        out_specs=pl.BlockSpec((tm, D_MODEL), row),
        out_shape=jax.ShapeDtypeStruct((n, D_MODEL), F32),
        compiler_params=_cparams(("parallel",)),
        name="mix_out",
    )(oa, ob, gates, x, gm, wa, wb, wo, ln_g, ln_b)


def _route_kernel(x_ref, sc_ref, sh_ref, wr_ref, o_ref, e_ref, g_ref):
    h = x_ref[...] * (1.0 + sc_ref[...]) + sh_ref[...]
    o_ref[...] = h.astype(o_ref.dtype)
    logits = _dot(h, wr_ref[...], HI)
    lane = lax.broadcasted_iota(I32, logits.shape, 1)
    logits = jnp.where(lane < N_EXPERTS, logits, -jnp.inf)
    e = jnp.exp(logits - jnp.max(logits, axis=-1, keepdims=True))
    p = e / jnp.sum(e, axis=-1, keepdims=True)
    p1 = jnp.max(p, axis=-1, keepdims=True)
    e1 = jnp.min(jnp.where(p == p1, lane, 128), axis=-1, keepdims=True)
    rest = jnp.where((lane == e1) | (lane >= N_EXPERTS), -1.0, p)
    p2 = jnp.max(rest, axis=-1, keepdims=True)
    e2 = jnp.min(jnp.where(rest == p2, lane, 128), axis=-1, keepdims=True)
    tot = p1 + p2
    e_ref[...] = jnp.concatenate([e1, e2], axis=1)
    g_ref[...] = jnp.concatenate([p1 / tot, p2 / tot], axis=1)


def _ffn_route(x, sc, sh, tm, w_router):
    n = x.shape[0]
    nb, r, _ = sc.shape
    nt = n // tm
    tiles_per_b = nt // nb
    row = lambda i: (i, 0)
    mod_spec = pl.BlockSpec((None, r, D_MODEL), lambda i: (i // tiles_per_b, 0, 0))
    in_specs = [pl.BlockSpec((tm, D_MODEL), row), mod_spec, mod_spec]
    wr = jnp.pad(w_router, ((0, 0), (0, 128 - N_EXPERTS)))
    return pl.pallas_call(
        _route_kernel, grid=(nt,),
        in_specs=in_specs + [pl.BlockSpec((D_MODEL, 128), lambda i: (0, 0))],
        out_specs=[pl.BlockSpec((tm, D_MODEL), row), pl.BlockSpec((tm, 2), row), pl.BlockSpec((tm, 2), row)],
        out_shape=[jax.ShapeDtypeStruct((n, D_MODEL), F32), jax.ShapeDtypeStruct((n, 2), I32),
                   jax.ShapeDtypeStruct((n, 2), F32)],
        compiler_params=_cparams(("parallel",)), name="ffn_route",
    )(x, sc, sh, wr)


def _gffn_kernel(te_ref, nu_ref, x_ref, *refs, modulate, substitute):
    if modulate:
        sc_ref, sh_ref = refs[:2]
        refs = refs[2:]
    if substitute:
        src_ref, late_ref = refs[:2]
        refs = refs[2:]
    wg_ref, wu_ref, wd_ref, o_ref, acc, x16 = refs
    t = pl.program_id(0)
    f = pl.program_id(1)

    @pl.when((t < nu_ref[0]) & (f == 0))
    def _():
        x = x_ref[...]
        if modulate:
            x = x * (1.0 + sc_ref[...]) + sh_ref[...]
        x = x.astype(BF16)
        if substitute:
            src = src_ref[...]
            late = late_ref[...]
            pick = jnp.where(src == lax.broadcasted_iota(I32, (src.shape[0], late.shape[0]), 1), 1.0, 0.0)
            x = jnp.where(src >= 0, _dot(pick.astype(BF16), late).astype(BF16), x)
        x16[...] = x

    @pl.when(t < nu_ref[0])
    def _():
        x = x16[...]
        a = _dot(x, wg_ref[...].astype(BF16))
        b = _dot(x, wu_ref[...].astype(BF16))
        y = _dot((_silu(a) * b).astype(BF16), wd_ref[...].astype(BF16))

        @pl.when(f == 0)
        def _():
            acc[...] = y

        @pl.when(f > 0)
        def _():
            acc[...] += y

    @pl.when(f == pl.num_programs(1) - 1)
    def _():
        o_ref[...] = jnp.where(t < nu_ref[0], acc[...], 0.0)


def _grouped_ffn(x, tile_expert, n_used, layer, w_gate, w_up, w_down, tm, mod=None, late=None, tf=256):
    n = x.shape[0]
    nt = n // tm
    nf = D_FF // tf
    mod_specs, mod_args = [], []
    if mod is not None:
        nb, r, _ = mod[0].shape
        tiles_per_b = nt // nb
        mod_specs = [pl.BlockSpec((None, r, D_MODEL), lambda t, f, te, nu: (t // tiles_per_b, 0, 0))] * 2
        mod_args = list(mod)
    if late is not None:
        mod_specs = mod_specs + [pl.BlockSpec((tm, 1), lambda t, f, te, nu: (t, 0)),
                                 pl.BlockSpec(late[1].shape, lambda t, f, te, nu: (0, 0))]
        mod_args = mod_args + list(late)
    grid_spec = pltpu.PrefetchScalarGridSpec(
        num_scalar_prefetch=2,
        grid=(nt, nf),
        in_specs=[pl.BlockSpec((tm, D_MODEL), lambda t, f, te, nu: (t, 0))] + mod_specs + [
            pl.BlockSpec((None, None, D_MODEL, tf), lambda t, f, te, nu: (layer, te[t], 0, f)),
            pl.BlockSpec((None, None, D_MODEL, tf), lambda t, f, te, nu: (layer, te[t], 0, f)),
            pl.BlockSpec((None, None, tf, D_MODEL), lambda t, f, te, nu: (layer, te[t], f, 0)),
        ],
        out_specs=pl.BlockSpec((tm, D_MODEL), lambda t, f, te, nu: (t, 0)),
        scratch_shapes=[pltpu.VMEM((tm, D_MODEL), F32), pltpu.VMEM((tm, D_MODEL), BF16)],
    )
    return pl.pallas_call(
        functools.partial(_gffn_kernel, modulate=mod is not None, substitute=late is not None),
        grid_spec=grid_spec,
        out_shape=jax.ShapeDtypeStruct((n, D_MODEL), F32),
        compiler_params=_cparams(("arbitrary", "arbitrary")),
        name="grouped_ffn",
    )(tile_expert, n_used, x, *mod_args, w_gate, w_up, w_down)


def _res_ln_kernel(x_ref, y_ref, gf_ref, g_ref, b_ref, o_ref):
    r = ALPHA * x_ref[...] + gf_ref[...] * y_ref[...]
    o_ref[...] = _layernorm(r, g_ref[...], b_ref[...])


def _res_ln2_kernel(x_ref, y0_ref, y1_ref, gt_ref, gf_ref, g_ref, b_ref, o_ref):
    gt = gt_ref[...]
    y = y0_ref[...] * gt[:, 0:1] + y1_ref[...] * gt[:, 1:2]
    r = ALPHA * x_ref[...] + gf_ref[...] * y
    o_ref[...] = _layernorm(r, g_ref[...], b_ref[...])


def _res_ln(x, ys, gate2, gf, ln_g, ln_b, tm):
    n = x.shape[0]
    nb, r, _ = gf.shape
    nt = n // tm
    tiles_per_b = nt // nb
    row = lambda i: (i, 0)
    const = lambda i: (0, 0)
    big = pl.BlockSpec((tm, D_MODEL), row)
    mod_spec = pl.BlockSpec((None, r, D_MODEL), lambda i: (i // tiles_per_b, 0, 0))
    vec = pl.BlockSpec((1, D_MODEL), const)
    if gate2 is None:
        kern, ins, specs = _res_ln_kernel, (x, ys[0], gf, ln_g, ln_b), [big, big, mod_spec, vec, vec]
    else:
        kern = _res_ln2_kernel
        ins = (x, ys[0], ys[1], gate2, gf, ln_g, ln_b)
        specs = [big, big, big, pl.BlockSpec((tm, 2), row), mod_spec, vec, vec]
    return pl.pallas_call(
        kern, grid=(nt,), in_specs=specs, out_specs=big,
        out_shape=jax.ShapeDtypeStruct((n, D_MODEL), F32),
        compiler_params=_cparams(("parallel",)), name="res_ln",
    )(*ins)


def _dense_ffn(x, sc, sh, gf, layer, w_gate, w_up, w_down, ln_g, ln_b, tm_mod, tm_ffn):
    n = x.shape[0]
    nt = n // tm_ffn
    y = _grouped_ffn(x, jnp.zeros((nt,), I32), jnp.full((1,), nt, I32), layer,
                     w_gate[:, None], w_up[:, None], w_down[:, None], tm_ffn, mod=(sc, sh))
    return _res_ln(x, (y,), None, gf, ln_g, ln_b, tm_mod)


def _expert_ranks(top_e):
    flat_e = top_e.reshape(-1)
    onehot = flat_e[:, None] == jnp.arange(N_EXPERTS, dtype=I32)[None, :]
    seen = jnp.cumsum(onehot.astype(I32), axis=0)
    pick = lambda table: jnp.sum(jnp.where(onehot, table, 0), axis=1)
    return flat_e, seen[-1], pick(seen) - 1, pick


def _slot_sources(flat_e, counts, first, tile_expert, pad_start, tm_e):
    n_tiles = tile_expert.shape[0]
    order = jnp.argsort(flat_e).astype(I32)
    start = jnp.cumsum(counts) - counts
    within = (jnp.arange(n_tiles * tm_e, dtype=I32).reshape(n_tiles, tm_e)
              - (pad_start + first)[tile_expert][:, None])
    live = (within >= 0) & (within < counts[tile_expert][:, None])
    src = jnp.clip(start[tile_expert][:, None] + within, 0, flat_e.shape[0] - 1)
    return jnp.where(live, order[src.reshape(-1)].reshape(n_tiles, tm_e), -1).reshape(-1)


def _moe_ffn(main, late, w_router, layer, w_gate, w_up, w_down, ln_g, ln_b, tm_e):
    x_m, x_l = main[0], late[0]
    n_m, n_l = x_m.shape[0], x_l.shape[0]
    h_m, top_m, gate_m = _ffn_route(*main[:3], main[4], w_router)
    flat_m, counts_m, rank_m, pick_m = _expert_ranks(top_m)
    padded = (counts_m + n_l + tm_e - 1) // tm_e * tm_e
    pad_end = jnp.cumsum(padded)
    pad_start = pad_end - padded
    n_tiles = -(-(2 * n_m + N_EXPERTS * n_l) // tm_e) + N_EXPERTS
    tile_expert = jnp.minimum(
        jnp.searchsorted(pad_end, jnp.arange(n_tiles, dtype=I32) * tm_e, side="right"), N_EXPERTS - 1).astype(I32)
    n_used = (pad_end[-1] // tm_e).astype(I32).reshape(1)
    slot_m = (pick_m(pad_start[None, :]) + rank_m).reshape(n_m, 2)
    zero = jnp.zeros_like(counts_m)
    rows_m = jnp.maximum(_slot_sources(flat_m, counts_m, zero, tile_expert, pad_start, tm_e), 0) // 2
    xb = h_m[rows_m]

    h_l, top_l, gate_l = _ffn_route(*late[:3], late[4], w_router)
    flat_l, counts_l, rank_l, pick_l = _expert_ranks(top_l)
    slot_l = (pick_l((pad_start + counts_m)[None, :]) + rank_l).reshape(n_l, 2)
    src_l = _slot_sources(flat_l, counts_l, counts_m, tile_expert, pad_start, tm_e)
    src_l = jnp.where(src_l >= 0, src_l // 2, -1).reshape(-1, 1)
    yb = _grouped_ffn(xb, tile_expert, n_used, layer, w_gate, w_up, w_down, tm_e, late=(src_l, h_l.astype(BF16)))
    out_m = _res_ln(x_m, (yb[slot_m[:, 0]], yb[slot_m[:, 1]]), gate_m, main[3], ln_g, ln_b, main[4])
    out_l = _res_ln(x_l, (yb[slot_l[:, 0]], yb[slot_l[:, 1]]), gate_l, late[3], ln_g, ln_b, late[4])
    return out_m, out_l


def kernel(x_prompt, x_sample, cache_k, cache_v, cache_idx_k, state_delta, state_conv, page_table, c_prompt, c_sample, w_ada, b_ada, w_in, conv_w, a_log, dt_bias, dn_norm_g, w_branch, w_out, ln1_g, ln1_b, ln2_g, ln2_b, w_ffn_gate, w_ffn_up, w_ffn_down, w_router, w_exp_gate, w_exp_up, w_exp_down):
    bp, seq, _ = x_prompt.shape
    bs = x_sample.shape[0]
    n_p = bp * seq
    n_phys = cache_k.shape[1]
    past = page_table.shape[1] * PAGE

    c_all = jnp.concatenate([c_prompt, c_sample], axis=0)
    c_rows = -(-c_all.shape[0] // 8) * 8
    c_all = jnp.pad(c_all, ((0, c_rows - c_all.shape[0]), (0, 0)))
    mods = _ada_all(c_all, w_ada, b_ada)

    cache_k4 = cache_k.reshape(DEPTH, n_phys, PAGE_ROWS, ATT_DIM)
    cache_v4 = cache_v.reshape(DEPTH, n_phys, PAGE_ROWS, ATT_DIM)
    cache_idx_t = jnp.swapaxes(cache_idx_k, 2, 3)
    ck = 512

    xp = x_prompt.reshape(n_p, D_MODEL)
    xs = x_sample.reshape(bs, D_MODEL)
    lane = jnp.arange(128)
    outs_p, outs_s = [], []
    for l in range(DEPTH):
        j = l // 2
        mod_p = [mods[l, :bp, u * D_MODEL:(u + 1) * D_MODEL].reshape(bp, 1, D_MODEL) for u in range(6)]
        mod_s = [mods[l, bp:bp + bs, u * D_MODEL:(u + 1) * D_MODEL].reshape(1, bs, D_MODEL) for u in range(6)]
        w_groups = _split_w_in(w_in[l])
        head_lane = jnp.clip(lane - SM_A, 0, DN_HEADS - 1)
        in_a = (lane >= SM_A) & (lane < SM_A + DN_HEADS)
        alog_row = jnp.where(in_a, a_log[l][head_lane], 0.0).reshape(1, 128)
        dt_row = jnp.where(in_a, dt_bias[l][head_lane], 0.0).reshape(1, 128)
        norm_g = dn_norm_g[l].reshape(1, DN_DIM)
        wa16 = w_branch[l, 0].astype(BF16)
        wb16 = w_branch[l, 1].astype(BF16)
        wo16 = w_out[l].astype(BF16)
        ln1 = (ln1_g[l].reshape(1, -1), ln1_b[l].reshape(1, -1))
        ln2 = (ln2_g[l].reshape(1, -1), ln2_b[l].reshape(1, -1))

        pr = _in_proj(xp, mod_p[1], mod_p[0], w_groups, tm=256)
        oa, s_new_p = _delta_prompt(pr["qkv_a"], pr["z"], pr["small"], conv_w[l], alog_row, dt_row, norm_g, bp)
        ik16 = pr["small"][:, SM_IK:SM_IK + IDX_DIM].astype(BF16)
        vt16 = jnp.swapaxes(pr["v_b16"].reshape(bp, seq // ck, ck, ATT_WIDTH), 2, 3)
        tq = 256
        iq_hm = jnp.swapaxes(pr["iq"].reshape(n_p // tq, tq, IDX_HEADS, IDX_DIM), 1, 2).reshape(-1, IDX_DIM)
        ob = _attn_prompt(pr["q_b"], iq_hm, pr["small"], pr["k_b16"], vt16, ik16, bp, tq=tq, ck=ck)
        x1 = _mix(oa, ob, pr["gates"], xp, mod_p[2], wa16, wb16, wo16, *ln1, tm=512)
        conv_p =pr["qkv_a"].reshape(bp, seq, -1)[:, seq - (CONV_W - 1):, :]
        outs_p.append((conv_p, s_new_p, pr["k_b"].reshape(bp, seq, ATT_HEADS, ATT_DIM),
                       pr["v_b"].reshape(bp, seq, ATT_HEADS, ATT_DIM),
                       pr["small"][:, SM_IK:SM_IK + IDX_DIM].reshape(bp, seq, IDX_DIM)))

        sr = _in_proj(xs, mod_s[1], mod_s[0], w_groups, tm=bs)
        oa_s, s_new_s = _delta_sample(sr["qkv_a"], state_conv[l], sr["z"], sr["small"], conv_w[l], alog_row, dt_row,
                                      norm_g, l, state_delta)
        ik_s = sr["small"][:, SM_IK:SM_IK + IDX_DIM]
        iq3 = sr["iq"].astype(F32).reshape(bs, IDX_HEADS, IDX_DIM)
        iw3 = sr["small"][:, SM_IW:SM_IW + IDX_HEADS].reshape(bs, IDX_HEADS, 1)
        scores = _sample_scores(l, page_table, iq3, iw3, ik_s.reshape(bs, 1, IDX_DIM), cache_idx_t)
        selb = _sample_select(scores.reshape(bs, -1)).reshape(scores.shape)
        head_rows = lambda a: jnp.pad(a.astype(F32).reshape(bs, ATT_HEADS, ATT_DIM), ((0, 0), (0, 8 - ATT_HEADS), (0, 0)))
        selb4 = jnp.repeat(selb, ATT_HEADS, axis=-1)
        ob_s = _sample_attend(l, page_table, head_rows(sr["q_b"]), head_rows(sr["k_b"]), head_rows(sr["v_b"]),
                              selb4, selb, cache_k4, cache_v4)[:, :ATT_HEADS, :].reshape(bs, ATT_WIDTH)
        x1s = _mix(oa_s, ob_s, sr["gates"], xs, mod_s[2], wa16, wb16, wo16, *ln1, tm=bs)

        if l % 2 == 0:
            xp = _dense_ffn(x1, mod_p[4], mod_p[3], mod_p[5], j, w_ffn_gate, w_ffn_up, w_ffn_down, *ln2,
                            tm_mod=512, tm_ffn=1024)
            xs = _dense_ffn(x1s, mod_s[4], mod_s[3], mod_s[5], j, w_ffn_gate, w_ffn_up, w_ffn_down, *ln2,
                            tm_mod=bs, tm_ffn=bs)
        else:
            xp, xs = _moe_ffn((x1, mod_p[4], mod_p[3], mod_p[5], 512), (x1s, mod_s[4], mod_s[3], mod_s[5], bs),
                              w_router[j], j, w_exp_gate, w_exp_up, w_exp_down, *ln2, tm_e=1024)
        conv_s =jnp.concatenate([state_conv[l][:, 1:, :], sr["qkv_a"][:, None, :]], axis=1)
        outs_s.append((conv_s, s_new_s, sr["k_b"].reshape(bs, 1, ATT_HEADS, ATT_DIM),
                       sr["v_b"].reshape(bs, 1, ATT_HEADS, ATT_DIM), ik_s.reshape(bs, 1, IDX_DIM)))

    stack = lambda rows, u: jnp.stack([r[u] for r in rows])
    return (xp.reshape(bp, seq, D_MODEL), xs.reshape(bs, 1, D_MODEL),
            stack(outs_p, 2), stack(outs_p, 3), stack(outs_p, 4), stack(outs_p, 1), stack(outs_p, 0),
            stack(outs_s, 2), stack(outs_s, 3), stack(outs_s, 4), stack(outs_s, 1), stack(outs_s, 0))
```

```python
import functools

import jax
import jax.numpy as jnp
from jax import lax
from jax.experimental import pallas as pl
from jax.experimental.pallas import tpu as pltpu

F32 = jnp.float32
BF16 = jnp.bfloat16
I32 = jnp.int32
HI = lax.Precision.HIGHEST

D_MODEL = 1024
DEPTH = 4
PAGE = 128
DN_HEADS = 4
DN_DIM = 128
DN_WIDTH = DN_HEADS * DN_DIM
CONV_W = 4
DN_CHUNK = 64
ATT_HEADS = 4
ATT_DIM = 128
ATT_WIDTH = ATT_HEADS * ATT_DIM
IDX_HEADS = 8
IDX_DIM = 64
TOPK = 256
D_FF = 2816
N_EXPERTS = 8
ALPHA = (2.0 * DEPTH) ** 0.25
EPS = 1e-5
NEG = -1e30
LOG2E = 1.4426950408889634

SM_IK = 0
SM_B = 64
SM_A = 68
SM_IW = 72

VMEM_LIMIT = 56 * 1024 * 1024


def _cparams(sem):
    return pltpu.CompilerParams(dimension_semantics=sem, vmem_limit_bytes=VMEM_LIMIT)


def _sigmoid(x):
    return jax.nn.sigmoid(x)


def _silu(x):
    return x * jax.nn.sigmoid(x)


def _softplus(x):
    return jnp.maximum(x, 0.0) + jnp.log1p(jnp.exp(-jnp.abs(x)))


def _dot(a, b, precision=None):
    return jnp.dot(a, b, precision=precision, preferred_element_type=F32)


def _dot_nt(a, b, precision=None):
    return lax.dot_general(a, b, (((1,), (1,)), ((), ())), precision=precision, preferred_element_type=F32)


def _dot_tn(a, b, precision=None):
    return lax.dot_general(a, b, (((0,), (0,)), ((), ())), precision=precision, preferred_element_type=F32)


def _split(a):
    hi = a.astype(BF16)
    return hi, (a - hi.astype(F32)).astype(BF16)


def _dot3(a, b):
    return _dot(a[0], b[0]) + (_dot(a[0], b[1]) + _dot(a[1], b[0]))


def _layernorm(r, g, b):
    mu = jnp.mean(r, axis=-1, keepdims=True)
    rc = r - mu
    var = jnp.mean(rc * rc, axis=-1, keepdims=True)
    return rc * lax.rsqrt(var + EPS) * g + b


def _ada_kernel(c_ref, w_ref, b_ref, o_ref):
    o_ref[...] = _dot(_silu(c_ref[...]), w_ref[...], HI) + b_ref[...]


def _ada_all(c, w_ada, b_ada):
    rows = c.shape[0]
    nj = w_ada.shape[2] // D_MODEL
    return pl.pallas_call(
        _ada_kernel,
        grid=(DEPTH, nj),
        in_specs=[
            pl.BlockSpec((rows, D_MODEL), lambda l, j: (0, 0)),
            pl.BlockSpec((None, D_MODEL, D_MODEL), lambda l, j: (l, 0, j)),
            pl.BlockSpec((None, 1, D_MODEL), lambda l, j: (l, 0, j)),
        ],
        out_specs=pl.BlockSpec((None, rows, D_MODEL), lambda l, j: (l, 0, j)),
        out_shape=jax.ShapeDtypeStruct((DEPTH, rows, nj * D_MODEL), F32),
        compiler_params=_cparams(("arbitrary", "arbitrary")),
        name="ada_mod",
    )(c, w_ada, b_ada.reshape(DEPTH, 1, -1))


IN_GROUPS = (
    ("qkv_a", 3 * DN_WIDTH, (F32,)),
    ("z", DN_WIDTH, (F32,)),
    ("q_b", ATT_WIDTH, (BF16,)),
    ("k_b", ATT_WIDTH, (F32, BF16)),
    ("v_b", ATT_WIDTH, (F32, BF16)),
    ("iq", IDX_HEADS * IDX_DIM, (BF16,)),
    ("gates", 2 * D_MODEL, (F32,)),
    ("small", 128, (F32,)),
)


HEAD_MAJOR_ROWS = (("k_b", F32), ("v_b", F32))


def _split_w_in(w):
    o = 0
    parts = {}
    for name, width in (("qkv_a", 3 * DN_WIDTH), ("z", DN_WIDTH), ("b", DN_HEADS), ("a", DN_HEADS),
                        ("q_b", ATT_WIDTH), ("k_b", ATT_WIDTH), ("v_b", ATT_WIDTH),
                        ("iq", IDX_HEADS * IDX_DIM), ("ik", IDX_DIM), ("iw", IDX_HEADS), ("gates", 2 * D_MODEL)):
        parts[name] = w[:, o:o + width]
        o += width
    small = jnp.concatenate(
        [parts["ik"], parts["b"], parts["a"], parts["iw"],
         jnp.zeros((w.shape[0], 128 - IDX_DIM - 2 * DN_HEADS - IDX_HEADS), w.dtype)], axis=1)
    parts["small"] = small
    return [parts[name].astype(BF16) for name, _, _ in IN_GROUPS]


def _in_kernel(x_ref, sc_ref, sh_ref, *refs):
    n = len(IN_GROUPS)
    w_refs, o_refs = refs[:n], refs[n:]
    h = (x_ref[...] * (1.0 + sc_ref[...]) + sh_ref[...]).astype(BF16)
    k = 0
    for (name, width, dtypes), w_ref in zip(IN_GROUPS, w_refs):
        y = _dot(h, w_ref[...])
        for dt in dtypes:
            if (name, dt) in HEAD_MAJOR_ROWS:
                rows = y.shape[0]
                for hd in range(ATT_HEADS):
                    o_refs[k][pl.ds(hd, rows, stride=ATT_HEADS), :] = y[:, hd * ATT_DIM:(hd + 1) * ATT_DIM]
            else:
                o_refs[k][...] = y.astype(dt)
            k += 1


def _in_proj(x, sc, sh, w_groups, tm):
    n = x.shape[0]
    nb, r, _ = sc.shape
    nt = n // tm
    tiles_per_b = nt // nb
    mod_spec = pl.BlockSpec((None, r, D_MODEL), lambda i: (i // tiles_per_b, 0, 0))
    in_specs = [pl.BlockSpec((tm, D_MODEL), lambda i: (i, 0)), mod_spec, mod_spec]
    for (name, width, _), w in zip(IN_GROUPS, w_groups):
        in_specs.append(pl.BlockSpec((D_MODEL, width), lambda i: (0, 0), pipeline_mode=pl.Buffered(1)))
    out_specs, out_shape = [], []
    for name, width, dtypes in IN_GROUPS:
        for dt in dtypes:
            if (name, dt) in HEAD_MAJOR_ROWS:
                out_specs.append(pl.BlockSpec((tm * ATT_HEADS, ATT_DIM), lambda i: (i, 0)))
                out_shape.append(jax.ShapeDtypeStruct((n * ATT_HEADS, ATT_DIM), dt))
            else:
                out_specs.append(pl.BlockSpec((tm, width), lambda i: (i, 0)))
                out_shape.append(jax.ShapeDtypeStruct((n, width), dt))
    outs = pl.pallas_call(
        _in_kernel,
        grid=(nt,),
        in_specs=in_specs,
        out_specs=out_specs,
        out_shape=out_shape,
        compiler_params=_cparams(("parallel",)),
        name="in_proj",
    )(x, sc, sh, *w_groups)
    names = []
    for name, _, dtypes in IN_GROUPS:
        for dt in dtypes:
            names.append(name if dt == dtypes[0] else name + "16")
    return dict(zip(names, outs))


def _delta_kernel(qkv_ref, z_ref, sm_ref, cw_ref, alog_ref, dt_ref, ng_ref, o_ref, s_out_ref, xs, s_sc, *, tt):
    i = pl.program_id(1)
    nc = tt // DN_CHUNK
    c = DN_CHUNK

    @pl.when(i == 0)
    def _():
        xs[0:8, :] = jnp.zeros((8, 3 * DN_WIDTH), F32)
        s_sc[...] = jnp.zeros_like(s_sc)

    xs[8:8 + tt, :] = qkv_ref[...]
    cw = cw_ref[...]
    y = xs[8:8 + tt, :] * cw[3:4, :]
    for j in range(1, CONV_W):
        y = y + xs[8 - j:8 - j + tt, :] * cw[3 - j:4 - j, :]
    act = _silu(y)
    xs[0:8, :] = xs[tt:tt + 8, :]

    sm = sm_ref[...]
    beta_all = _sigmoid(sm)
    g_all = -jnp.exp(alog_ref[...]) * _softplus(sm + dt_ref[...])
    r_io = lax.broadcasted_iota(I32, (tt, tt), 0)
    c_io = lax.broadcasted_iota(I32, (tt, tt), 1)
    lblk = jnp.where((r_io // c == c_io // c) & (r_io >= c_io), 1.0, 0.0).astype(F32)
    gcum = _dot(lblk, g_all, HI)
    gcum_t = gcum.T

    ri = lax.broadcasted_iota(I32, (c, c), 0)
    ci = lax.broadcasted_iota(I32, (c, c), 1)
    incl = ri >= ci
    strict = ri > ci
    eye = jnp.where(ri == ci, 1.0, 0.0).astype(F32)
    ng = ng_ref[...]

    blocks = [(h, cc) for h in range(DN_HEADS) for cc in range(nc)]
    qs, ks, gis, egs, dmats, qks, nms, rhss = {}, {}, {}, {}, {}, {}, {}, {}
    for h in range(DN_HEADS):
        lo, hi = h * DN_DIM, (h + 1) * DN_DIM
        q_h = act[:, lo:hi]
        k_h = act[:, DN_WIDTH + lo:DN_WIDTH + hi]
        v_h = act[:, 2 * DN_WIDTH + lo:2 * DN_WIDTH + hi]
        q_h = q_h * lax.rsqrt(jnp.sum(q_h * q_h, axis=-1, keepdims=True) + 1e-6) * (DN_DIM ** -0.5)
        k_h = k_h * lax.rsqrt(jnp.sum(k_h * k_h, axis=-1, keepdims=True) + 1e-6)
        for cc in range(nc):
            r0, r1 = cc * c, (cc + 1) * c
            qc, kc, vc = q_h[r0:r1], k_h[r0:r1], v_h[r0:r1]
            gi = gcum[r0:r1, SM_A + h:SM_A + h + 1]
            gj = gcum_t[SM_A + h:SM_A + h + 1, r0:r1]
            bi = beta_all[r0:r1, SM_B + h:SM_B + h + 1]
            dmat = jnp.exp(jnp.where(incl, gi - gj, -jnp.inf))
            eg = jnp.exp(gi)
            key = (h, cc)
            qs[key], ks[key], gis[key], egs[key], dmats[key] = qc, kc, gi, eg, dmat
            qks[key] = _dot_nt(qc, kc) * dmat
            nms[key] = jnp.where(strict, -(bi * _dot_nt(kc, kc) * dmat), 0.0)
            rhss[key] = jnp.concatenate([vc * bi, kc * (bi * eg)], axis=1)

    tms = {key: eye + nms[key] for key in blocks}
    ps = {key: _split(nms[key]) for key in blocks}
    for _ in range(5):
        for key in blocks:
            p2 = _dot3(ps[key], ps[key])
            ps[key] = _split(p2)
            tms[key] = tms[key] + _dot3(_split(tms[key]), ps[key])
    sols = {key: _dot3(_split(tms[key]), _split(rhss[key])) for key in blocks}

    states = [s_sc[h] for h in range(DN_HEADS)]
    for cc in range(nc):
        r0, r1 = cc * c, (cc + 1) * c
        for h in range(DN_HEADS):
            lo, hi = h * DN_DIM, (h + 1) * DN_DIM
            key = (h, cc)
            s_h = states[h]
            gi, eg = gis[key], egs[key]
            u, w = sols[key][:, :DN_DIM], sols[key][:, DN_DIM:]
            v_new = u - _dot(w, s_h)
            o = _dot(qs[key] * eg, s_h) + _dot(qks[key], v_new)
            g_last = gi[c - 1:c, :]
            states[h] = s_h * jnp.exp(g_last) + _dot_tn(ks[key] * jnp.exp(g_last - gi), v_new)
            o = o * lax.rsqrt(jnp.mean(o * o, axis=-1, keepdims=True) + EPS) * ng
            o_ref[r0:r1, lo:hi] = o * _silu(z_ref[r0:r1, lo:hi])
    for h in range(DN_HEADS):
        s_sc[h] = states[h]

    @pl.when(i == pl.num_programs(1) - 1)
    def _():
        s_out_ref[...] = s_sc[...]


def _delta_prompt(qkv, z, small, conv_w, alog_row, dt_row, norm_g, nb, tt=256):
    n = qkv.shape[0]
    nt = n // nb // tt
    row = lambda b, i: (b * nt + i, 0)
    const = lambda b, i: (0, 0)
    return pl.pallas_call(
        functools.partial(_delta_kernel, tt=tt),
        grid=(nb, nt),
        in_specs=[
            pl.BlockSpec((tt, 3 * DN_WIDTH), row),
            pl.BlockSpec((tt, DN_WIDTH), row),
            pl.BlockSpec((tt, 128), row),
            pl.BlockSpec((CONV_W, 3 * DN_WIDTH), const),
            pl.BlockSpec((1, 128), const),
            pl.BlockSpec((1, 128), const),
            pl.BlockSpec((1, DN_DIM), const),
        ],
        out_specs=[
            pl.BlockSpec((tt, DN_WIDTH), row),
            pl.BlockSpec((None, DN_HEADS, DN_DIM, DN_DIM), lambda b, i: (b, 0, 0, 0)),
        ],
        out_shape=[
            jax.ShapeDtypeStruct((n, DN_WIDTH), F32),
            jax.ShapeDtypeStruct((nb, DN_HEADS, DN_DIM, DN_DIM), F32),
        ],
        scratch_shapes=[
            pltpu.VMEM((tt + 8, 3 * DN_WIDTH), F32),
            pltpu.VMEM((DN_HEADS, DN_DIM, DN_DIM), F32),
        ],
        compiler_params=_cparams(("arbitrary", "arbitrary")),
        name="delta_prompt",
    )(qkv, z, small, conv_w, alog_row, dt_row, norm_g)


def _sdelta_kernel(qkv_ref, buf_ref, z_ref, sm_ref, cw_ref, alog_ref, dt_ref, ng_ref, s_ref, o_ref, s_out_ref, *, rows):
    cw = cw_ref[...]
    y = qkv_ref[...] * cw[3:4, :]
    for j in range(CONV_W - 1):
        y = y + buf_ref[:, j, :] * cw[j:j + 1, :]
    act = _silu(y)
    sm = sm_ref[...]
    beta_all = _sigmoid(sm)
    g_all = -jnp.exp(alog_ref[...]) * _softplus(sm + dt_ref[...])
    ng = ng_ref[...]
    sub = lax.broadcasted_iota(I32, (8, DN_DIM), 0)
    for h in range(DN_HEADS):
        lo, hi = h * DN_DIM, (h + 1) * DN_DIM
        q_h = act[:, lo:hi]
        k_h = act[:, DN_WIDTH + lo:DN_WIDTH + hi]
        v_h = act[:, 2 * DN_WIDTH + lo:2 * DN_WIDTH + hi]
        q_h = q_h * lax.rsqrt(jnp.sum(q_h * q_h, axis=-1, keepdims=True) + 1e-6) * (DN_DIM ** -0.5)
        k_h = k_h * lax.rsqrt(jnp.sum(k_h * k_h, axis=-1, keepdims=True) + 1e-6)
        g = g_all[:, SM_A + h:SM_A + h + 1]
        b = beta_all[:, SM_B + h:SM_B + h + 1]
        eg = jnp.exp(g)
        u = v_h * b
        w = k_h * (b * eg)
        qe = q_h * eg
        qk = jnp.sum(q_h * k_h, axis=-1, keepdims=True)
        z_h = z_ref[:, lo:hi]
        for r in range(rows):
            s = s_ref[r, h]
            w8 = jnp.where(sub == 0, w[r:r + 1], 0.0)
            q8 = jnp.where(sub == 0, qe[r:r + 1], 0.0)
            k8 = jnp.where(sub == 0, k_h[r:r + 1], 0.0)
            v_new = u[r:r + 1] - _dot(w8, s, HI)[0:1]
            o = _dot(q8, s, HI)[0:1] + qk[r:r + 1] * v_new
            vn8 = jnp.where(sub == 0, v_new, 0.0)
            s_out_ref[r, h] = s * eg[r:r + 1] + _dot_tn(k8, vn8, HI)
            o = o * lax.rsqrt(jnp.mean(o * o, axis=-1, keepdims=True) + EPS) * ng
            o_ref[r:r + 1, lo:hi] = o * _silu(z_h[r:r + 1])


def _delta_sample(qkv, buf, z, small, conv_w, alog_row, dt_row, norm_g, layer, state, rows=8):
    n = qkv.shape[0]
    row = lambda i: (i, 0)
    const = lambda i: (0, 0)
    return pl.pallas_call(
        functools.partial(_sdelta_kernel, rows=rows),
        grid=(n // rows,),
        in_specs=[
            pl.BlockSpec((rows, 3 * DN_WIDTH), row),
            pl.BlockSpec((rows, CONV_W - 1, 3 * DN_WIDTH), lambda i: (i, 0, 0)),
            pl.BlockSpec((rows, DN_WIDTH), row),
            pl.BlockSpec((rows, 128), row),
            pl.BlockSpec((CONV_W, 3 * DN_WIDTH), const),
            pl.BlockSpec((1, 128), const),
            pl.BlockSpec((1, 128), const),
            pl.BlockSpec((1, DN_DIM), const),
            pl.BlockSpec((None, rows, DN_HEADS, DN_DIM, DN_DIM), lambda i: (layer, i, 0, 0, 0)),
        ],
        out_specs=[
            pl.BlockSpec((rows, DN_WIDTH), row),
            pl.BlockSpec((rows, DN_HEADS, DN_DIM, DN_DIM), lambda i: (i, 0, 0, 0)),
        ],
        out_shape=[
            jax.ShapeDtypeStruct((n, DN_WIDTH), F32),
            jax.ShapeDtypeStruct(state.shape[1:], F32),
        ],
        compiler_params=_cparams(("parallel",)),
        name="delta_sample",
    )(qkv, buf, z, small, conv_w, alog_row, dt_row, norm_g, state)


def _score_keys(s):
    s = jnp.where(s == 0.0, 0.0, s)
    k = pltpu.bitcast(s, I32)
    return jnp.where(k < 0, k ^ jnp.int32(0x7FFFFFFF), k)


SEARCH_ROUND = 4


def _alibi_slope(h):
    return 2.0 ** (-8.0 * (h + 1) / ATT_HEADS)


def _fold_groups(x, reduce_fn, ways=8):
    g = x.shape[0]
    if g % ways or g <= ways:
        return reduce_fn(x, axis=0)
    return reduce_fn(reduce_fn(x.reshape(g // ways, ways, 8, x.shape[-1]), axis=0), axis=0)


def _sublane_allreduce(x, op):
    for shift in (4, 2, 1):
        x = op(x, pltpu.roll(x, shift, axis=0))
    return x


def _attn_kernel(q_ref, iq_ref, sm_ref, k_ref, vt_ref, ik_ref, o_ref, keys, alibi, acc_sc, *, tq, ck):
    i = pl.program_id(1)
    nk = (i * tq + tq + ck - 1) // ck
    g8 = ck // 8
    t = i * tq + lax.broadcasted_iota(I32, (8, tq), 1)
    sub = lax.broadcasted_iota(I32, (ck, 1), 0)
    sub3 = lax.broadcasted_iota(I32, (g8, 8, 1), 0) * 8 + lax.broadcasted_iota(I32, (g8, 8, 1), 1)

    @pl.when((pl.program_id(0) == 0) & (i == 0))
    def _():
        for h in range(ATT_HEADS):
            alibi[h] = jnp.broadcast_to(sub.astype(F32) * (_alibi_slope(h) * LOG2E), (ck, tq))

    sm_t = sm_ref[...].T
    iw = [jnp.broadcast_to(sm_t[SM_IW + h:SM_IW + h + 1, :] * (IDX_HEADS ** -0.5 * IDX_DIM ** -0.5), (8, tq))
          for h in range(IDX_HEADS)]
    iq = iq_ref[...]

    def score_chunk(c, diagonal):
        off = pl.multiple_of(c * ck, ck)
        d = _dot_nt(ik_ref[pl.ds(off, ck), :], iq)
        s = jnp.maximum(d[:, 0:tq].reshape(g8, 8, tq), 0.0) * iw[0][None]
        for h in range(1, IDX_HEADS):
            s = s + jnp.maximum(d[:, h * tq:(h + 1) * tq].reshape(g8, 8, tq), 0.0) * iw[h][None]
        if diagonal:
            s = jnp.where(off + sub3 <= t[None], s, -jnp.inf)
        keys[c] = _score_keys(s)

    def score_full(c, carry):
        score_chunk(c, False)
        return carry

    lax.fori_loop(0, nk - 1, score_full, 0)
    score_chunk(nk - 1, True)

    kth = jnp.minimum(t + 1, TOPK)

    def count(pred_fn):
        def body(c, cnt):
            return cnt + _fold_groups(jnp.where(pred_fn(keys[c], c), 1, 0).astype(I32), jnp.sum)
        return _sublane_allreduce(lax.fori_loop(0, nk, body, jnp.zeros((8, tq), I32)), jnp.add)

    def search_cond(st):
        return (st[0] < 32) & (st[3] > 0)

    def search_round(st):
        it, thr, cnt_thr, _ = st
        for u in range(SEARCH_ROUND):
            cand = thr ^ jnp.left_shift(jnp.int32(1), 31 - u - it)
            cnt = count(lambda kc, c: kc >= cand[None])
            ok = cnt >= kth
            thr = jnp.where(ok, cand, thr)
            cnt_thr = jnp.where(ok, cnt, cnt_thr)
        return it + SEARCH_ROUND, thr, cnt_thr, jnp.max(jnp.where(cnt_thr != kth, 1, 0))

    _, thr, _, unresolved = lax.while_loop(
        search_cond, search_round,
        (jnp.int32(0), jnp.full((8, tq), -2 ** 31, I32), jnp.full((8, tq), 2 ** 30, I32), jnp.int32(1)))

    @pl.when(unresolved > 0)
    def _():
        need = kth - count(lambda kc, c: kc > thr[None])

        def pos_step(it, cut):
            cand = cut | jnp.left_shift(jnp.int32(1), 13 - it)
            cnt = count(lambda kc, c: (kc == thr[None]) & (c * ck + sub3 < cand[None]))
            return jnp.where(cnt < need, cand, cut)
        cut = lax.fori_loop(0, 14, pos_step, jnp.zeros((8, tq), I32))

        def drop(c, carry):
            kc = keys[c]
            keys[c] = jnp.where((kc == thr[None]) & (c * ck + sub3 > cut[None]), jnp.int32(-2 ** 31), kc)
            return carry
        lax.fori_loop(0, nk, drop, 0)

    acc_sc[...] = jnp.zeros_like(acc_sc)
    q = q_ref[...]
    qh = [q[:, h * ATT_DIM:(h + 1) * ATT_DIM] for h in range(ATT_HEADS)]
    heads = range(ATT_HEADS)

    def attend_chunk(c, carry):
        ms, ls = carry
        off = pl.multiple_of(c * ck, ck)
        selb = jnp.where(keys[c] >= thr[None], 0.0, NEG).reshape(ck, tq)
        kc = k_ref[pl.ds(off, ck), :]
        off_f = off.astype(F32)
        c_off = [off_f * (_alibi_slope(h) * LOG2E) for h in heads]
        zs = [(_dot_nt(kc[:, h * ATT_DIM:(h + 1) * ATT_DIM], qh[h]) * (ATT_DIM ** -0.5 * LOG2E)
               + (alibi[h] + selb)).reshape(g8, 8, tq) for h in heads]
        m_new = [jnp.maximum(ms[h], _sublane_allreduce(_fold_groups(zs[h], jnp.max), jnp.maximum) + c_off[h])
                 for h in heads]
        a = [jnp.exp2(ms[h] - m_new[h]) for h in heads]
        ps = [jnp.exp2(zs[h] - (m_new[h] - c_off[h])[None]) for h in heads]
        l_new = [a[h] * ls[h] + _sublane_allreduce(_fold_groups(ps[h], jnp.sum), jnp.add) for h in heads]
        for h in heads:
            pv = _dot(vt_ref[c, h * ATT_DIM:(h + 1) * ATT_DIM, :], ps[h].reshape(ck, tq).astype(BF16))
            acc_sc[h] = a[h][None] * acc_sc[h] + pv.reshape(ATT_DIM // 8, 8, tq)
        return tuple(m_new), tuple(l_new)

    init = (tuple(jnp.full((8, tq), NEG, F32) for _ in heads), tuple(jnp.zeros((8, tq), F32) for _ in heads))
    _, ls = lax.fori_loop(0, nk, attend_chunk, init)
    for h in heads:
        o_ref[:, h * ATT_DIM:(h + 1) * ATT_DIM] = (acc_sc[h] / ls[h][None]).reshape(ATT_DIM, tq).T


def _attn_prompt(q16, iq16, small, k16, vt16, ik16, nb, tq=128, ck=512):
    n = q16.shape[0]
    s = n // nb
    nq = s // tq
    row = lambda b, i: (b * nq + i, 0)
    full = lambda b, i: (b, 0)
    once = pl.Buffered(1)
    return pl.pallas_call(
        functools.partial(_attn_kernel, tq=tq, ck=ck),
        grid=(nb, nq),
        in_specs=[
            pl.BlockSpec((tq, ATT_WIDTH), row),
            pl.BlockSpec((IDX_HEADS * tq, IDX_DIM), row),
            pl.BlockSpec((tq, 128), row),
            pl.BlockSpec((s, ATT_WIDTH), full, pipeline_mode=once),
            pl.BlockSpec((None, s // ck, ATT_WIDTH, ck), lambda b, i: (b, 0, 0, 0), pipeline_mode=once),
            pl.BlockSpec((s, IDX_DIM), full, pipeline_mode=once),
        ],
        out_specs=pl.BlockSpec((tq, ATT_WIDTH), row),
        out_shape=jax.ShapeDtypeStruct((n, ATT_WIDTH), F32),
        scratch_shapes=[
            pltpu.VMEM((s // ck, ck // 8, 8, tq), I32),
            pltpu.VMEM((ATT_HEADS, ck, tq), F32),
            pltpu.VMEM((ATT_HEADS, ATT_DIM // 8, 8, tq), F32),
        ],
        compiler_params=_cparams(("arbitrary", "arbitrary")),
        name="attn_prompt",
    )(q16, iq16, small, k16, vt16, ik16)


PG = 8
SCORE_PAGES = 32


def _sscore_kernel(pt_ref, iq_ref, iw_ref, ikn_ref, *refs, pgs):
    pages, o_ref = refs[:pgs], refs[pgs]
    j = pl.program_id(1)
    iq = iq_ref[...]
    iw = iw_ref[...] * (IDX_HEADS ** -0.5)

    @pl.when(j < pl.num_programs(1) - 1)
    def _():
        for u in range(pgs):
            d = _dot(iq, pages[u][...]) * (IDX_DIM ** -0.5)
            o_ref[:, u * PAGE:(u + 1) * PAGE] = jnp.sum(jnp.maximum(d, 0.0) * iw, axis=0, keepdims=True)

    @pl.when(j == pl.num_programs(1) - 1)
    def _():
        d = _dot_nt(iq, jnp.broadcast_to(ikn_ref[...], (8, IDX_DIM)))[:, 0:1] * (IDX_DIM ** -0.5)
        s_new = jnp.sum(jnp.maximum(d, 0.0) * iw, axis=0, keepdims=True)
        lane = lax.broadcasted_iota(I32, (1, pgs * PAGE), 1)
        o_ref[...] = jnp.where(lane == 0, s_new, -jnp.inf)


def _sample_scores(layer, page_table, iq3, iw3, ik_new3, cache_idx_t):
    nbatch, n_pages = page_table.shape
    pgs = min(SCORE_PAGES, n_pages)
    nj = n_pages // pgs

    def page_map(u):
        return lambda b, j, pt: (layer, pt[b, jnp.minimum(j * pgs + u, n_pages - 1)], 0, 0)

    grid_spec = pltpu.PrefetchScalarGridSpec(
        num_scalar_prefetch=1,
        grid=(nbatch, nj + 1),
        in_specs=[
            pl.BlockSpec((None, IDX_HEADS, IDX_DIM), lambda b, j, pt: (b, 0, 0)),
            pl.BlockSpec((None, IDX_HEADS, 1), lambda b, j, pt: (b, 0, 0)),
            pl.BlockSpec((None, 1, IDX_DIM), lambda b, j, pt: (b, 0, 0)),
        ] + [pl.BlockSpec((None, None, IDX_DIM, PAGE), page_map(u)) for u in range(pgs)],
        out_specs=pl.BlockSpec((None, 1, pgs * PAGE), lambda b, j, pt: (b, 0, j)),
    )
    return pl.pallas_call(
        functools.partial(_sscore_kernel, pgs=pgs),
        grid_spec=grid_spec,
        out_shape=jax.ShapeDtypeStruct((nbatch, 1, (nj + 1) * pgs * PAGE), F32),
        compiler_params=_cparams(("arbitrary", "arbitrary")),
        name="sample_scores",
    )(page_table, iq3, iw3, ik_new3, *([cache_idx_t] * pgs))


def _sselect_kernel(s_ref, o_ref):
    keys = _score_keys(s_ref[...])
    rows, width = keys.shape
    pos = lax.broadcasted_iota(I32, (1, width), 1)

    def cnt(m):
        return jnp.sum(jnp.where(m, 1, 0).astype(I32), axis=-1, keepdims=True)

    def search_step(it, thr):
        cand = thr ^ jnp.left_shift(jnp.int32(1), 31 - it)
        return jnp.where(cnt(keys >= cand) >= TOPK, cand, thr)

    thr = lax.fori_loop(0, 32, search_step, jnp.full((rows, 1), -2 ** 31, I32))
    need = TOPK - cnt(keys > thr)
    eq = keys == thr

    def pos_step(it, cut):
        cand = cut | jnp.left_shift(jnp.int32(1), 14 - it)
        return jnp.where(cnt(eq & (pos < cand)) < need, cand, cut)

    cut = lax.fori_loop(0, 15, pos_step, jnp.zeros((rows, 1), I32))
    sel = (keys > thr) | (eq & (pos <= cut))
    o_ref[...] = jnp.where(sel, 0.0, NEG)


def _sample_select(scores):
    return pl.pallas_call(
        _sselect_kernel,
        out_shape=jax.ShapeDtypeStruct(scores.shape, F32),
        compiler_params=pltpu.CompilerParams(vmem_limit_bytes=VMEM_LIMIT),
        name="sample_select",
    )(scores)


PAGE_ROWS = PAGE * ATT_HEADS


def _sattn_kernel(pt_ref, q_ref, kn_ref, vn_ref, selb_ref, seln_ref, *refs, past):
    kp, vp = refs[:PG], refs[PG:2 * PG]
    o_ref, lg_sc, l_sc, pn_sc, acc_sc = refs[2 * PG:2 * PG + 5]
    ph = pl.program_id(1)
    j = pl.program_id(2)
    scale = ATT_DIM ** -0.5
    q8 = q_ref[...]
    row = lax.broadcasted_iota(I32, (8, 1), 0)
    lane = lax.broadcasted_iota(I32, (1, PAGE_ROWS), 1)
    slope = jnp.exp2(-8.0 * (row + 1).astype(F32) / ATT_HEADS)
    own = jnp.bitwise_and(lane, ATT_HEADS - 1) == row
    tok = jnp.right_shift(lane, 2)

    @pl.when(ph == 0)
    def _():
        for u in range(PG):
            dist = (past - ((j * PG + u) * PAGE + tok)).astype(F32)
            lg = _dot_nt(q8, kp[u][...]) * scale - slope * dist + selb_ref[:, u * PAGE_ROWS:(u + 1) * PAGE_ROWS]
            lg_sc[j * PG + u] = jnp.where(own, lg, NEG)

    @pl.when((ph == 1) & (j == 0))
    def _():
        allg = lg_sc[...]
        lgn = jnp.sum(q8 * kn_ref[...], axis=-1, keepdims=True) * scale + seln_ref[:, 0:1]
        m = jnp.maximum(jnp.max(jnp.max(allg, axis=0), axis=-1, keepdims=True), lgn)
        p = jnp.exp(allg - m)
        lg_sc[...] = p
        pn = jnp.exp(lgn - m)
        l_sc[...] = jnp.sum(jnp.sum(p, axis=0), axis=-1, keepdims=True) + pn
        pn_sc[...] = pn
        acc_sc[...] = jnp.zeros_like(acc_sc)

    @pl.when(ph == 1)
    def _():
        acc = acc_sc[...]
        for u in range(PG):
            acc = acc + _dot(lg_sc[j * PG + u], vp[u][...])
        acc_sc[...] = acc

    @pl.when((ph == 1) & (j == pl.num_programs(2) - 1))
    def _():
        o_ref[...] = (acc_sc[...] + pn_sc[...] * vn_ref[...]) / l_sc[...]


def _sample_attend(layer, page_table, q8, kn8, vn8, selb4, selb, cache_k4, cache_v4):
    nbatch, n_pages = page_table.shape
    nj = n_pages // PG
    past = n_pages * PAGE

    def k_map(u):
        return lambda b, ph, j, pt: (layer, pt[b, jnp.where(ph == 0, j, nj - 1) * PG + u], 0, 0)

    def v_map(u):
        return lambda b, ph, j, pt: (layer, pt[b, jnp.where(ph == 0, 0, j) * PG + u], 0, 0)

    vec = pl.BlockSpec((None, 8, ATT_DIM), lambda b, ph, j, pt: (b, 0, 0))
    grid_spec = pltpu.PrefetchScalarGridSpec(
        num_scalar_prefetch=1,
        grid=(nbatch, 2, nj),
        in_specs=[
            vec, vec, vec,
            pl.BlockSpec((None, 1, PG * PAGE_ROWS), lambda b, ph, j, pt: (b, 0, jnp.where(ph == 0, j, nj - 1))),
            pl.BlockSpec((None, 1, 128), lambda b, ph, j, pt: (b, 0, past // 128)),
        ] + [pl.BlockSpec((None, None, PAGE_ROWS, ATT_DIM), k_map(u)) for u in range(PG)]
          + [pl.BlockSpec((None, None, PAGE_ROWS, ATT_DIM), v_map(u)) for u in range(PG)],
        out_specs=vec,
        scratch_shapes=[
            pltpu.VMEM((n_pages, 8, PAGE_ROWS), F32),
            pltpu.VMEM((8, 1), F32),
            pltpu.VMEM((8, 1), F32),
            pltpu.VMEM((8, ATT_DIM), F32),
        ],
    )
    return pl.pallas_call(
        functools.partial(_sattn_kernel, past=past),
        grid_spec=grid_spec,
        out_shape=jax.ShapeDtypeStruct((nbatch, 8, ATT_DIM), F32),
        compiler_params=_cparams(("arbitrary", "arbitrary", "arbitrary")),
        name="sample_attend",
    )(page_table, q8, kn8, vn8, selb4, selb, *([cache_k4] * PG), *([cache_v4] * PG))


def _mix_kernel(oa_ref, ob_ref, gates_ref, x_ref, gm_ref, wa_ref, wb_ref, wo_ref, g_ref, b_ref, o_ref):
    br_a = _dot(oa_ref[...].astype(BF16), wa_ref[...])
    br_b = _dot(ob_ref[...].astype(BF16), wb_ref[...])
    merged = _sigmoid(gates_ref[:, :D_MODEL]) * br_a + _sigmoid(gates_ref[:, D_MODEL:]) * br_b
    y = _dot(merged.astype(BF16), wo_ref[...])
    r = ALPHA * x_ref[...] + gm_ref[...] * y
    o_ref[...] = _layernorm(r, g_ref[...], b_ref[...])


def _mix(oa, ob, gates, x, gm, wa, wb, wo, ln_g, ln_b, tm):
    n = x.shape[0]
    nb, r, _ = gm.shape
    nt = n // tm
    tiles_per_b = nt // nb
    row = lambda i: (i, 0)
    const = lambda i: (0, 0)
    return pl.pallas_call(
        _mix_kernel,
        grid=(nt,),
        in_specs=[
            pl.BlockSpec((tm, DN_WIDTH), row),
            pl.BlockSpec((tm, ATT_WIDTH), row),
            pl.BlockSpec((tm, 2 * D_MODEL), row),
            pl.BlockSpec((tm, D_MODEL), row),
            pl.BlockSpec((None, r, D_MODEL), lambda i: (i // tiles_per_b, 0, 0)),
            pl.BlockSpec((DN_WIDTH, D_MODEL), const),
            pl.BlockSpec((ATT_WIDTH, D_MODEL), const),
            pl.BlockSpec((D_MODEL, D_MODEL), const),
            pl.BlockSpec((1, D_MODEL), const),
            pl.BlockSpec((1, D_MODEL), const),
        ],
        out_specs=pl.BlockSpec((tm, D_MODEL), row),
        out_shape=jax.ShapeDtypeStruct((n, D_MODEL), F32),
        compiler_params=_cparams(("parallel",)),
        name="mix_out",
    )(oa, ob, gates, x, gm, wa, wb, wo, ln_g, ln_b)


def _route_kernel(x_ref, sc_ref, sh_ref, wr_ref, o_ref, e_ref, g_ref):
    h = x_ref[...] * (1.0 + sc_ref[...]) + sh_ref[...]
    o_ref[...] = h.astype(o_ref.dtype)
    logits = _dot(h, wr_ref[...], HI)
    lane = lax.broadcasted_iota(I32, logits.shape, 1)
    logits = jnp.where(lane < N_EXPERTS, logits, -jnp.inf)
    e = jnp.exp(logits - jnp.max(logits, axis=-1, keepdims=True))
    p = e / jnp.sum(e, axis=-1, keepdims=True)
    p1 = jnp.max(p, axis=-1, keepdims=True)
    e1 = jnp.min(jnp.where(p == p1, lane, 128), axis=-1, keepdims=True)
    rest = jnp.where((lane == e1) | (lane >= N_EXPERTS), -1.0, p)
    p2 = jnp.max(rest, axis=-1, keepdims=True)
    e2 = jnp.min(jnp.where(rest == p2, lane, 128), axis=-1, keepdims=True)
    tot = p1 + p2
    e_ref[...] = jnp.concatenate([e1, e2], axis=1)
    g_ref[...] = jnp.concatenate([p1 / tot, p2 / tot], axis=1)


def _ffn_route(x, sc, sh, tm, w_router):
    n = x.shape[0]
    nb, r, _ = sc.shape
    nt = n // tm
    tiles_per_b = nt // nb
    row = lambda i: (i, 0)
    mod_spec = pl.BlockSpec((None, r, D_MODEL), lambda i: (i // tiles_per_b, 0, 0))
    in_specs = [pl.BlockSpec((tm, D_MODEL), row), mod_spec, mod_spec]
    wr = jnp.pad(w_router, ((0, 0), (0, 128 - N_EXPERTS)))
    return pl.pallas_call(
        _route_kernel, grid=(nt,),
        in_specs=in_specs + [pl.BlockSpec((D_MODEL, 128), lambda i: (0, 0))],
        out_specs=[pl.BlockSpec((tm, D_MODEL), row), pl.BlockSpec((tm, 2), row), pl.BlockSpec((tm, 2), row)],
        out_shape=[jax.ShapeDtypeStruct((n, D_MODEL), F32), jax.ShapeDtypeStruct((n, 2), I32),
                   jax.ShapeDtypeStruct((n, 2), F32)],
        compiler_params=_cparams(("parallel",)), name="ffn_route",
    )(x, sc, sh, wr)


def _gffn_kernel(te_ref, nu_ref, x_ref, *refs, modulate, substitute):
    if modulate:
        sc_ref, sh_ref = refs[:2]
        refs = refs[2:]
    if substitute:
        src_ref, late_ref = refs[:2]
        refs = refs[2:]
    wg_ref, wu_ref, wd_ref, o_ref, acc, x16 = refs
    t = pl.program_id(0)
    f = pl.program_id(1)

    @pl.when((t < nu_ref[0]) & (f == 0))
    def _():
        x = x_ref[...]
        if modulate:
            x = x * (1.0 + sc_ref[...]) + sh_ref[...]
        x = x.astype(BF16)
        if substitute:
            src = src_ref[...]
            late = late_ref[...]
            pick = jnp.where(src == lax.broadcasted_iota(I32, (src.shape[0], late.shape[0]), 1), 1.0, 0.0)
            x = jnp.where(src >= 0, _dot(pick.astype(BF16), late).astype(BF16), x)
        x16[...] = x

    @pl.when(t < nu_ref[0])
    def _():
        x = x16[...]
        a = _dot(x, wg_ref[...].astype(BF16))
        b = _dot(x, wu_ref[...].astype(BF16))
        y = _dot((_silu(a) * b).astype(BF16), wd_ref[...].astype(BF16))

        @pl.when(f == 0)
        def _():
            acc[...] = y

        @pl.when(f > 0)
        def _():
            acc[...] += y

    @pl.when(f == pl.num_programs(1) - 1)
    def _():
        o_ref[...] = jnp.where(t < nu_ref[0], acc[...], 0.0)


def _grouped_ffn(x, tile_expert, n_used, layer, w_gate, w_up, w_down, tm, mod=None, late=None, tf=256):
    n = x.shape[0]
    nt = n // tm
    nf = D_FF // tf
    mod_specs, mod_args = [], []
    if mod is not None:
        nb, r, _ = mod[0].shape
        tiles_per_b = nt // nb
        mod_specs = [pl.BlockSpec((None, r, D_MODEL), lambda t, f, te, nu: (t // tiles_per_b, 0, 0))] * 2
        mod_args = list(mod)
    if late is not None:
        mod_specs = mod_specs + [pl.BlockSpec((tm, 1), lambda t, f, te, nu: (t, 0)),
                                 pl.BlockSpec(late[1].shape, lambda t, f, te, nu: (0, 0))]
        mod_args = mod_args + list(late)
    grid_spec = pltpu.PrefetchScalarGridSpec(
        num_scalar_prefetch=2,
        grid=(nt, nf),
        in_specs=[pl.BlockSpec((tm, D_MODEL), lambda t, f, te, nu: (t, 0))] + mod_specs + [
            pl.BlockSpec((None, None, D_MODEL, tf), lambda t, f, te, nu: (layer, te[t], 0, f)),
            pl.BlockSpec((None, None, D_MODEL, tf), lambda t, f, te, nu: (layer, te[t], 0, f)),
            pl.BlockSpec((None, None, tf, D_MODEL), lambda t, f, te, nu: (layer, te[t], f, 0)),
        ],
        out_specs=pl.BlockSpec((tm, D_MODEL), lambda t, f, te, nu: (t, 0)),
        scratch_shapes=[pltpu.VMEM((tm, D_MODEL), F32), pltpu.VMEM((tm, D_MODEL), BF16)],
    )
    return pl.pallas_call(
        functools.partial(_gffn_kernel, modulate=mod is not None, substitute=late is not None),
        grid_spec=grid_spec,
        out_shape=jax.ShapeDtypeStruct((n, D_MODEL), F32),
        compiler_params=_cparams(("arbitrary", "arbitrary")),
        name="grouped_ffn",
    )(tile_expert, n_used, x, *mod_args, w_gate, w_up, w_down)


def _res_ln_kernel(x_ref, y_ref, gf_ref, g_ref, b_ref, o_ref):
    r = ALPHA * x_ref[...] + gf_ref[...] * y_ref[...]
    o_ref[...] = _layernorm(r, g_ref[...], b_ref[...])


def _res_ln2_kernel(x_ref, y0_ref, y1_ref, gt_ref, gf_ref, g_ref, b_ref, o_ref):
    gt = gt_ref[...]
    y = y0_ref[...] * gt[:, 0:1] + y1_ref[...] * gt[:, 1:2]
    r = ALPHA * x_ref[...] + gf_ref[...] * y
    o_ref[...] = _layernorm(r, g_ref[...], b_ref[...])


def _res_ln(x, ys, gate2, gf, ln_g, ln_b, tm):
    n = x.shape[0]
    nb, r, _ = gf.shape
    nt = n // tm
    tiles_per_b = nt // nb
    row = lambda i: (i, 0)
    const = lambda i: (0, 0)
    big = pl.BlockSpec((tm, D_MODEL), row)
    mod_spec = pl.BlockSpec((None, r, D_MODEL), lambda i: (i // tiles_per_b, 0, 0))
    vec = pl.BlockSpec((1, D_MODEL), const)
    if gate2 is None:
        kern, ins, specs = _res_ln_kernel, (x, ys[0], gf, ln_g, ln_b), [big, big, mod_spec, vec, vec]
    else:
        kern = _res_ln2_kernel
        ins = (x, ys[0], ys[1], gate2, gf, ln_g, ln_b)
        specs = [big, big, big, pl.BlockSpec((tm, 2), row), mod_spec, vec, vec]
    return pl.pallas_call(
        kern, grid=(nt,), in_specs=specs, out_specs=big,
        out_shape=jax.ShapeDtypeStruct((n, D_MODEL), F32),
        compiler_params=_cparams(("parallel",)), name="res_ln",
    )(*ins)


def _dense_ffn(x, sc, sh, gf, layer, w_gate, w_up, w_down, ln_g, ln_b, tm_mod, tm_ffn):
    n = x.shape[0]
    nt = n // tm_ffn
    y = _grouped_ffn(x, jnp.zeros((nt,), I32), jnp.full((1,), nt, I32), layer,
                     w_gate[:, None], w_up[:, None], w_down[:, None], tm_ffn, mod=(sc, sh))
    return _res_ln(x, (y,), None, gf, ln_g, ln_b, tm_mod)


def _expert_ranks(top_e):
    flat_e = top_e.reshape(-1)
    onehot = flat_e[:, None] == jnp.arange(N_EXPERTS, dtype=I32)[None, :]
    seen = jnp.cumsum(onehot.astype(I32), axis=0)
    pick = lambda table: jnp.sum(jnp.where(onehot, table, 0), axis=1)
    return flat_e, seen[-1], pick(seen) - 1, pick


def _moe_ffn(main, late, w_router, layer, w_gate, w_up, w_down, ln_g, ln_b, tm_e):
    x_m, x_l = main[0], late[0]
    n_m, n_l = x_m.shape[0], x_l.shape[0]
    h_m, top_m, gate_m = _ffn_route(*main[:3], main[4], w_router)
    flat_m, counts_m, rank_m, pick_m = _expert_ranks(top_m)
    padded = (counts_m + n_l + tm_e - 1) // tm_e * tm_e
    pad_end = jnp.cumsum(padded)
    pad_start = pad_end - padded
    n_tiles = -(-(2 * n_m + N_EXPERTS * n_l) // tm_e) + N_EXPERTS
    tile_expert = jnp.minimum(
        jnp.searchsorted(pad_end, jnp.arange(n_tiles, dtype=I32) * tm_e, side="right"), N_EXPERTS - 1).astype(I32)
    n_used = (pad_end[-1] // tm_e).astype(I32).reshape(1)
    slot_m = (pick_m(pad_start[None, :]) + rank_m).reshape(n_m, 2)
    rows_m = jnp.zeros((n_tiles * tm_e,), I32).at[slot_m.reshape(-1)].set(
        jnp.arange(2 * n_m, dtype=I32) // 2, unique_indices=True)
    xb = h_m[rows_m]

    h_l, top_l, gate_l = _ffn_route(*late[:3], late[4], w_router)
    flat_l, counts_l, rank_l, pick_l = _expert_ranks(top_l)
    slot_l = (pick_l((pad_start + counts_m)[None, :]) + rank_l).reshape(n_l, 2)
    src_l = jnp.full((n_tiles * tm_e,), -1, I32).at[slot_l.reshape(-1)].set(
        jnp.arange(2 * n_l, dtype=I32) // 2, unique_indices=True).reshape(-1, 1)
    yb = _grouped_ffn(xb, tile_expert, n_used, layer, w_gate, w_up, w_down, tm_e, late=(src_l, h_l.astype(BF16)))
    out_m = _res_ln(x_m, (yb[slot_m[:, 0]], yb[slot_m[:, 1]]), gate_m, main[3], ln_g, ln_b, main[4])
    out_l = _res_ln(x_l, (yb[slot_l[:, 0]], yb[slot_l[:, 1]]), gate_l, late[3], ln_g, ln_b, late[4])
    return out_m, out_l


def kernel(x_prompt, x_sample, cache_k, cache_v, cache_idx_k, state_delta, state_conv, page_table, c_prompt, c_sample, w_ada, b_ada, w_in, conv_w, a_log, dt_bias, dn_norm_g, w_branch, w_out, ln1_g, ln1_b, ln2_g, ln2_b, w_ffn_gate, w_ffn_up, w_ffn_down, w_router, w_exp_gate, w_exp_up, w_exp_down):
    bp, seq, _ = x_prompt.shape
    bs = x_sample.shape[0]
    n_p = bp * seq
    n_phys = cache_k.shape[1]
    past = page_table.shape[1] * PAGE

    c_all = jnp.concatenate([c_prompt, c_sample], axis=0)
    c_rows = -(-c_all.shape[0] // 8) * 8
    c_all = jnp.pad(c_all, ((0, c_rows - c_all.shape[0]), (0, 0)))
    mods = _ada_all(c_all, w_ada, b_ada)

    cache_k4 = cache_k.reshape(DEPTH, n_phys, PAGE_ROWS, ATT_DIM)
    cache_v4 = cache_v.reshape(DEPTH, n_phys, PAGE_ROWS, ATT_DIM)
    cache_idx_t = jnp.swapaxes(cache_idx_k, 2, 3)
    ck = 512

    xp = x_prompt.reshape(n_p, D_MODEL)
    xs = x_sample.reshape(bs, D_MODEL)
    lane = jnp.arange(128)
    outs_p, outs_s = [], []
    for l in range(DEPTH):
        j = l // 2
        mod_p = [mods[l, :bp, u * D_MODEL:(u + 1) * D_MODEL].reshape(bp, 1, D_MODEL) for u in range(6)]
        mod_s = [mods[l, bp:bp + bs, u * D_MODEL:(u + 1) * D_MODEL].reshape(1, bs, D_MODEL) for u in range(6)]
        w_groups = _split_w_in(w_in[l])
        head_lane = jnp.clip(lane - SM_A, 0, DN_HEADS - 1)
        in_a = (lane >= SM_A) & (lane < SM_A + DN_HEADS)
        alog_row = jnp.where(in_a, a_log[l][head_lane], 0.0).reshape(1, 128)
        dt_row = jnp.where(in_a, dt_bias[l][head_lane], 0.0).reshape(1, 128)
        norm_g = dn_norm_g[l].reshape(1, DN_DIM)
        wa16 = w_branch[l, 0].astype(BF16)
        wb16 = w_branch[l, 1].astype(BF16)
        wo16 = w_out[l].astype(BF16)
        ln1 = (ln1_g[l].reshape(1, -1), ln1_b[l].reshape(1, -1))
        ln2 = (ln2_g[l].reshape(1, -1), ln2_b[l].reshape(1, -1))

        pr = _in_proj(xp, mod_p[1], mod_p[0], w_groups, tm=256)
        oa, s_new_p = _delta_prompt(pr["qkv_a"], pr["z"], pr["small"], conv_w[l], alog_row, dt_row, norm_g, bp)
        ik16 = pr["small"][:, SM_IK:SM_IK + IDX_DIM].astype(BF16)
        vt16 = jnp.swapaxes(pr["v_b16"].reshape(bp, seq // ck, ck, ATT_WIDTH), 2, 3)
        tq = 256
        iq_hm = jnp.swapaxes(pr["iq"].reshape(n_p // tq, tq, IDX_HEADS, IDX_DIM), 1, 2).reshape(-1, IDX_DIM)
        ob = _attn_prompt(pr["q_b"], iq_hm, pr["small"], pr["k_b16"], vt16, ik16, bp, tq=tq, ck=ck)
        x1 = _mix(oa, ob, pr["gates"], xp, mod_p[2], wa16, wb16, wo16, *ln1, tm=512)
        conv_p =pr["qkv_a"].reshape(bp, seq, -1)[:, seq - (CONV_W - 1):, :]
        outs_p.append((conv_p, s_new_p, pr["k_b"].reshape(bp, seq, ATT_HEADS, ATT_DIM),
                       pr["v_b"].reshape(bp, seq, ATT_HEADS, ATT_DIM),
                       pr["small"][:, SM_IK:SM_IK + IDX_DIM].reshape(bp, seq, IDX_DIM)))

        sr = _in_proj(xs, mod_s[1], mod_s[0], w_groups, tm=bs)
        oa_s, s_new_s = _delta_sample(sr["qkv_a"], state_conv[l], sr["z"], sr["small"], conv_w[l], alog_row, dt_row,
                                      norm_g, l, state_delta)
        ik_s = sr["small"][:, SM_IK:SM_IK + IDX_DIM]
        iq3 = sr["iq"].astype(F32).reshape(bs, IDX_HEADS, IDX_DIM)
        iw3 = sr["small"][:, SM_IW:SM_IW + IDX_HEADS].reshape(bs, IDX_HEADS, 1)
        scores = _sample_scores(l, page_table, iq3, iw3, ik_s.reshape(bs, 1, IDX_DIM), cache_idx_t)
        selb = _sample_select(scores.reshape(bs, -1)).reshape(scores.shape)
        head_rows = lambda a: jnp.pad(a.astype(F32).reshape(bs, ATT_HEADS, ATT_DIM), ((0, 0), (0, 8 - ATT_HEADS), (0, 0)))
        selb4 = jnp.repeat(selb, ATT_HEADS, axis=-1)
        ob_s = _sample_attend(l, page_table, head_rows(sr["q_b"]), head_rows(sr["k_b"]), head_rows(sr["v_b"]),
                              selb4, selb, cache_k4, cache_v4)[:, :ATT_HEADS, :].reshape(bs, ATT_WIDTH)
        x1s = _mix(oa_s, ob_s, sr["gates"], xs, mod_s[2], wa16, wb16, wo16, *ln1, tm=bs)

        if l % 2 == 0:
            xp = _dense_ffn(x1, mod_p[4], mod_p[3], mod_p[5], j, w_ffn_gate, w_ffn_up, w_ffn_down, *ln2,
                            tm_mod=512, tm_ffn=1024)
            xs = _dense_ffn(x1s, mod_s[4], mod_s[3], mod_s[5], j, w_ffn_gate, w_ffn_up, w_ffn_down, *ln2,
                            tm_mod=bs, tm_ffn=bs)
        else:
            xp, xs = _moe_ffn((x1, mod_p[4], mod_p[3], mod_p[5], 512), (x1s, mod_s[4], mod_s[3], mod_s[5], bs),
                              w_router[j], j, w_exp_gate, w_exp_up, w_exp_down, *ln2, tm_e=1024)
        conv_s =jnp.concatenate([state_conv[l][:, 1:, :], sr["qkv_a"][:, None, :]], axis=1)
        outs_s.append((conv_s, s_new_s, sr["k_b"].reshape(bs, 1, ATT_HEADS, ATT_DIM),
                       sr["v_b"].reshape(bs, 1, ATT_HEADS, ATT_DIM), ik_s.reshape(bs, 1, IDX_DIM)))

    stack = lambda rows, u: jnp.stack([r[u] for r in rows])
    return (xp.reshape(bp, seq, D_MODEL), xs.reshape(bs, 1, D_MODEL),
            stack(outs_p, 2), stack(outs_p, 3), stack(outs_p, 4), stack(outs_p, 1), stack(outs_p, 0),
            stack(outs_s, 2), stack(outs_s, 3), stack(outs_s, 4), stack(outs_s, 1), stack(outs_s, 0))
```

```python
import functools

import jax
import jax.numpy as jnp
from jax import lax
from jax.experimental import pallas as pl
from jax.experimental.pallas import tpu as pltpu

F32 = jnp.float32
BF16 = jnp.bfloat16
I32 = jnp.int32
HI = lax.Precision.HIGHEST

D_MODEL = 1024
DEPTH = 4
PAGE = 128
DN_HEADS = 4
DN_DIM = 128
DN_WIDTH = DN_HEADS * DN_DIM
CONV_W = 4
DN_CHUNK = 64
ATT_HEADS = 4
ATT_DIM = 128
ATT_WIDTH = ATT_HEADS * ATT_DIM
IDX_HEADS = 8
IDX_DIM = 64
TOPK = 256
D_FF = 2816
N_EXPERTS = 8
ALPHA = (2.0 * DEPTH) ** 0.25
EPS = 1e-5
NEG = -1e30
LOG2E = 1.4426950408889634

SM_IK = 0
SM_B = 64
SM_A = 68
SM_IW = 72

VMEM_LIMIT = 56 * 1024 * 1024


def _cparams(sem):
    return pltpu.CompilerParams(dimension_semantics=sem, vmem_limit_bytes=VMEM_LIMIT)


def _sigmoid(x):
    return jax.nn.sigmoid(x)


def _silu(x):
    return x * jax.nn.sigmoid(x)


def _softplus(x):
    return jnp.maximum(x, 0.0) + jnp.log1p(jnp.exp(-jnp.abs(x)))


def _dot(a, b, precision=None):
    return jnp.dot(a, b, precision=precision, preferred_element_type=F32)


def _dot_nt(a, b, precision=None):
    return lax.dot_general(a, b, (((1,), (1,)), ((), ())), precision=precision, preferred_element_type=F32)


def _dot_tn(a, b, precision=None):
    return lax.dot_general(a, b, (((0,), (0,)), ((), ())), precision=precision, preferred_element_type=F32)


def _split(a):
    hi = a.astype(BF16)
    return hi, (a - hi.astype(F32)).astype(BF16)


def _dot3(a, b):
    return _dot(a[0], b[0]) + (_dot(a[0], b[1]) + _dot(a[1], b[0]))


def _layernorm(r, g, b):
    mu = jnp.mean(r, axis=-1, keepdims=True)
    rc = r - mu
    var = jnp.mean(rc * rc, axis=-1, keepdims=True)
    return rc * lax.rsqrt(var + EPS) * g + b


def _ada_kernel(c_ref, w_ref, b_ref, o_ref):
    o_ref[...] = _dot(_silu(c_ref[...]), w_ref[...], HI) + b_ref[...]


def _ada_all(c, w_ada, b_ada):
    rows = c.shape[0]
    nj = w_ada.shape[2] // D_MODEL
    return pl.pallas_call(
        _ada_kernel,
        grid=(DEPTH, nj),
        in_specs=[
            pl.BlockSpec((rows, D_MODEL), lambda l, j: (0, 0)),
            pl.BlockSpec((None, D_MODEL, D_MODEL), lambda l, j: (l, 0, j)),
            pl.BlockSpec((None, 1, D_MODEL), lambda l, j: (l, 0, j)),
        ],
        out_specs=pl.BlockSpec((None, rows, D_MODEL), lambda l, j: (l, 0, j)),
        out_shape=jax.ShapeDtypeStruct((DEPTH, rows, nj * D_MODEL), F32),
        compiler_params=_cparams(("arbitrary", "arbitrary")),
        name="ada_mod",
    )(c, w_ada, b_ada.reshape(DEPTH, 1, -1))


IN_GROUPS = (
    ("qkv_a", 3 * DN_WIDTH, (F32,)),
    ("z", DN_WIDTH, (F32,)),
    ("q_b", ATT_WIDTH, (BF16,)),
    ("k_b", ATT_WIDTH, (F32, BF16)),
    ("v_b", ATT_WIDTH, (F32, BF16)),
    ("iq", IDX_HEADS * IDX_DIM, (BF16,)),
    ("gates", 2 * D_MODEL, (F32,)),
    ("small", 128, (F32, BF16)),
)


HEAD_MAJOR_ROWS = (("k_b", F32), ("v_b", F32))
IDX_Q_TILES = ("iq", BF16)
VALUES_T = ("v_b", BF16)
IDX_K_ONLY = ("small", BF16)


def _split_w_in(w):
    o = 0
    parts = {}
    for name, width in (("qkv_a", 3 * DN_WIDTH), ("z", DN_WIDTH), ("b", DN_HEADS), ("a", DN_HEADS),
                        ("q_b", ATT_WIDTH), ("k_b", ATT_WIDTH), ("v_b", ATT_WIDTH),
                        ("iq", IDX_HEADS * IDX_DIM), ("ik", IDX_DIM), ("iw", IDX_HEADS), ("gates", 2 * D_MODEL)):
        parts[name] = w[:, o:o + width]
        o += width
    small = jnp.concatenate(
        [parts["ik"], parts["b"], parts["a"], parts["iw"],
         jnp.zeros((w.shape[0], 128 - IDX_DIM - 2 * DN_HEADS - IDX_HEADS), w.dtype)], axis=1)
    parts["small"] = small
    return [parts[name].astype(BF16) for name, _, _ in IN_GROUPS]


def _in_kernel(x_ref, sc_ref, sh_ref, *refs, attn_layouts):
    n = len(IN_GROUPS)
    w_refs, o_refs = refs[:n], refs[n:]
    h = (x_ref[...] * (1.0 + sc_ref[...]) + sh_ref[...]).astype(BF16)
    k = 0
    for (name, width, dtypes), w_ref in zip(IN_GROUPS, w_refs):
        y = _dot(h, w_ref[...])
        for dt in dtypes:
            if (name, dt) in HEAD_MAJOR_ROWS:
                rows = y.shape[0]
                for hd in range(ATT_HEADS):
                    o_refs[k][pl.ds(hd, rows, stride=ATT_HEADS), :] = y[:, hd * ATT_DIM:(hd + 1) * ATT_DIM]
            elif attn_layouts and (name, dt) == IDX_Q_TILES:
                rows = y.shape[0]
                for hd in range(IDX_HEADS):
                    o_refs[k][hd * rows:(hd + 1) * rows, :] = y[:, hd * IDX_DIM:(hd + 1) * IDX_DIM].astype(dt)
            elif attn_layouts and (name, dt) == VALUES_T:
                o_refs[k][...] = y.T.astype(dt)
            elif (name, dt) == IDX_K_ONLY:
                o_refs[k][...] = y[:, SM_IK:SM_IK + IDX_DIM].astype(dt)
            else:
                o_refs[k][...] = y.astype(dt)
            k += 1


def _in_proj(x, sc, sh, w_groups, tm, ck=None):
    n = x.shape[0]
    nb, r, _ = sc.shape
    nt = n // tm
    tiles_per_b = nt // nb
    attn_layouts = ck is not None
    tiles_per_c = ck // tm if attn_layouts else 1
    mod_spec = pl.BlockSpec((None, r, D_MODEL), lambda i: (i // tiles_per_b, 0, 0))
    in_specs = [pl.BlockSpec((tm, D_MODEL), lambda i: (i, 0)), mod_spec, mod_spec]
    for (name, width, _), w in zip(IN_GROUPS, w_groups):
        in_specs.append(pl.BlockSpec((D_MODEL, width), lambda i: (0, 0), pipeline_mode=pl.Buffered(1)))
    out_specs, out_shape = [], []
    for name, width, dtypes in IN_GROUPS:
        for dt in dtypes:
            if (name, dt) in HEAD_MAJOR_ROWS:
                out_specs.append(pl.BlockSpec((tm * ATT_HEADS, ATT_DIM), lambda i: (i, 0)))
                out_shape.append(jax.ShapeDtypeStruct((n * ATT_HEADS, ATT_DIM), dt))
            elif attn_layouts and (name, dt) == IDX_Q_TILES:
                out_specs.append(pl.BlockSpec((tm * IDX_HEADS, IDX_DIM), lambda i: (i, 0)))
                out_shape.append(jax.ShapeDtypeStruct((n * IDX_HEADS, IDX_DIM), dt))
            elif attn_layouts and (name, dt) == VALUES_T:
                out_specs.append(pl.BlockSpec(
                    (None, None, width, tm),
                    lambda i: (i // tiles_per_b, (i % tiles_per_b) // tiles_per_c, 0, i % tiles_per_c)))
                out_shape.append(jax.ShapeDtypeStruct((nb, n // nb // ck, width, ck), dt))
            elif (name, dt) == IDX_K_ONLY:
                out_specs.append(pl.BlockSpec((tm, IDX_DIM), lambda i: (i, 0)))
                out_shape.append(jax.ShapeDtypeStruct((n, IDX_DIM), dt))
            else:
                out_specs.append(pl.BlockSpec((tm, width), lambda i: (i, 0)))
                out_shape.append(jax.ShapeDtypeStruct((n, width), dt))
    outs = pl.pallas_call(
        functools.partial(_in_kernel, attn_layouts=attn_layouts),
        grid=(nt,),
        in_specs=in_specs,
        out_specs=out_specs,
        out_shape=out_shape,
        compiler_params=_cparams(("parallel",)),
        name="in_proj",
    )(x, sc, sh, *w_groups)
    names = []
    for name, _, dtypes in IN_GROUPS:
        for dt in dtypes:
            names.append(name if dt == dtypes[0] else name + "16")
    return dict(zip(names, outs))


def _delta_kernel(qkv_ref, z_ref, sm_ref, cw_ref, alog_ref, dt_ref, ng_ref, o_ref, s_out_ref, xs, s_sc, *, tt):
    i = pl.program_id(1)
    nc = tt // DN_CHUNK
    c = DN_CHUNK

    @pl.when(i == 0)
    def _():
        xs[0:8, :] = jnp.zeros((8, 3 * DN_WIDTH), F32)
        s_sc[...] = jnp.zeros_like(s_sc)

    xs[8:8 + tt, :] = qkv_ref[...]
    cw = cw_ref[...]
    y = xs[8:8 + tt, :] * cw[3:4, :]
    for j in range(1, CONV_W):
        y = y + xs[8 - j:8 - j + tt, :] * cw[3 - j:4 - j, :]
    act = _silu(y)
    xs[0:8, :] = xs[tt:tt + 8, :]

    sm = sm_ref[...]
    beta_all = _sigmoid(sm)
    g_all = -jnp.exp(alog_ref[...]) * _softplus(sm + dt_ref[...])
    r_io = lax.broadcasted_iota(I32, (tt, tt), 0)
    c_io = lax.broadcasted_iota(I32, (tt, tt), 1)
    lblk = jnp.where((r_io // c == c_io // c) & (r_io >= c_io), 1.0, 0.0).astype(F32)
    gcum = _dot(lblk, g_all, HI)
    gcum_t = gcum.T

    ri = lax.broadcasted_iota(I32, (c, c), 0)
    ci = lax.broadcasted_iota(I32, (c, c), 1)
    incl = ri >= ci
    strict = ri > ci
    eye = jnp.where(ri == ci, 1.0, 0.0).astype(F32)
    ng = ng_ref[...]

    blocks = [(h, cc) for h in range(DN_HEADS) for cc in range(nc)]
    qs, ks, gis, egs, dmats, qks, nms, rhss = {}, {}, {}, {}, {}, {}, {}, {}
    for h in range(DN_HEADS):
        lo, hi = h * DN_DIM, (h + 1) * DN_DIM
        q_h = act[:, lo:hi]
        k_h = act[:, DN_WIDTH + lo:DN_WIDTH + hi]
        v_h = act[:, 2 * DN_WIDTH + lo:2 * DN_WIDTH + hi]
        q_h = q_h * lax.rsqrt(jnp.sum(q_h * q_h, axis=-1, keepdims=True) + 1e-6) * (DN_DIM ** -0.5)
        k_h = k_h * lax.rsqrt(jnp.sum(k_h * k_h, axis=-1, keepdims=True) + 1e-6)
        for cc in range(nc):
            r0, r1 = cc * c, (cc + 1) * c
            qc, kc, vc = q_h[r0:r1], k_h[r0:r1], v_h[r0:r1]
            gi = gcum[r0:r1, SM_A + h:SM_A + h + 1]
            gj = gcum_t[SM_A + h:SM_A + h + 1, r0:r1]
            bi = beta_all[r0:r1, SM_B + h:SM_B + h + 1]
            dmat = jnp.exp(jnp.where(incl, gi - gj, -jnp.inf))
            eg = jnp.exp(gi)
            key = (h, cc)
            qs[key], ks[key], gis[key], egs[key], dmats[key] = qc, kc, gi, eg, dmat
            qks[key] = _dot_nt(qc, kc) * dmat
            nms[key] = jnp.where(strict, -(bi * _dot_nt(kc, kc) * dmat), 0.0)
            rhss[key] = jnp.concatenate([vc * bi, kc * (bi * eg)], axis=1)

    tms = {key: eye + nms[key] for key in blocks}
    ps = {key: _split(nms[key]) for key in blocks}
    for _ in range(5):
        for key in blocks:
            p2 = _dot3(ps[key], ps[key])
            ps[key] = _split(p2)
            tms[key] = tms[key] + _dot3(_split(tms[key]), ps[key])
    sols = {key: _dot3(_split(tms[key]), _split(rhss[key])) for key in blocks}

    states = [s_sc[h] for h in range(DN_HEADS)]
    for cc in range(nc):
        r0, r1 = cc * c, (cc + 1) * c
        for h in range(DN_HEADS):
            lo, hi = h * DN_DIM, (h + 1) * DN_DIM
            key = (h, cc)
            s_h = states[h]
            gi, eg = gis[key], egs[key]
            u, w = sols[key][:, :DN_DIM], sols[key][:, DN_DIM:]
            v_new = u - _dot(w, s_h)
            o = _dot(qs[key] * eg, s_h) + _dot(qks[key], v_new)
            g_last = gi[c - 1:c, :]
            states[h] = s_h * jnp.exp(g_last) + _dot_tn(ks[key] * jnp.exp(g_last - gi), v_new)
            o = o * lax.rsqrt(jnp.mean(o * o, axis=-1, keepdims=True) + EPS) * ng
            o_ref[r0:r1, lo:hi] = o * _silu(z_ref[r0:r1, lo:hi])
    for h in range(DN_HEADS):
        s_sc[h] = states[h]

    @pl.when(i == pl.num_programs(1) - 1)
    def _():
        s_out_ref[...] = s_sc[...]


def _delta_prompt(qkv, z, small, conv_w, alog_row, dt_row, norm_g, nb, tt=256):
    n = qkv.shape[0]
    nt = n // nb // tt
    row = lambda b, i: (b * nt + i, 0)
    const = lambda b, i: (0, 0)
    return pl.pallas_call(
        functools.partial(_delta_kernel, tt=tt),
        grid=(nb, nt),
        in_specs=[
            pl.BlockSpec((tt, 3 * DN_WIDTH), row),
            pl.BlockSpec((tt, DN_WIDTH), row),
            pl.BlockSpec((tt, 128), row),
            pl.BlockSpec((CONV_W, 3 * DN_WIDTH), const),
            pl.BlockSpec((1, 128), const),
            pl.BlockSpec((1, 128), const),
            pl.BlockSpec((1, DN_DIM), const),
        ],
        out_specs=[
            pl.BlockSpec((tt, DN_WIDTH), row),
            pl.BlockSpec((None, DN_HEADS, DN_DIM, DN_DIM), lambda b, i: (b, 0, 0, 0)),
        ],
        out_shape=[
            jax.ShapeDtypeStruct((n, DN_WIDTH), F32),
            jax.ShapeDtypeStruct((nb, DN_HEADS, DN_DIM, DN_DIM), F32),
        ],
        scratch_shapes=[
            pltpu.VMEM((tt + 8, 3 * DN_WIDTH), F32),
            pltpu.VMEM((DN_HEADS, DN_DIM, DN_DIM), F32),
        ],
        compiler_params=_cparams(("arbitrary", "arbitrary")),
        name="delta_prompt",
    )(qkv, z, small, conv_w, alog_row, dt_row, norm_g)


def _sdelta_kernel(qkv_ref, buf_ref, z_ref, sm_ref, cw_ref, alog_ref, dt_ref, ng_ref, s_ref, o_ref, s_out_ref, *, rows):
    cw = cw_ref[...]
    y = qkv_ref[...] * cw[3:4, :]
    for j in range(CONV_W - 1):
        y = y + buf_ref[:, j, :] * cw[j:j + 1, :]
    act = _silu(y)
    sm = sm_ref[...]
    beta_all = _sigmoid(sm)
    g_all = -jnp.exp(alog_ref[...]) * _softplus(sm + dt_ref[...])
    ng = ng_ref[...]
    sub = lax.broadcasted_iota(I32, (8, DN_DIM), 0)
    for h in range(DN_HEADS):
        lo, hi = h * DN_DIM, (h + 1) * DN_DIM
        q_h = act[:, lo:hi]
        k_h = act[:, DN_WIDTH + lo:DN_WIDTH + hi]
        v_h = act[:, 2 * DN_WIDTH + lo:2 * DN_WIDTH + hi]
        q_h = q_h * lax.rsqrt(jnp.sum(q_h * q_h, axis=-1, keepdims=True) + 1e-6) * (DN_DIM ** -0.5)
        k_h = k_h * lax.rsqrt(jnp.sum(k_h * k_h, axis=-1, keepdims=True) + 1e-6)
        g = g_all[:, SM_A + h:SM_A + h + 1]
        b = beta_all[:, SM_B + h:SM_B + h + 1]
        eg = jnp.exp(g)
        u = v_h * b
        w = k_h * (b * eg)
        qe = q_h * eg
        qk = jnp.sum(q_h * k_h, axis=-1, keepdims=True)
        z_h = z_ref[:, lo:hi]
        for r in range(rows):
            s = s_ref[r, h]
            w8 = jnp.where(sub == 0, w[r:r + 1], 0.0)
            q8 = jnp.where(sub == 0, qe[r:r + 1], 0.0)
            k8 = jnp.where(sub == 0, k_h[r:r + 1], 0.0)
            v_new = u[r:r + 1] - _dot(w8, s, HI)[0:1]
            o = _dot(q8, s, HI)[0:1] + qk[r:r + 1] * v_new
            vn8 = jnp.where(sub == 0, v_new, 0.0)
            s_out_ref[r, h] = s * eg[r:r + 1] + _dot_tn(k8, vn8, HI)
            o = o * lax.rsqrt(jnp.mean(o * o, axis=-1, keepdims=True) + EPS) * ng
            o_ref[r:r + 1, lo:hi] = o * _silu(z_h[r:r + 1])


def _delta_sample(qkv, buf, z, small, conv_w, alog_row, dt_row, norm_g, layer, state, rows=8):
    n = qkv.shape[0]
    row = lambda i: (i, 0)
    const = lambda i: (0, 0)
    return pl.pallas_call(
        functools.partial(_sdelta_kernel, rows=rows),
        grid=(n // rows,),
        in_specs=[
            pl.BlockSpec((rows, 3 * DN_WIDTH), row),
            pl.BlockSpec((rows, CONV_W - 1, 3 * DN_WIDTH), lambda i: (i, 0, 0)),
            pl.BlockSpec((rows, DN_WIDTH), row),
            pl.BlockSpec((rows, 128), row),
            pl.BlockSpec((CONV_W, 3 * DN_WIDTH), const),
            pl.BlockSpec((1, 128), const),
            pl.BlockSpec((1, 128), const),
            pl.BlockSpec((1, DN_DIM), const),
            pl.BlockSpec((None, rows, DN_HEADS, DN_DIM, DN_DIM), lambda i: (layer, i, 0, 0, 0)),
        ],
        out_specs=[
            pl.BlockSpec((rows, DN_WIDTH), row),
            pl.BlockSpec((rows, DN_HEADS, DN_DIM, DN_DIM), lambda i: (i, 0, 0, 0)),
        ],
        out_shape=[
            jax.ShapeDtypeStruct((n, DN_WIDTH), F32),
            jax.ShapeDtypeStruct(state.shape[1:], F32),
        ],
        compiler_params=_cparams(("parallel",)),
        name="delta_sample",
    )(qkv, buf, z, small, conv_w, alog_row, dt_row, norm_g, state)


def _score_keys(s):
    s = jnp.where(s == 0.0, 0.0, s)
    k = pltpu.bitcast(s, I32)
    return jnp.where(k < 0, k ^ jnp.int32(0x7FFFFFFF), k)


SEARCH_ROUND = 4


def _alibi_slope(h):
    return 2.0 ** (-8.0 * (h + 1) / ATT_HEADS)


def _fold_groups(x, reduce_fn, ways=8):
    g = x.shape[0]
    if g % ways or g <= ways:
        return reduce_fn(x, axis=0)
    return reduce_fn(reduce_fn(x.reshape(g // ways, ways, 8, x.shape[-1]), axis=0), axis=0)


def _sublane_allreduce(x, op):
    for shift in (4, 2, 1):
        x = op(x, pltpu.roll(x, shift, axis=0))
    return x


def _attn_kernel(q_ref, iq_ref, sm_ref, k_ref, vt_ref, ik_ref, o_ref, keys, alibi, acc_sc, *, tq, ck):
    i = pl.program_id(1)
    nk = (i * tq + tq + ck - 1) // ck
    g8 = ck // 8
    t = i * tq + lax.broadcasted_iota(I32, (8, tq), 1)
    sub = lax.broadcasted_iota(I32, (ck, 1), 0)
    sub3 = lax.broadcasted_iota(I32, (g8, 8, 1), 0) * 8 + lax.broadcasted_iota(I32, (g8, 8, 1), 1)

    @pl.when((pl.program_id(0) == 0) & (i == 0))
    def _():
        for h in range(ATT_HEADS):
            alibi[h] = jnp.broadcast_to(sub.astype(F32) * (_alibi_slope(h) * LOG2E), (ck, tq))

    sm_t = sm_ref[...].T
    iw = [jnp.broadcast_to(sm_t[SM_IW + h:SM_IW + h + 1, :] * (IDX_HEADS ** -0.5 * IDX_DIM ** -0.5), (8, tq))
          for h in range(IDX_HEADS)]
    iq = iq_ref[...]

    def score_chunk(c, diagonal):
        off = pl.multiple_of(c * ck, ck)
        d = _dot_nt(ik_ref[pl.ds(off, ck), :], iq)
        s = jnp.maximum(d[:, 0:tq].reshape(g8, 8, tq), 0.0) * iw[0][None]
        for h in range(1, IDX_HEADS):
            s = s + jnp.maximum(d[:, h * tq:(h + 1) * tq].reshape(g8, 8, tq), 0.0) * iw[h][None]
        if diagonal:
            s = jnp.where(off + sub3 <= t[None], s, -jnp.inf)
        keys[c] = _score_keys(s)

    def score_full(c, carry):
        score_chunk(c, False)
        return carry

    lax.fori_loop(0, nk - 1, score_full, 0)
    score_chunk(nk - 1, True)

    kth = jnp.minimum(t + 1, TOPK)

    def count(pred_fn):
        def body(c, cnt):
            return cnt + _fold_groups(jnp.where(pred_fn(keys[c], c), 1, 0).astype(I32), jnp.sum)
        return _sublane_allreduce(lax.fori_loop(0, nk, body, jnp.zeros((8, tq), I32)), jnp.add)

    def search_cond(st):
        return (st[0] < 32) & (st[3] > 0)

    def search_round(st):
        it, thr, cnt_thr, _ = st
        for u in range(SEARCH_ROUND):
            cand = thr ^ jnp.left_shift(jnp.int32(1), 31 - u - it)
            cnt = count(lambda kc, c: kc >= cand[None])
            ok = cnt >= kth
            thr = jnp.where(ok, cand, thr)
            cnt_thr = jnp.where(ok, cnt, cnt_thr)
        return it + SEARCH_ROUND, thr, cnt_thr, jnp.max(jnp.where(cnt_thr != kth, 1, 0))

    _, thr, _, unresolved = lax.while_loop(
        search_cond, search_round,
        (jnp.int32(0), jnp.full((8, tq), -2 ** 31, I32), jnp.full((8, tq), 2 ** 30, I32), jnp.int32(1)))

    @pl.when(unresolved > 0)
    def _():
        need = kth - count(lambda kc, c: kc > thr[None])

        def pos_step(it, cut):
            cand = cut | jnp.left_shift(jnp.int32(1), 13 - it)
            cnt = count(lambda kc, c: (kc == thr[None]) & (c * ck + sub3 < cand[None]))
            return jnp.where(cnt < need, cand, cut)
        cut = lax.fori_loop(0, 14, pos_step, jnp.zeros((8, tq), I32))

        def drop(c, carry):
            kc = keys[c]
            keys[c] = jnp.where((kc == thr[None]) & (c * ck + sub3 > cut[None]), jnp.int32(-2 ** 31), kc)
            return carry
        lax.fori_loop(0, nk, drop, 0)

    acc_sc[...] = jnp.zeros_like(acc_sc)
    q = q_ref[...]
    qh = [q[:, h * ATT_DIM:(h + 1) * ATT_DIM] for h in range(ATT_HEADS)]
    heads = range(ATT_HEADS)

    def attend_chunk(c, carry):
        ms, ls = carry
        off = pl.multiple_of(c * ck, ck)
        selb = jnp.where(keys[c] >= thr[None], 0.0, NEG).reshape(ck, tq)
        kc = k_ref[pl.ds(off, ck), :]
        off_f = off.astype(F32)
        c_off = [off_f * (_alibi_slope(h) * LOG2E) for h in heads]
        zs = [(_dot_nt(kc[:, h * ATT_DIM:(h + 1) * ATT_DIM], qh[h]) * (ATT_DIM ** -0.5 * LOG2E)
               + (alibi[h] + selb)).reshape(g8, 8, tq) for h in heads]
        m_new = [jnp.maximum(ms[h], _sublane_allreduce(_fold_groups(zs[h], jnp.max), jnp.maximum) + c_off[h])
                 for h in heads]
        a = [jnp.exp2(ms[h] - m_new[h]) for h in heads]
        ps = [jnp.exp2(zs[h] - (m_new[h] - c_off[h])[None]) for h in heads]
        l_new = [a[h] * ls[h] + _sublane_allreduce(_fold_groups(ps[h], jnp.sum), jnp.add) for h in heads]
        for h in heads:
            pv = _dot(vt_ref[c, h * ATT_DIM:(h + 1) * ATT_DIM, :], ps[h].reshape(ck, tq).astype(BF16))
            acc_sc[h] = a[h][None] * acc_sc[h] + pv.reshape(ATT_DIM // 8, 8, tq)
        return tuple(m_new), tuple(l_new)

    init = (tuple(jnp.full((8, tq), NEG, F32) for _ in heads), tuple(jnp.zeros((8, tq), F32) for _ in heads))
    _, ls = lax.fori_loop(0, nk, attend_chunk, init)
    for h in heads:
        o_ref[:, h * ATT_DIM:(h + 1) * ATT_DIM] = (acc_sc[h] / ls[h][None]).reshape(ATT_DIM, tq).T


def _attn_prompt(q16, iq16, small, k16, vt16, ik16, nb, tq=128, ck=512):
    n = q16.shape[0]
    s = n // nb
    nq = s // tq
    row = lambda b, i: (b * nq + i, 0)
    full = lambda b, i: (b, 0)
    once = pl.Buffered(1)
    return pl.pallas_call(
        functools.partial(_attn_kernel, tq=tq, ck=ck),
        grid=(nb, nq),
        in_specs=[
            pl.BlockSpec((tq, ATT_WIDTH), row),
            pl.BlockSpec((IDX_HEADS * tq, IDX_DIM), row),
            pl.BlockSpec((tq, 128), row),
            pl.BlockSpec((s, ATT_WIDTH), full, pipeline_mode=once),
            pl.BlockSpec((None, s // ck, ATT_WIDTH, ck), lambda b, i: (b, 0, 0, 0), pipeline_mode=once),
            pl.BlockSpec((s, IDX_DIM), full, pipeline_mode=once),
        ],
        out_specs=pl.BlockSpec((tq, ATT_WIDTH), row),
        out_shape=jax.ShapeDtypeStruct((n, ATT_WIDTH), F32),
        scratch_shapes=[
            pltpu.VMEM((s // ck, ck // 8, 8, tq), I32),
            pltpu.VMEM((ATT_HEADS, ck, tq), F32),
            pltpu.VMEM((ATT_HEADS, ATT_DIM // 8, 8, tq), F32),
        ],
        compiler_params=_cparams(("arbitrary", "arbitrary")),
        name="attn_prompt",
    )(q16, iq16, small, k16, vt16, ik16)


ATTEND_PAGES = 16
SCORE_PAGES = 32


def _sscore_kernel(pt_ref, iq_ref, iw_ref, ikn_ref, *refs, pgs):
    pages, o_ref = refs[:pgs], refs[pgs]
    j = pl.program_id(1)
    iq = iq_ref[...]
    iw = iw_ref[...] * (IDX_HEADS ** -0.5)

    @pl.when(j < pl.num_programs(1) - 1)
    def _():
        for u in range(pgs):
            d = _dot(iq, pages[u][...]) * (IDX_DIM ** -0.5)
            o_ref[:, u * PAGE:(u + 1) * PAGE] = jnp.sum(jnp.maximum(d, 0.0) * iw, axis=0, keepdims=True)

    @pl.when(j == pl.num_programs(1) - 1)
    def _():
        d = _dot_nt(iq, jnp.broadcast_to(ikn_ref[...], (8, IDX_DIM)))[:, 0:1] * (IDX_DIM ** -0.5)
        s_new = jnp.sum(jnp.maximum(d, 0.0) * iw, axis=0, keepdims=True)
        lane = lax.broadcasted_iota(I32, (1, pgs * PAGE), 1)
        o_ref[...] = jnp.where(lane == 0, s_new, -jnp.inf)


def _sample_scores(layer, page_table, iq3, iw3, ik_new3, cache_idx_t):
    nbatch, n_pages = page_table.shape
    pgs = min(SCORE_PAGES, n_pages)
    nj = n_pages // pgs

    def page_map(u):
        return lambda b, j, pt: (layer, pt[b, jnp.minimum(j * pgs + u, n_pages - 1)], 0, 0)

    grid_spec = pltpu.PrefetchScalarGridSpec(
        num_scalar_prefetch=1,
        grid=(nbatch, nj + 1),
        in_specs=[
            pl.BlockSpec((None, IDX_HEADS, IDX_DIM), lambda b, j, pt: (b, 0, 0)),
            pl.BlockSpec((None, IDX_HEADS, 1), lambda b, j, pt: (b, 0, 0)),
            pl.BlockSpec((None, 1, IDX_DIM), lambda b, j, pt: (b, 0, 0)),
        ] + [pl.BlockSpec((None, None, IDX_DIM, PAGE), page_map(u)) for u in range(pgs)],
        out_specs=pl.BlockSpec((None, 1, pgs * PAGE), lambda b, j, pt: (b, 0, j)),
    )
    return pl.pallas_call(
        functools.partial(_sscore_kernel, pgs=pgs),
        grid_spec=grid_spec,
        out_shape=jax.ShapeDtypeStruct((nbatch, 1, (nj + 1) * pgs * PAGE), F32),
        compiler_params=_cparams(("arbitrary", "arbitrary")),
        name="sample_scores",
    )(page_table, iq3, iw3, ik_new3, *([cache_idx_t] * pgs))


def _sselect_kernel(s_ref, o_ref):
    keys = _score_keys(s_ref[...])
    rows, width = keys.shape
    pos = lax.broadcasted_iota(I32, (1, width), 1)

    def cnt(m):
        return jnp.sum(jnp.where(m, 1, 0).astype(I32), axis=-1, keepdims=True)

    def search_step(it, thr):
        cand = thr ^ jnp.left_shift(jnp.int32(1), 31 - it)
        return jnp.where(cnt(keys >= cand) >= TOPK, cand, thr)

    thr = lax.fori_loop(0, 32, search_step, jnp.full((rows, 1), -2 ** 31, I32))
    need = TOPK - cnt(keys > thr)
    eq = keys == thr

    def pos_step(it, cut):
        cand = cut | jnp.left_shift(jnp.int32(1), 14 - it)
        return jnp.where(cnt(eq & (pos < cand)) < need, cand, cut)

    cut = lax.fori_loop(0, 15, pos_step, jnp.zeros((rows, 1), I32))
    sel = (keys > thr) | (eq & (pos <= cut))
    o_ref[...] = jnp.where(sel, 0.0, NEG)


def _sample_select(scores):
    return pl.pallas_call(
        _sselect_kernel,
        out_shape=jax.ShapeDtypeStruct(scores.shape, F32),
        compiler_params=pltpu.CompilerParams(vmem_limit_bytes=VMEM_LIMIT),
        name="sample_select",
    )(scores)


PAGE_ROWS = PAGE * ATT_HEADS


def _sattn_kernel(pt_ref, q_ref, kn_ref, vn_ref, selb_ref, seln_ref, *refs, past, pg):
    kp, vp = refs[:pg], refs[pg:2 * pg]
    o_ref, lg_sc, l_sc, pn_sc, acc_sc = refs[2 * pg:2 * pg + 5]
    ph = pl.program_id(1)
    j = pl.program_id(2)
    scale = ATT_DIM ** -0.5
    q8 = q_ref[...]
    row = lax.broadcasted_iota(I32, (8, 1), 0)
    lane = lax.broadcasted_iota(I32, (1, PAGE_ROWS), 1)
    slope = jnp.exp2(-8.0 * (row + 1).astype(F32) / ATT_HEADS)
    own = jnp.bitwise_and(lane, ATT_HEADS - 1) == row
    tok = jnp.right_shift(lane, 2)

    @pl.when(ph == 0)
    def _():
        for u in range(pg):
            dist = (past - ((j * pg + u) * PAGE + tok)).astype(F32)
            lg = _dot_nt(q8, kp[u][...]) * scale - slope * dist + selb_ref[:, u * PAGE_ROWS:(u + 1) * PAGE_ROWS]
            lg_sc[j * pg + u] = jnp.where(own, lg, NEG)

    @pl.when((ph == 1) & (j == 0))
    def _():
        allg = lg_sc[...]
        lgn = jnp.sum(q8 * kn_ref[...], axis=-1, keepdims=True) * scale + seln_ref[:, 0:1]
        m = jnp.maximum(jnp.max(jnp.max(allg, axis=0), axis=-1, keepdims=True), lgn)
        p = jnp.exp(allg - m)
        lg_sc[...] = p
        pn = jnp.exp(lgn - m)
        l_sc[...] = jnp.sum(jnp.sum(p, axis=0), axis=-1, keepdims=True) + pn
        pn_sc[...] = pn
        acc_sc[...] = jnp.zeros_like(acc_sc)

    @pl.when(ph == 1)
    def _():
        acc = acc_sc[...]
        for u in range(pg):
            acc = acc + _dot(lg_sc[j * pg + u], vp[u][...])
        acc_sc[...] = acc

    @pl.when((ph == 1) & (j == pl.num_programs(2) - 1))
    def _():
        o_ref[...] = (acc_sc[...] + pn_sc[...] * vn_ref[...]) / l_sc[...]


def _sample_attend(layer, page_table, q8, kn8, vn8, selb4, selb, cache_k4, cache_v4):
    nbatch, n_pages = page_table.shape
    pg = min(ATTEND_PAGES, n_pages)
    nj = n_pages // pg
    past = n_pages * PAGE

    def k_map(u):
        return lambda b, ph, j, pt: (layer, pt[b, jnp.where(ph == 0, j, nj - 1) * pg + u], 0, 0)

    def v_map(u):
        return lambda b, ph, j, pt: (layer, pt[b, jnp.where(ph == 0, 0, j) * pg + u], 0, 0)

    vec = pl.BlockSpec((None, 8, ATT_DIM), lambda b, ph, j, pt: (b, 0, 0))
    grid_spec = pltpu.PrefetchScalarGridSpec(
        num_scalar_prefetch=1,
        grid=(nbatch, 2, nj),
        in_specs=[
            vec, vec, vec,
            pl.BlockSpec((None, 1, pg * PAGE_ROWS), lambda b, ph, j, pt: (b, 0, jnp.where(ph == 0, j, nj - 1))),
            pl.BlockSpec((None, 1, 128), lambda b, ph, j, pt: (b, 0, past // 128)),
        ] + [pl.BlockSpec((None, None, PAGE_ROWS, ATT_DIM), k_map(u)) for u in range(pg)]
          + [pl.BlockSpec((None, None, PAGE_ROWS, ATT_DIM), v_map(u)) for u in range(pg)],
        out_specs=vec,
        scratch_shapes=[
            pltpu.VMEM((n_pages, 8, PAGE_ROWS), F32),
            pltpu.VMEM((8, 1), F32),
            pltpu.VMEM((8, 1), F32),
            pltpu.VMEM((8, ATT_DIM), F32),
        ],
    )
    return pl.pallas_call(
        functools.partial(_sattn_kernel, past=past, pg=pg),
        grid_spec=grid_spec,
        out_shape=jax.ShapeDtypeStruct((nbatch, 8, ATT_DIM), F32),
        compiler_params=_cparams(("arbitrary", "arbitrary", "arbitrary")),
        name="sample_attend",
    )(page_table, q8, kn8, vn8, selb4, selb, *([cache_k4] * pg), *([cache_v4] * pg))


def _mix_kernel(oa_ref, ob_ref, gates_ref, x_ref, gm_ref, wa_ref, wb_ref, wo_ref, g_ref, b_ref, o_ref):
    br_a = _dot(oa_ref[...].astype(BF16), wa_ref[...])
    br_b = _dot(ob_ref[...].astype(BF16), wb_ref[...])
    merged = _sigmoid(gates_ref[:, :D_MODEL]) * br_a + _sigmoid(gates_ref[:, D_MODEL:]) * br_b
    y = _dot(merged.astype(BF16), wo_ref[...])
    r = ALPHA * x_ref[...] + gm_ref[...] * y
    o_ref[...] = _layernorm(r, g_ref[...], b_ref[...])


def _mix(oa, ob, gates, x, gm, wa, wb, wo, ln_g, ln_b, tm):
    n = x.shape[0]
    nb, r, _ = gm.shape
    nt = n // tm
    tiles_per_b = nt // nb
    row = lambda i: (i, 0)
    const = lambda i: (0, 0)
    return pl.pallas_call(
        _mix_kernel,
        grid=(nt,),
        in_specs=[
            pl.BlockSpec((tm, DN_WIDTH), row),
            pl.BlockSpec((tm, ATT_WIDTH), row),
            pl.BlockSpec((tm, 2 * D_MODEL), row),
            pl.BlockSpec((tm, D_MODEL), row),
            pl.BlockSpec((None, r, D_MODEL), lambda i: (i // tiles_per_b, 0, 0)),
            pl.BlockSpec((DN_WIDTH, D_MODEL), const),
            pl.BlockSpec((ATT_WIDTH, D_MODEL), const),
            pl.BlockSpec((D_MODEL, D_MODEL), const),
            pl.BlockSpec((1, D_MODEL), const),
            pl.BlockSpec((1, D_MODEL), const),
        ],
        out_specs=pl.BlockSpec((tm, D_MODEL), row),
        out_shape=jax.ShapeDtypeStruct((n, D_MODEL), F32),
        compiler_params=_cparams(("parallel",)),
        name="mix_out",
    )(oa, ob, gates, x, gm, wa, wb, wo, ln_g, ln_b)


def _route_kernel(x_ref, sc_ref, sh_ref, wr_ref, o_ref, e_ref, g_ref):
    h = x_ref[...] * (1.0 + sc_ref[...]) + sh_ref[...]
    o_ref[...] = h.astype(o_ref.dtype)
    logits = _dot(h, wr_ref[...], HI)
    lane = lax.broadcasted_iota(I32, logits.shape, 1)
    logits = jnp.where(lane < N_EXPERTS, logits, -jnp.inf)
    e = jnp.exp(logits - jnp.max(logits, axis=-1, keepdims=True))
    p = e / jnp.sum(e, axis=-1, keepdims=True)
    p1 = jnp.max(p, axis=-1, keepdims=True)
    e1 = jnp.min(jnp.where(p == p1, lane, 128), axis=-1, keepdims=True)
    rest = jnp.where((lane == e1) | (lane >= N_EXPERTS), -1.0, p)
    p2 = jnp.max(rest, axis=-1, keepdims=True)
    e2 = jnp.min(jnp.where(rest == p2, lane, 128), axis=-1, keepdims=True)
    tot = p1 + p2
    e_ref[...] = jnp.concatenate([e1, e2], axis=1)
    g_ref[...] = jnp.concatenate([p1 / tot, p2 / tot], axis=1)


def _ffn_route(x, sc, sh, tm, w_router):
    n = x.shape[0]
    nb, r, _ = sc.shape
    nt = n // tm
    tiles_per_b = nt // nb
    row = lambda i: (i, 0)
    mod_spec = pl.BlockSpec((None, r, D_MODEL), lambda i: (i // tiles_per_b, 0, 0))
    in_specs = [pl.BlockSpec((tm, D_MODEL), row), mod_spec, mod_spec]
    wr = jnp.pad(w_router, ((0, 0), (0, 128 - N_EXPERTS)))
    return pl.pallas_call(
        _route_kernel, grid=(nt,),
        in_specs=in_specs + [pl.BlockSpec((D_MODEL, 128), lambda i: (0, 0))],
        out_specs=[pl.BlockSpec((tm, D_MODEL), row), pl.BlockSpec((tm, 2), row), pl.BlockSpec((tm, 2), row)],
        out_shape=[jax.ShapeDtypeStruct((n, D_MODEL), F32), jax.ShapeDtypeStruct((n, 2), I32),
                   jax.ShapeDtypeStruct((n, 2), F32)],
        compiler_params=_cparams(("parallel",)), name="ffn_route",
    )(x, sc, sh, wr)


def _gffn_kernel(te_ref, nu_ref, x_ref, *refs, modulate, substitute):
    if modulate:
        sc_ref, sh_ref = refs[:2]
        refs = refs[2:]
    if substitute:
        src_ref, late_ref = refs[:2]
        refs = refs[2:]
    wg_ref, wu_ref, wd_ref, o_ref, acc, x16 = refs
    t = pl.program_id(0)
    f = pl.program_id(1)

    @pl.when((t < nu_ref[0]) & (f == 0))
    def _():
        x = x_ref[...]
        if modulate:
            x = x * (1.0 + sc_ref[...]) + sh_ref[...]
        x = x.astype(BF16)
        if substitute:
            src = src_ref[...]
            late = late_ref[...]
            pick = jnp.where(src == lax.broadcasted_iota(I32, (src.shape[0], late.shape[0]), 1), 1.0, 0.0)
            x = jnp.where(src >= 0, _dot(pick.astype(BF16), late).astype(BF16), x)
        x16[...] = x

    @pl.when(t < nu_ref[0])
    def _():
        x = x16[...]
        a = _dot(x, wg_ref[...].astype(BF16))
        b = _dot(x, wu_ref[...].astype(BF16))
        y = _dot((_silu(a) * b).astype(BF16), wd_ref[...].astype(BF16))

        @pl.when(f == 0)
        def _():
            acc[...] = y

        @pl.when(f > 0)
        def _():
            acc[...] += y

    @pl.when(f == pl.num_programs(1) - 1)
    def _():
        o_ref[...] = jnp.where(t < nu_ref[0], acc[...], 0.0)


def _grouped_ffn(x, tile_expert, n_used, layer, w_gate, w_up, w_down, tm, mod=None, late=None, tf=256):
    n = x.shape[0]
    nt = n // tm
    nf = D_FF // tf
    mod_specs, mod_args = [], []
    if mod is not None:
        nb, r, _ = mod[0].shape
        tiles_per_b = nt // nb
        mod_specs = [pl.BlockSpec((None, r, D_MODEL), lambda t, f, te, nu: (t // tiles_per_b, 0, 0))] * 2
        mod_args = list(mod)
    if late is not None:
        mod_specs = mod_specs + [pl.BlockSpec((tm, 1), lambda t, f, te, nu: (t, 0)),
                                 pl.BlockSpec(late[1].shape, lambda t, f, te, nu: (0, 0))]
        mod_args = mod_args + list(late)
    grid_spec = pltpu.PrefetchScalarGridSpec(
        num_scalar_prefetch=2,
        grid=(nt, nf),
        in_specs=[pl.BlockSpec((tm, D_MODEL), lambda t, f, te, nu: (t, 0))] + mod_specs + [
            pl.BlockSpec((None, None, D_MODEL, tf), lambda t, f, te, nu: (layer, te[t], 0, f)),
            pl.BlockSpec((None, None, D_MODEL, tf), lambda t, f, te, nu: (layer, te[t], 0, f)),
            pl.BlockSpec((None, None, tf, D_MODEL), lambda t, f, te, nu: (layer, te[t], f, 0)),
        ],
        out_specs=pl.BlockSpec((tm, D_MODEL), lambda t, f, te, nu: (t, 0)),
        scratch_shapes=[pltpu.VMEM((tm, D_MODEL), F32), pltpu.VMEM((tm, D_MODEL), BF16)],
    )
    return pl.pallas_call(
        functools.partial(_gffn_kernel, modulate=mod is not None, substitute=late is not None),
        grid_spec=grid_spec,
        out_shape=jax.ShapeDtypeStruct((n, D_MODEL), F32),
        compiler_params=_cparams(("arbitrary", "arbitrary")),
        name="grouped_ffn",
    )(tile_expert, n_used, x, *mod_args, w_gate, w_up, w_down)


def _res_ln_kernel(x_ref, y_ref, gf_ref, g_ref, b_ref, o_ref):
    r = ALPHA * x_ref[...] + gf_ref[...] * y_ref[...]
    o_ref[...] = _layernorm(r, g_ref[...], b_ref[...])


def _res_ln2_kernel(x_ref, y0_ref, y1_ref, gt_ref, gf_ref, g_ref, b_ref, o_ref):
    gt = gt_ref[...]
    y = y0_ref[...] * gt[:, 0:1] + y1_ref[...] * gt[:, 1:2]
    r = ALPHA * x_ref[...] + gf_ref[...] * y
    o_ref[...] = _layernorm(r, g_ref[...], b_ref[...])


def _res_ln(x, ys, gate2, gf, ln_g, ln_b, tm):
    n = x.shape[0]
    nb, r, _ = gf.shape
    nt = n // tm
    tiles_per_b = nt // nb
    row = lambda i: (i, 0)
    const = lambda i: (0, 0)
    big = pl.BlockSpec((tm, D_MODEL), row)
    mod_spec = pl.BlockSpec((None, r, D_MODEL), lambda i: (i // tiles_per_b, 0, 0))
    vec = pl.BlockSpec((1, D_MODEL), const)
    if gate2 is None:
        kern, ins, specs = _res_ln_kernel, (x, ys[0], gf, ln_g, ln_b), [big, big, mod_spec, vec, vec]
    else:
        kern = _res_ln2_kernel
        ins = (x, ys[0], ys[1], gate2, gf, ln_g, ln_b)
        specs = [big, big, big, pl.BlockSpec((tm, 2), row), mod_spec, vec, vec]
    return pl.pallas_call(
        kern, grid=(nt,), in_specs=specs, out_specs=big,
        out_shape=jax.ShapeDtypeStruct((n, D_MODEL), F32),
        compiler_params=_cparams(("parallel",)), name="res_ln",
    )(*ins)


def _dense_ffn(x, sc, sh, gf, layer, w_gate, w_up, w_down, ln_g, ln_b, tm_mod, tm_ffn):
    n = x.shape[0]
    nt = n // tm_ffn
    y = _grouped_ffn(x, jnp.zeros((nt,), I32), jnp.full((1,), nt, I32), layer,
                     w_gate[:, None], w_up[:, None], w_down[:, None], tm_ffn, mod=(sc, sh))
    return _res_ln(x, (y,), None, gf, ln_g, ln_b, tm_mod)


def _expert_ranks(top_e):
    flat_e = top_e.reshape(-1)
    onehot = flat_e[:, None] == jnp.arange(N_EXPERTS, dtype=I32)[None, :]
    seen = jnp.cumsum(onehot.astype(I32), axis=0)
    pick = lambda table: jnp.sum(jnp.where(onehot, table, 0), axis=1)
    return flat_e, seen[-1], pick(seen) - 1, pick


def _moe_ffn(main, late, w_router, layer, w_gate, w_up, w_down, ln_g, ln_b, tm_e):
    x_m, x_l = main[0], late[0]
    n_m, n_l = x_m.shape[0], x_l.shape[0]
    h_m, top_m, gate_m = _ffn_route(*main[:3], main[4], w_router)
    flat_m, counts_m, rank_m, pick_m = _expert_ranks(top_m)
    padded = (counts_m + n_l + tm_e - 1) // tm_e * tm_e
    pad_end = jnp.cumsum(padded)
    pad_start = pad_end - padded
    n_tiles = -(-(2 * n_m + N_EXPERTS * n_l) // tm_e) + N_EXPERTS
    tile_expert = jnp.minimum(
        jnp.searchsorted(pad_end, jnp.arange(n_tiles, dtype=I32) * tm_e, side="right"), N_EXPERTS - 1).astype(I32)
    n_used = (pad_end[-1] // tm_e).astype(I32).reshape(1)
    slot_m = (pick_m(pad_start[None, :]) + rank_m).reshape(n_m, 2)
    rows_m = jnp.zeros((n_tiles * tm_e,), I32).at[slot_m.reshape(-1)].set(
        jnp.arange(2 * n_m, dtype=I32) // 2, unique_indices=True)
    xb = h_m[rows_m]

    h_l, top_l, gate_l = _ffn_route(*late[:3], late[4], w_router)
    flat_l, counts_l, rank_l, pick_l = _expert_ranks(top_l)
    slot_l = (pick_l((pad_start + counts_m)[None, :]) + rank_l).reshape(n_l, 2)
    src_l = jnp.full((n_tiles * tm_e,), -1, I32).at[slot_l.reshape(-1)].set(
        jnp.arange(2 * n_l, dtype=I32) // 2, unique_indices=True).reshape(-1, 1)
    yb = _grouped_ffn(xb, tile_expert, n_used, layer, w_gate, w_up, w_down, tm_e, late=(src_l, h_l.astype(BF16)))
    out_m = _res_ln(x_m, (yb[slot_m[:, 0]], yb[slot_m[:, 1]]), gate_m, main[3], ln_g, ln_b, main[4])
    out_l = _res_ln(x_l, (yb[slot_l[:, 0]], yb[slot_l[:, 1]]), gate_l, late[3], ln_g, ln_b, late[4])
    return out_m, out_l


def kernel(x_prompt, x_sample, cache_k, cache_v, cache_idx_k, state_delta, state_conv, page_table, c_prompt, c_sample, w_ada, b_ada, w_in, conv_w, a_log, dt_bias, dn_norm_g, w_branch, w_out, ln1_g, ln1_b, ln2_g, ln2_b, w_ffn_gate, w_ffn_up, w_ffn_down, w_router, w_exp_gate, w_exp_up, w_exp_down):
    bp, seq, _ = x_prompt.shape
    bs = x_sample.shape[0]
    n_p = bp * seq
    n_phys = cache_k.shape[1]
    past = page_table.shape[1] * PAGE

    c_all = jnp.concatenate([c_prompt, c_sample], axis=0)
    c_rows = -(-c_all.shape[0] // 8) * 8
    c_all = jnp.pad(c_all, ((0, c_rows - c_all.shape[0]), (0, 0)))
    mods = _ada_all(c_all, w_ada, b_ada)

    cache_k4 = cache_k.reshape(DEPTH, n_phys, PAGE_ROWS, ATT_DIM)
    cache_v4 = cache_v.reshape(DEPTH, n_phys, PAGE_ROWS, ATT_DIM)
    cache_idx_t = jnp.swapaxes(cache_idx_k, 2, 3)
    ck = 512
    tq = 256

    xp = x_prompt.reshape(n_p, D_MODEL)
    xs = x_sample.reshape(bs, D_MODEL)
    lane = jnp.arange(128)
    outs_p, outs_s = [], []
    for l in range(DEPTH):
        j = l // 2
        mod_p = [mods[l, :bp, u * D_MODEL:(u + 1) * D_MODEL].reshape(bp, 1, D_MODEL) for u in range(6)]
        mod_s = [mods[l, bp:bp + bs, u * D_MODEL:(u + 1) * D_MODEL].reshape(1, bs, D_MODEL) for u in range(6)]
        w_groups = _split_w_in(w_in[l])
        head_lane = jnp.clip(lane - SM_A, 0, DN_HEADS - 1)
        in_a = (lane >= SM_A) & (lane < SM_A + DN_HEADS)
        alog_row = jnp.where(in_a, a_log[l][head_lane], 0.0).reshape(1, 128)
        dt_row = jnp.where(in_a, dt_bias[l][head_lane], 0.0).reshape(1, 128)
        norm_g = dn_norm_g[l].reshape(1, DN_DIM)
        wa16 = w_branch[l, 0].astype(BF16)
        wb16 = w_branch[l, 1].astype(BF16)
        wo16 = w_out[l].astype(BF16)
        ln1 = (ln1_g[l].reshape(1, -1), ln1_b[l].reshape(1, -1))
        ln2 = (ln2_g[l].reshape(1, -1), ln2_b[l].reshape(1, -1))

        pr = _in_proj(xp, mod_p[1], mod_p[0], w_groups, tm=tq, ck=ck)
        oa, s_new_p = _delta_prompt(pr["qkv_a"], pr["z"], pr["small"], conv_w[l], alog_row, dt_row, norm_g, bp)
        ob = _attn_prompt(pr["q_b"], pr["iq"], pr["small"], pr["k_b16"], pr["v_b16"], pr["small16"], bp, tq=tq, ck=ck)
        x1 = _mix(oa, ob, pr["gates"], xp, mod_p[2], wa16, wb16, wo16, *ln1, tm=512)
        conv_p = pr["qkv_a"].reshape(bp, seq, -1)[:, seq - (CONV_W - 1):, :]
        outs_p.append((conv_p, s_new_p, pr["k_b"].reshape(bp, seq, ATT_HEADS, ATT_DIM),
                       pr["v_b"].reshape(bp, seq, ATT_HEADS, ATT_DIM),
                       pr["small"][:, SM_IK:SM_IK + IDX_DIM].reshape(bp, seq, IDX_DIM)))

        sr = _in_proj(xs, mod_s[1], mod_s[0], w_groups, tm=bs)
        oa_s, s_new_s = _delta_sample(sr["qkv_a"], state_conv[l], sr["z"], sr["small"], conv_w[l], alog_row, dt_row,
                                      norm_g, l, state_delta)
        ik_s = sr["small"][:, SM_IK:SM_IK + IDX_DIM]
        iq3 = sr["iq"].astype(F32).reshape(bs, IDX_HEADS, IDX_DIM)
        iw3 = sr["small"][:, SM_IW:SM_IW + IDX_HEADS].reshape(bs, IDX_HEADS, 1)
        scores = _sample_scores(l, page_table, iq3, iw3, ik_s.reshape(bs, 1, IDX_DIM), cache_idx_t)
        selb = _sample_select(scores.reshape(bs, -1)).reshape(scores.shape)
        head_rows = lambda a: jnp.pad(a.astype(F32).reshape(bs, ATT_HEADS, ATT_DIM), ((0, 0), (0, 8 - ATT_HEADS), (0, 0)))
        selb4 = jnp.repeat(selb, ATT_HEADS, axis=-1)
        ob_s = _sample_attend(l, page_table, head_rows(sr["q_b"]), head_rows(sr["k_b"]), head_rows(sr["v_b"]),
                              selb4, selb, cache_k4, cache_v4)[:, :ATT_HEADS, :].reshape(bs, ATT_WIDTH)
        x1s = _mix(oa_s, ob_s, sr["gates"], xs, mod_s[2], wa16, wb16, wo16, *ln1, tm=bs)

        if l % 2 == 0:
            xp = _dense_ffn(x1, mod_p[4], mod_p[3], mod_p[5], j, w_ffn_gate, w_ffn_up, w_ffn_down, *ln2,
                            tm_mod=512, tm_ffn=1024)
            xs = _dense_ffn(x1s, mod_s[4], mod_s[3], mod_s[5], j, w_ffn_gate, w_ffn_up, w_ffn_down, *ln2,
                            tm_mod=bs, tm_ffn=bs)
        else:
            xp, xs = _moe_ffn((x1, mod_p[4], mod_p[3], mod_p[5], 512), (x1s, mod_s[4], mod_s[3], mod_s[5], bs),
                              w_router[j], j, w_exp_gate, w_exp_up, w_exp_down, *ln2, tm_e=1024)
        conv_s =jnp.concatenate([state_conv[l][:, 1:, :], sr["qkv_a"][:, None, :]], axis=1)
        outs_s.append((conv_s, s_new_s, sr["k_b"].reshape(bs, 1, ATT_HEADS, ATT_DIM),
                       sr["v_b"].reshape(bs, 1, ATT_HEADS, ATT_DIM), ik_s.reshape(bs, 1, IDX_DIM)))

    stack = lambda rows, u: jnp.stack([r[u] for r in rows])
    return (xp.reshape(bp, seq, D_MODEL), xs.reshape(bs, 1, D_MODEL),
            stack(outs_p, 2), stack(outs_p, 3), stack(outs_p, 4), stack(outs_p, 1), stack(outs_p, 0),
            stack(outs_s, 2), stack(outs_s, 3), stack(outs_s, 4), stack(outs_s, 1), stack(outs_s, 0))
```

```python
import functools

import jax
import jax.numpy as jnp
from jax import lax
from jax.experimental import pallas as pl
from jax.experimental.pallas import tpu as pltpu

F32 = jnp.float32
BF16 = jnp.bfloat16
I32 = jnp.int32
HI = lax.Precision.HIGHEST

D_MODEL = 1024
DEPTH = 4
PAGE = 128
DN_HEADS = 4
DN_DIM = 128
DN_WIDTH = DN_HEADS * DN_DIM
CONV_W = 4
DN_CHUNK = 64
ATT_HEADS = 4
ATT_DIM = 128
ATT_WIDTH = ATT_HEADS * ATT_DIM
IDX_HEADS = 8
IDX_DIM = 64
TOPK = 256
D_FF = 2816
N_EXPERTS = 8
ALPHA = (2.0 * DEPTH) ** 0.25
EPS = 1e-5
NEG = -1e30
LOG2E = 1.4426950408889634

SM_IK = 0
SM_B = 64
SM_A = 68
SM_IW = 72

VMEM_LIMIT = 56 * 1024 * 1024


def _cparams(sem):
    return pltpu.CompilerParams(dimension_semantics=sem, vmem_limit_bytes=VMEM_LIMIT)


def _sigmoid(x):
    return jax.nn.sigmoid(x)


def _silu(x):
    return x * jax.nn.sigmoid(x)


def _softplus(x):
    return jnp.maximum(x, 0.0) + jnp.log1p(jnp.exp(-jnp.abs(x)))


def _dot(a, b, precision=None):
    return jnp.dot(a, b, precision=precision, preferred_element_type=F32)


def _dot_nt(a, b, precision=None):
    return lax.dot_general(a, b, (((1,), (1,)), ((), ())), precision=precision, preferred_element_type=F32)


def _dot_tn(a, b, precision=None):
    return lax.dot_general(a, b, (((0,), (0,)), ((), ())), precision=precision, preferred_element_type=F32)


def _split(a):
    hi = a.astype(BF16)
    return hi, (a - hi.astype(F32)).astype(BF16)


def _dot3(a, b):
    return _dot(a[0], b[0]) + (_dot(a[0], b[1]) + _dot(a[1], b[0]))


def _layernorm(r, g, b):
    mu = jnp.mean(r, axis=-1, keepdims=True)
    rc = r - mu
    var = jnp.mean(rc * rc, axis=-1, keepdims=True)
    return rc * lax.rsqrt(var + EPS) * g + b


def _ada_kernel(c_ref, w_ref, b_ref, o_ref):
    o_ref[...] = _dot(_silu(c_ref[...]), w_ref[...], HI) + b_ref[...]


def _ada_all(c, w_ada, b_ada):
    rows = c.shape[0]
    nj = w_ada.shape[2] // D_MODEL
    return pl.pallas_call(
        _ada_kernel,
        grid=(DEPTH, nj),
        in_specs=[
            pl.BlockSpec((rows, D_MODEL), lambda l, j: (0, 0)),
            pl.BlockSpec((None, D_MODEL, D_MODEL), lambda l, j: (l, 0, j)),
            pl.BlockSpec((None, 1, D_MODEL), lambda l, j: (l, 0, j)),
        ],
        out_specs=pl.BlockSpec((None, rows, D_MODEL), lambda l, j: (l, 0, j)),
        out_shape=jax.ShapeDtypeStruct((DEPTH, rows, nj * D_MODEL), F32),
        compiler_params=_cparams(("arbitrary", "arbitrary")),
        name="ada_mod",
    )(c, w_ada, b_ada.reshape(DEPTH, 1, -1))


IN_GROUPS = (
    ("qkv_a", 3 * DN_WIDTH, (F32,)),
    ("z", DN_WIDTH, (F32,)),
    ("q_b", ATT_WIDTH, (BF16,)),
    ("k_b", ATT_WIDTH, (F32, BF16)),
    ("v_b", ATT_WIDTH, (F32, BF16)),
    ("iq", IDX_HEADS * IDX_DIM, (BF16,)),
    ("gates", 2 * D_MODEL, (F32,)),
    ("small", 128, (F32, BF16)),
)


HEAD_MAJOR_ROWS = (("k_b", F32), ("v_b", F32))
IDX_Q_TILES = ("iq", BF16)
VALUES_T = ("v_b", BF16)
IDX_K_ONLY = ("small", BF16)


def _split_w_in(w):
    o = 0
    parts = {}
    for name, width in (("qkv_a", 3 * DN_WIDTH), ("z", DN_WIDTH), ("b", DN_HEADS), ("a", DN_HEADS),
                        ("q_b", ATT_WIDTH), ("k_b", ATT_WIDTH), ("v_b", ATT_WIDTH),
                        ("iq", IDX_HEADS * IDX_DIM), ("ik", IDX_DIM), ("iw", IDX_HEADS), ("gates", 2 * D_MODEL)):
        parts[name] = w[:, o:o + width]
        o += width
    small = jnp.concatenate(
        [parts["ik"], parts["b"], parts["a"], parts["iw"],
         jnp.zeros((w.shape[0], 128 - IDX_DIM - 2 * DN_HEADS - IDX_HEADS), w.dtype)], axis=1)
    parts["small"] = small
    return [parts[name].astype(BF16) for name, _, _ in IN_GROUPS]


def _in_kernel(x_ref, sc_ref, sh_ref, *refs, attn_layouts):
    n = len(IN_GROUPS)
    w_refs, o_refs = refs[:n], refs[n:]
    h = (x_ref[...] * (1.0 + sc_ref[...]) + sh_ref[...]).astype(BF16)
    k = 0
    for (name, width, dtypes), w_ref in zip(IN_GROUPS, w_refs):
        y = _dot(h, w_ref[...])
        for dt in dtypes:
            if (name, dt) in HEAD_MAJOR_ROWS:
                rows = y.shape[0]
                for hd in range(ATT_HEADS):
                    o_refs[k][pl.ds(hd, rows, stride=ATT_HEADS), :] = y[:, hd * ATT_DIM:(hd + 1) * ATT_DIM]
            elif attn_layouts and (name, dt) == IDX_Q_TILES:
                rows = y.shape[0]
                for hd in range(IDX_HEADS):
                    o_refs[k][hd * rows:(hd + 1) * rows, :] = y[:, hd * IDX_DIM:(hd + 1) * IDX_DIM].astype(dt)
            elif attn_layouts and (name, dt) == VALUES_T:
                o_refs[k][...] = y.T.astype(dt)
            elif (name, dt) == IDX_K_ONLY:
                o_refs[k][...] = y[:, SM_IK:SM_IK + IDX_DIM].astype(dt)
            else:
                o_refs[k][...] = y.astype(dt)
            k += 1


def _in_proj(x, sc, sh, w_groups, tm, ck=None):
    n = x.shape[0]
    nb, r, _ = sc.shape
    nt = n // tm
    tiles_per_b = nt // nb
    attn_layouts = ck is not None
    tiles_per_c = ck // tm if attn_layouts else 1
    mod_spec = pl.BlockSpec((None, r, D_MODEL), lambda i: (i // tiles_per_b, 0, 0))
    in_specs = [pl.BlockSpec((tm, D_MODEL), lambda i: (i, 0)), mod_spec, mod_spec]
    for (name, width, _), w in zip(IN_GROUPS, w_groups):
        in_specs.append(pl.BlockSpec((D_MODEL, width), lambda i: (0, 0), pipeline_mode=pl.Buffered(1)))
    out_specs, out_shape = [], []
    for name, width, dtypes in IN_GROUPS:
        for dt in dtypes:
            if (name, dt) in HEAD_MAJOR_ROWS:
                out_specs.append(pl.BlockSpec((tm * ATT_HEADS, ATT_DIM), lambda i: (i, 0)))
                out_shape.append(jax.ShapeDtypeStruct((n * ATT_HEADS, ATT_DIM), dt))
            elif attn_layouts and (name, dt) == IDX_Q_TILES:
                out_specs.append(pl.BlockSpec((tm * IDX_HEADS, IDX_DIM), lambda i: (i, 0)))
                out_shape.append(jax.ShapeDtypeStruct((n * IDX_HEADS, IDX_DIM), dt))
            elif attn_layouts and (name, dt) == VALUES_T:
                out_specs.append(pl.BlockSpec(
                    (None, None, width, tm),
                    lambda i: (i // tiles_per_b, (i % tiles_per_b) // tiles_per_c, 0, i % tiles_per_c)))
                out_shape.append(jax.ShapeDtypeStruct((nb, n // nb // ck, width, ck), dt))
            elif (name, dt) == IDX_K_ONLY:
                out_specs.append(pl.BlockSpec((tm, IDX_DIM), lambda i: (i, 0)))
                out_shape.append(jax.ShapeDtypeStruct((n, IDX_DIM), dt))
            else:
                out_specs.append(pl.BlockSpec((tm, width), lambda i: (i, 0)))
                out_shape.append(jax.ShapeDtypeStruct((n, width), dt))
    outs = pl.pallas_call(
        functools.partial(_in_kernel, attn_layouts=attn_layouts),
        grid=(nt,),
        in_specs=in_specs,
        out_specs=out_specs,
        out_shape=out_shape,
        compiler_params=_cparams(("parallel",)),
        name="in_proj",
    )(x, sc, sh, *w_groups)
    names = []
    for name, _, dtypes in IN_GROUPS:
        for dt in dtypes:
            names.append(name if dt == dtypes[0] else name + "16")
    return dict(zip(names, outs))


def _delta_kernel(qkv_ref, z_ref, sm_ref, cw_ref, alog_ref, dt_ref, ng_ref, o_ref, s_out_ref, xs, s_sc, *, tt):
    i = pl.program_id(1)
    nc = tt // DN_CHUNK
    c = DN_CHUNK

    @pl.when(i == 0)
    def _():
        xs[0:8, :] = jnp.zeros((8, 3 * DN_WIDTH), F32)
        s_sc[...] = jnp.zeros_like(s_sc)

    xs[8:8 + tt, :] = qkv_ref[...]
    cw = cw_ref[...]
    y = xs[8:8 + tt, :] * cw[3:4, :]
    for j in range(1, CONV_W):
        y = y + xs[8 - j:8 - j + tt, :] * cw[3 - j:4 - j, :]
    act = _silu(y)
    xs[0:8, :] = xs[tt:tt + 8, :]

    sm = sm_ref[...]
    beta_all = _sigmoid(sm)
    g_all = -jnp.exp(alog_ref[...]) * _softplus(sm + dt_ref[...])
    r_io = lax.broadcasted_iota(I32, (tt, tt), 0)
    c_io = lax.broadcasted_iota(I32, (tt, tt), 1)
    lblk = jnp.where((r_io // c == c_io // c) & (r_io >= c_io), 1.0, 0.0).astype(F32)
    gcum = _dot(lblk, g_all, HI)
    gcum_t = gcum.T

    ri = lax.broadcasted_iota(I32, (c, c), 0)
    ci = lax.broadcasted_iota(I32, (c, c), 1)
    incl = ri >= ci
    strict = ri > ci
    eye = jnp.where(ri == ci, 1.0, 0.0).astype(F32)
    ng = ng_ref[...]

    blocks = [(h, cc) for h in range(DN_HEADS) for cc in range(nc)]
    qs, ks, gis, egs, dmats, qks, nms, rhss = {}, {}, {}, {}, {}, {}, {}, {}
    for h in range(DN_HEADS):
        lo, hi = h * DN_DIM, (h + 1) * DN_DIM
        q_h = act[:, lo:hi]
        k_h = act[:, DN_WIDTH + lo:DN_WIDTH + hi]
        v_h = act[:, 2 * DN_WIDTH + lo:2 * DN_WIDTH + hi]
        q_h = q_h * lax.rsqrt(jnp.sum(q_h * q_h, axis=-1, keepdims=True) + 1e-6) * (DN_DIM ** -0.5)
        k_h = k_h * lax.rsqrt(jnp.sum(k_h * k_h, axis=-1, keepdims=True) + 1e-6)
        for cc in range(nc):
            r0, r1 = cc * c, (cc + 1) * c
            qc, kc, vc = q_h[r0:r1], k_h[r0:r1], v_h[r0:r1]
            gi = gcum[r0:r1, SM_A + h:SM_A + h + 1]
            gj = gcum_t[SM_A + h:SM_A + h + 1, r0:r1]
            bi = beta_all[r0:r1, SM_B + h:SM_B + h + 1]
            dmat = jnp.exp(jnp.where(incl, gi - gj, -jnp.inf))
            eg = jnp.exp(gi)
            key = (h, cc)
            qs[key], ks[key], gis[key], egs[key], dmats[key] = qc, kc, gi, eg, dmat
            qks[key] = _dot_nt(qc, kc) * dmat
            nms[key] = jnp.where(strict, -(bi * _dot_nt(kc, kc) * dmat), 0.0)
            rhss[key] = jnp.concatenate([vc * bi, kc * (bi * eg)], axis=1)

    tms = {key: eye + nms[key] for key in blocks}
    ps = {key: _split(nms[key]) for key in blocks}
    for _ in range(5):
        for key in blocks:
            p2 = _dot3(ps[key], ps[key])
            ps[key] = _split(p2)
            tms[key] = tms[key] + _dot3(_split(tms[key]), ps[key])
    sols = {key: _dot3(_split(tms[key]), _split(rhss[key])) for key in blocks}

    states = [s_sc[h] for h in range(DN_HEADS)]
    for cc in range(nc):
        r0, r1 = cc * c, (cc + 1) * c
        for h in range(DN_HEADS):
            lo, hi = h * DN_DIM, (h + 1) * DN_DIM
            key = (h, cc)
            s_h = states[h]
            gi, eg = gis[key], egs[key]
            u, w = sols[key][:, :DN_DIM], sols[key][:, DN_DIM:]
            v_new = u - _dot(w, s_h)
            o = _dot(qs[key] * eg, s_h) + _dot(qks[key], v_new)
            g_last = gi[c - 1:c, :]
            states[h] = s_h * jnp.exp(g_last) + _dot_tn(ks[key] * jnp.exp(g_last - gi), v_new)
            o = o * lax.rsqrt(jnp.mean(o * o, axis=-1, keepdims=True) + EPS) * ng
            o_ref[r0:r1, lo:hi] = o * _silu(z_ref[r0:r1, lo:hi])
    for h in range(DN_HEADS):
        s_sc[h] = states[h]

    @pl.when(i == pl.num_programs(1) - 1)
    def _():
        s_out_ref[...] = s_sc[...]


def _delta_prompt(qkv, z, small, conv_w, alog_row, dt_row, norm_g, nb, tt=256):
    n = qkv.shape[0]
    nt = n // nb // tt
    row = lambda b, i: (b * nt + i, 0)
    const = lambda b, i: (0, 0)
    return pl.pallas_call(
        functools.partial(_delta_kernel, tt=tt),
        grid=(nb, nt),
        in_specs=[
            pl.BlockSpec((tt, 3 * DN_WIDTH), row),
            pl.BlockSpec((tt, DN_WIDTH), row),
            pl.BlockSpec((tt, 128), row),
            pl.BlockSpec((CONV_W, 3 * DN_WIDTH), const),
            pl.BlockSpec((1, 128), const),
            pl.BlockSpec((1, 128), const),
            pl.BlockSpec((1, DN_DIM), const),
        ],
        out_specs=[
            pl.BlockSpec((tt, DN_WIDTH), row),
            pl.BlockSpec((None, DN_HEADS, DN_DIM, DN_DIM), lambda b, i: (b, 0, 0, 0)),
        ],
        out_shape=[
            jax.ShapeDtypeStruct((n, DN_WIDTH), F32),
            jax.ShapeDtypeStruct((nb, DN_HEADS, DN_DIM, DN_DIM), F32),
        ],
        scratch_shapes=[
            pltpu.VMEM((tt + 8, 3 * DN_WIDTH), F32),
            pltpu.VMEM((DN_HEADS, DN_DIM, DN_DIM), F32),
        ],
        compiler_params=_cparams(("arbitrary", "arbitrary")),
        name="delta_prompt",
    )(qkv, z, small, conv_w, alog_row, dt_row, norm_g)


def _sdelta_kernel(qkv_ref, buf_ref, z_ref, sm_ref, cw_ref, alog_ref, dt_ref, ng_ref, s_ref, o_ref, s_out_ref, *, rows):
    cw = cw_ref[...]
    y = qkv_ref[...] * cw[3:4, :]
    for j in range(CONV_W - 1):
        y = y + buf_ref[:, j, :] * cw[j:j + 1, :]
    act = _silu(y)
    sm = sm_ref[...]
    beta_all = _sigmoid(sm)
    g_all = -jnp.exp(alog_ref[...]) * _softplus(sm + dt_ref[...])
    ng = ng_ref[...]
    sub = lax.broadcasted_iota(I32, (8, DN_DIM), 0)
    for h in range(DN_HEADS):
        lo, hi = h * DN_DIM, (h + 1) * DN_DIM
        q_h = act[:, lo:hi]
        k_h = act[:, DN_WIDTH + lo:DN_WIDTH + hi]
        v_h = act[:, 2 * DN_WIDTH + lo:2 * DN_WIDTH + hi]
        q_h = q_h * lax.rsqrt(jnp.sum(q_h * q_h, axis=-1, keepdims=True) + 1e-6) * (DN_DIM ** -0.5)
        k_h = k_h * lax.rsqrt(jnp.sum(k_h * k_h, axis=-1, keepdims=True) + 1e-6)
        g = g_all[:, SM_A + h:SM_A + h + 1]
        b = beta_all[:, SM_B + h:SM_B + h + 1]
        eg = jnp.exp(g)
        u = v_h * b
        w = k_h * (b * eg)
        qe = q_h * eg
        qk = jnp.sum(q_h * k_h, axis=-1, keepdims=True)
        z_h = z_ref[:, lo:hi]
        for r in range(rows):
            s = s_ref[r, h]
            w8 = jnp.where(sub == 0, w[r:r + 1], 0.0)
            q8 = jnp.where(sub == 0, qe[r:r + 1], 0.0)
            k8 = jnp.where(sub == 0, k_h[r:r + 1], 0.0)
            v_new = u[r:r + 1] - _dot(w8, s, HI)[0:1]
            o = _dot(q8, s, HI)[0:1] + qk[r:r + 1] * v_new
            vn8 = jnp.where(sub == 0, v_new, 0.0)
            s_out_ref[r, h] = s * eg[r:r + 1] + _dot_tn(k8, vn8, HI)
            o = o * lax.rsqrt(jnp.mean(o * o, axis=-1, keepdims=True) + EPS) * ng
            o_ref[r:r + 1, lo:hi] = o * _silu(z_h[r:r + 1])


def _delta_sample(qkv, buf, z, small, conv_w, alog_row, dt_row, norm_g, layer, state, rows=8):
    n = qkv.shape[0]
    row = lambda i: (i, 0)
    const = lambda i: (0, 0)
    return pl.pallas_call(
        functools.partial(_sdelta_kernel, rows=rows),
        grid=(n // rows,),
        in_specs=[
            pl.BlockSpec((rows, 3 * DN_WIDTH), row),
            pl.BlockSpec((rows, CONV_W - 1, 3 * DN_WIDTH), lambda i: (i, 0, 0)),
            pl.BlockSpec((rows, DN_WIDTH), row),
            pl.BlockSpec((rows, 128), row),
            pl.BlockSpec((CONV_W, 3 * DN_WIDTH), const),
            pl.BlockSpec((1, 128), const),
            pl.BlockSpec((1, 128), const),
            pl.BlockSpec((1, DN_DIM), const),
            pl.BlockSpec((None, rows, DN_HEADS, DN_DIM, DN_DIM), lambda i: (layer, i, 0, 0, 0)),
        ],
        out_specs=[
            pl.BlockSpec((rows, DN_WIDTH), row),
            pl.BlockSpec((rows, DN_HEADS, DN_DIM, DN_DIM), lambda i: (i, 0, 0, 0)),
        ],
        out_shape=[
            jax.ShapeDtypeStruct((n, DN_WIDTH), F32),
            jax.ShapeDtypeStruct(state.shape[1:], F32),
        ],
        compiler_params=_cparams(("parallel",)),
        name="delta_sample",
    )(qkv, buf, z, small, conv_w, alog_row, dt_row, norm_g, state)


def _score_keys(s):
    s = jnp.where(s == 0.0, 0.0, s)
    k = pltpu.bitcast(s, I32)
    return jnp.where(k < 0, k ^ jnp.int32(0x7FFFFFFF), k)


SEARCH_ROUND = 4


def _alibi_slope(h):
    return 2.0 ** (-8.0 * (h + 1) / ATT_HEADS)


def _fold_groups(x, reduce_fn, ways=8):
    g = x.shape[0]
    if g % ways or g <= ways:
        return reduce_fn(x, axis=0)
    return reduce_fn(reduce_fn(x.reshape(g // ways, ways, 8, x.shape[-1]), axis=0), axis=0)


def _sublane_allreduce(x, op):
    for shift in (4, 2, 1):
        x = op(x, pltpu.roll(x, shift, axis=0))
    return x


def _attn_kernel(q_ref, iq_ref, sm_ref, k_ref, vt_ref, ik_ref, o_ref, keys, alibi, acc_sc, *, tq, ck):
    i = pl.program_id(1)
    nk = (i * tq + tq + ck - 1) // ck
    g8 = ck // 8
    t = i * tq + lax.broadcasted_iota(I32, (8, tq), 1)
    sub = lax.broadcasted_iota(I32, (ck, 1), 0)
    sub3 = lax.broadcasted_iota(I32, (g8, 8, 1), 0) * 8 + lax.broadcasted_iota(I32, (g8, 8, 1), 1)

    @pl.when((pl.program_id(0) == 0) & (i == 0))
    def _():
        for h in range(ATT_HEADS):
            alibi[h] = jnp.broadcast_to(sub.astype(F32) * (_alibi_slope(h) * LOG2E), (ck, tq))

    sm_t = sm_ref[...].T
    iw = [jnp.broadcast_to(sm_t[SM_IW + h:SM_IW + h + 1, :] * (IDX_HEADS ** -0.5 * IDX_DIM ** -0.5), (8, tq))
          for h in range(IDX_HEADS)]
    iq = iq_ref[...]

    def score_chunk(c, diagonal):
        off = pl.multiple_of(c * ck, ck)
        d = _dot_nt(ik_ref[pl.ds(off, ck), :], iq)
        s = jnp.maximum(d[:, 0:tq].reshape(g8, 8, tq), 0.0) * iw[0][None]
        for h in range(1, IDX_HEADS):
            s = s + jnp.maximum(d[:, h * tq:(h + 1) * tq].reshape(g8, 8, tq), 0.0) * iw[h][None]
        if diagonal:
            s = jnp.where(off + sub3 <= t[None], s, -jnp.inf)
        keys[c] = _score_keys(s)

    def score_full(c, carry):
        score_chunk(c, False)
        return carry

    lax.fori_loop(0, nk - 1, score_full, 0)
    score_chunk(nk - 1, True)

    kth = jnp.minimum(t + 1, TOPK)

    def count(pred_fn):
        def body(c, cnt):
            return cnt + _fold_groups(jnp.where(pred_fn(keys[c], c), 1, 0).astype(I32), jnp.sum)
        return _sublane_allreduce(lax.fori_loop(0, nk, body, jnp.zeros((8, tq), I32)), jnp.add)

    def search_cond(st):
        return (st[0] < 32) & (st[3] > 0)

    def search_round(st):
        it, thr, cnt_thr, _ = st
        for u in range(SEARCH_ROUND):
            cand = thr ^ jnp.left_shift(jnp.int32(1), 31 - u - it)
            cnt = count(lambda kc, c: kc >= cand[None])
            ok = cnt >= kth
            thr = jnp.where(ok, cand, thr)
            cnt_thr = jnp.where(ok, cnt, cnt_thr)
        return it + SEARCH_ROUND, thr, cnt_thr, jnp.max(jnp.where(cnt_thr != kth, 1, 0))

    _, thr, _, unresolved = lax.while_loop(
        search_cond, search_round,
        (jnp.int32(0), jnp.full((8, tq), -2 ** 31, I32), jnp.full((8, tq), 2 ** 30, I32), jnp.int32(1)))

    @pl.when(unresolved > 0)
    def _():
        need = kth - count(lambda kc, c: kc > thr[None])

        def pos_step(it, cut):
            cand = cut | jnp.left_shift(jnp.int32(1), 13 - it)
            cnt = count(lambda kc, c: (kc == thr[None]) & (c * ck + sub3 < cand[None]))
            return jnp.where(cnt < need, cand, cut)
        cut = lax.fori_loop(0, 14, pos_step, jnp.zeros((8, tq), I32))

        def drop(c, carry):
            kc = keys[c]
            keys[c] = jnp.where((kc == thr[None]) & (c * ck + sub3 > cut[None]), jnp.int32(-2 ** 31), kc)
            return carry
        lax.fori_loop(0, nk, drop, 0)

    acc_sc[...] = jnp.zeros_like(acc_sc)
    q = q_ref[...]
    qh = [q[:, h * ATT_DIM:(h + 1) * ATT_DIM] for h in range(ATT_HEADS)]
    heads = range(ATT_HEADS)

    def attend_chunk(c, carry):
        ms, ls = carry
        off = pl.multiple_of(c * ck, ck)
        selb = jnp.where(keys[c] >= thr[None], 0.0, NEG).reshape(ck, tq)
        kc = k_ref[pl.ds(off, ck), :]
        off_f = off.astype(F32)
        c_off = [off_f * (_alibi_slope(h) * LOG2E) for h in heads]
        zs = [(_dot_nt(kc[:, h * ATT_DIM:(h + 1) * ATT_DIM], qh[h]) * (ATT_DIM ** -0.5 * LOG2E)
               + (alibi[h] + selb)).reshape(g8, 8, tq) for h in heads]
        m_new = [jnp.maximum(ms[h], _sublane_allreduce(_fold_groups(zs[h], jnp.max), jnp.maximum) + c_off[h])
                 for h in heads]
        a = [jnp.exp2(ms[h] - m_new[h]) for h in heads]
        ps = [jnp.exp2(zs[h] - (m_new[h] - c_off[h])[None]) for h in heads]
        l_new = [a[h] * ls[h] + _sublane_allreduce(_fold_groups(ps[h], jnp.sum), jnp.add) for h in heads]
        for h in heads:
            pv = _dot(vt_ref[c, h * ATT_DIM:(h + 1) * ATT_DIM, :], ps[h].reshape(ck, tq).astype(BF16))
            acc_sc[h] = a[h][None] * acc_sc[h] + pv.reshape(ATT_DIM // 8, 8, tq)
        return tuple(m_new), tuple(l_new)

    init = (tuple(jnp.full((8, tq), NEG, F32) for _ in heads), tuple(jnp.zeros((8, tq), F32) for _ in heads))
    _, ls = lax.fori_loop(0, nk, attend_chunk, init)
    for h in heads:
        o_ref[:, h * ATT_DIM:(h + 1) * ATT_DIM] = (acc_sc[h] / ls[h][None]).reshape(ATT_DIM, tq).T


def _attn_prompt(q16, iq16, small, k16, vt16, ik16, nb, tq=128, ck=512):
    n = q16.shape[0]
    s = n // nb
    nq = s // tq
    row = lambda b, i: (b * nq + i, 0)
    full = lambda b, i: (b, 0)
    once = pl.Buffered(1)
    return pl.pallas_call(
        functools.partial(_attn_kernel, tq=tq, ck=ck),
        grid=(nb, nq),
        in_specs=[
            pl.BlockSpec((tq, ATT_WIDTH), row),
            pl.BlockSpec((IDX_HEADS * tq, IDX_DIM), row),
            pl.BlockSpec((tq, 128), row),
            pl.BlockSpec((s, ATT_WIDTH), full, pipeline_mode=once),
            pl.BlockSpec((None, s // ck, ATT_WIDTH, ck), lambda b, i: (b, 0, 0, 0), pipeline_mode=once),
            pl.BlockSpec((s, IDX_DIM), full, pipeline_mode=once),
        ],
        out_specs=pl.BlockSpec((tq, ATT_WIDTH), row),
        out_shape=jax.ShapeDtypeStruct((n, ATT_WIDTH), F32),
        scratch_shapes=[
            pltpu.VMEM((s // ck, ck // 8, 8, tq), I32),
            pltpu.VMEM((ATT_HEADS, ck, tq), F32),
            pltpu.VMEM((ATT_HEADS, ATT_DIM // 8, 8, tq), F32),
        ],
        compiler_params=_cparams(("arbitrary", "arbitrary")),
        name="attn_prompt",
    )(q16, iq16, small, k16, vt16, ik16)


ATTEND_PAGES = 32
SCORE_PAGES = 64


def _sscore_kernel(pt_ref, iq_ref, iw_ref, ikn_ref, *refs, pgs):
    pages, o_ref = refs[:pgs], refs[pgs]
    j = pl.program_id(1)
    iq = iq_ref[...]
    iw = iw_ref[...] * (IDX_HEADS ** -0.5)

    @pl.when(j < pl.num_programs(1) - 1)
    def _():
        for u in range(pgs):
            d = _dot(iq, pages[u][...]) * (IDX_DIM ** -0.5)
            o_ref[:, u * PAGE:(u + 1) * PAGE] = jnp.sum(jnp.maximum(d, 0.0) * iw, axis=0, keepdims=True)

    @pl.when(j == pl.num_programs(1) - 1)
    def _():
        d = _dot_nt(iq, jnp.broadcast_to(ikn_ref[...], (8, IDX_DIM)))[:, 0:1] * (IDX_DIM ** -0.5)
        s_new = jnp.sum(jnp.maximum(d, 0.0) * iw, axis=0, keepdims=True)
        lane = lax.broadcasted_iota(I32, (1, pgs * PAGE), 1)
        o_ref[...] = jnp.where(lane == 0, s_new, -jnp.inf)


def _sample_scores(layer, page_table, iq3, iw3, ik_new3, cache_idx_t):
    nbatch, n_pages = page_table.shape
    pgs = min(SCORE_PAGES, n_pages)
    nj = n_pages // pgs

    def page_map(u):
        return lambda b, j, pt: (layer, pt[b, jnp.minimum(j * pgs + u, n_pages - 1)], 0, 0)

    grid_spec = pltpu.PrefetchScalarGridSpec(
        num_scalar_prefetch=1,
        grid=(nbatch, nj + 1),
        in_specs=[
            pl.BlockSpec((None, IDX_HEADS, IDX_DIM), lambda b, j, pt: (b, 0, 0)),
            pl.BlockSpec((None, IDX_HEADS, 1), lambda b, j, pt: (b, 0, 0)),
            pl.BlockSpec((None, 1, IDX_DIM), lambda b, j, pt: (b, 0, 0)),
        ] + [pl.BlockSpec((None, None, IDX_DIM, PAGE), page_map(u)) for u in range(pgs)],
        out_specs=pl.BlockSpec((None, 1, pgs * PAGE), lambda b, j, pt: (b, 0, j)),
    )
    return pl.pallas_call(
        functools.partial(_sscore_kernel, pgs=pgs),
        grid_spec=grid_spec,
        out_shape=jax.ShapeDtypeStruct((nbatch, 1, (nj + 1) * pgs * PAGE), F32),
        compiler_params=_cparams(("arbitrary", "arbitrary")),
        name="sample_scores",
    )(page_table, iq3, iw3, ik_new3, *([cache_idx_t] * pgs))


def _sselect_kernel(s_ref, o_ref):
    keys = _score_keys(s_ref[...])
    rows, width = keys.shape
    pos = lax.broadcasted_iota(I32, (1, width), 1)

    def cnt(m):
        return jnp.sum(jnp.where(m, 1, 0).astype(I32), axis=-1, keepdims=True)

    def search_step(it, thr):
        cand = thr ^ jnp.left_shift(jnp.int32(1), 31 - it)
        return jnp.where(cnt(keys >= cand) >= TOPK, cand, thr)

    thr = lax.fori_loop(0, 32, search_step, jnp.full((rows, 1), -2 ** 31, I32))
    need = TOPK - cnt(keys > thr)
    eq = keys == thr

    def pos_step(it, cut):
        cand = cut | jnp.left_shift(jnp.int32(1), 14 - it)
        return jnp.where(cnt(eq & (pos < cand)) < need, cand, cut)

    cut = lax.fori_loop(0, 15, pos_step, jnp.zeros((rows, 1), I32))
    sel = (keys > thr) | (eq & (pos <= cut))
    o_ref[...] = jnp.where(sel, 0.0, NEG)


def _sample_select(scores):
    return pl.pallas_call(
        _sselect_kernel,
        out_shape=jax.ShapeDtypeStruct(scores.shape, F32),
        compiler_params=pltpu.CompilerParams(vmem_limit_bytes=VMEM_LIMIT),
        name="sample_select",
    )(scores)


PAGE_ROWS = PAGE * ATT_HEADS


def _sattn_kernel(pt_ref, q_ref, kn_ref, vn_ref, selb_ref, seln_ref, *refs, past, pg):
    kp, vp = refs[:pg], refs[pg:2 * pg]
    o_ref, lg_sc, l_sc, pn_sc, acc_sc = refs[2 * pg:2 * pg + 5]
    ph = pl.program_id(1)
    j = pl.program_id(2)
    scale = ATT_DIM ** -0.5
    q8 = q_ref[...]
    row = lax.broadcasted_iota(I32, (8, 1), 0)
    lane = lax.broadcasted_iota(I32, (1, PAGE_ROWS), 1)
    slope = jnp.exp2(-8.0 * (row + 1).astype(F32) / ATT_HEADS)
    own = jnp.bitwise_and(lane, ATT_HEADS - 1) == row
    tok = jnp.right_shift(lane, 2)

    @pl.when(ph == 0)
    def _():
        for u in range(pg):
            dist = (past - ((j * pg + u) * PAGE + tok)).astype(F32)
            lg = _dot_nt(q8, kp[u][...]) * scale - slope * dist + selb_ref[:, u * PAGE_ROWS:(u + 1) * PAGE_ROWS]
            lg_sc[j * pg + u] = jnp.where(own, lg, NEG)

    @pl.when((ph == 1) & (j == 0))
    def _():
        allg = lg_sc[...]
        lgn = jnp.sum(q8 * kn_ref[...], axis=-1, keepdims=True) * scale + seln_ref[:, 0:1]
        m = jnp.maximum(jnp.max(jnp.max(allg, axis=0), axis=-1, keepdims=True), lgn)
        p = jnp.exp(allg - m)
        lg_sc[...] = p
        pn = jnp.exp(lgn - m)
        l_sc[...] = jnp.sum(jnp.sum(p, axis=0), axis=-1, keepdims=True) + pn
        pn_sc[...] = pn
        acc_sc[...] = jnp.zeros_like(acc_sc)

    @pl.when(ph == 1)
    def _():
        acc = acc_sc[...]
        for u in range(pg):
            acc = acc + _dot(lg_sc[j * pg + u], vp[u][...])
        acc_sc[...] = acc

    @pl.when((ph == 1) & (j == pl.num_programs(2) - 1))
    def _():
        o_ref[...] = (acc_sc[...] + pn_sc[...] * vn_ref[...]) / l_sc[...]


def _sample_attend(layer, page_table, q8, kn8, vn8, selb4, selb, cache_k4, cache_v4):
    nbatch, n_pages = page_table.shape
    pg = min(ATTEND_PAGES, n_pages)
    nj = n_pages // pg
    past = n_pages * PAGE

    def k_map(u):
        return lambda b, ph, j, pt: (layer, pt[b, jnp.where(ph == 0, j, nj - 1) * pg + u], 0, 0)

    def v_map(u):
        return lambda b, ph, j, pt: (layer, pt[b, jnp.where(ph == 0, 0, j) * pg + u], 0, 0)

    vec = pl.BlockSpec((None, 8, ATT_DIM), lambda b, ph, j, pt: (b, 0, 0))
    grid_spec = pltpu.PrefetchScalarGridSpec(
        num_scalar_prefetch=1,
        grid=(nbatch, 2, nj),
        in_specs=[
            vec, vec, vec,
            pl.BlockSpec((None, 1, pg * PAGE_ROWS), lambda b, ph, j, pt: (b, 0, jnp.where(ph == 0, j, nj - 1))),
            pl.BlockSpec((None, 1, 128), lambda b, ph, j, pt: (b, 0, past // 128)),
        ] + [pl.BlockSpec((None, None, PAGE_ROWS, ATT_DIM), k_map(u)) for u in range(pg)]
          + [pl.BlockSpec((None, None, PAGE_ROWS, ATT_DIM), v_map(u)) for u in range(pg)],
        out_specs=vec,
        scratch_shapes=[
            pltpu.VMEM((n_pages, 8, PAGE_ROWS), F32),
            pltpu.VMEM((8, 1), F32),
            pltpu.VMEM((8, 1), F32),
            pltpu.VMEM((8, ATT_DIM), F32),
        ],
    )
    return pl.pallas_call(
        functools.partial(_sattn_kernel, past=past, pg=pg),
        grid_spec=grid_spec,
        out_shape=jax.ShapeDtypeStruct((nbatch, 8, ATT_DIM), F32),
        compiler_params=_cparams(("arbitrary", "arbitrary", "arbitrary")),
        name="sample_attend",
    )(page_table, q8, kn8, vn8, selb4, selb, *([cache_k4] * pg), *([cache_v4] * pg))


def _mix_kernel(oa_ref, ob_ref, gates_ref, x_ref, gm_ref, wa_ref, wb_ref, wo_ref, g_ref, b_ref, o_ref):
    br_a = _dot(oa_ref[...].astype(BF16), wa_ref[...])
    br_b = _dot(ob_ref[...].astype(BF16), wb_ref[...])
    merged = _sigmoid(gates_ref[:, :D_MODEL]) * br_a + _sigmoid(gates_ref[:, D_MODEL:]) * br_b
    y = _dot(merged.astype(BF16), wo_ref[...])
    r = ALPHA * x_ref[...] + gm_ref[...] * y
    o_ref[...] = _layernorm(r, g_ref[...], b_ref[...])


def _mix(oa, ob, gates, x, gm, wa, wb, wo, ln_g, ln_b, tm):
    n = x.shape[0]
    nb, r, _ = gm.shape
    nt = n // tm
    tiles_per_b = nt // nb
    row = lambda i: (i, 0)
    const = lambda i: (0, 0)
    return pl.pallas_call(
        _mix_kernel,
        grid=(nt,),
        in_specs=[
            pl.BlockSpec((tm, DN_WIDTH), row),
            pl.BlockSpec((tm, ATT_WIDTH), row),
            pl.BlockSpec((tm, 2 * D_MODEL), row),
            pl.BlockSpec((tm, D_MODEL), row),
            pl.BlockSpec((None, r, D_MODEL), lambda i: (i // tiles_per_b, 0, 0)),
            pl.BlockSpec((DN_WIDTH, D_MODEL), const),
            pl.BlockSpec((ATT_WIDTH, D_MODEL), const),
            pl.BlockSpec((D_MODEL, D_MODEL), const),
            pl.BlockSpec((1, D_MODEL), const),
            pl.BlockSpec((1, D_MODEL), const),
        ],
        out_specs=pl.BlockSpec((tm, D_MODEL), row),
        out_shape=jax.ShapeDtypeStruct((n, D_MODEL), F32),
        compiler_params=_cparams(("parallel",)),
        name="mix_out",
    )(oa, ob, gates, x, gm, wa, wb, wo, ln_g, ln_b)


def _route_kernel(x_ref, sc_ref, sh_ref, wr_ref, o_ref, e_ref, g_ref):
    h = x_ref[...] * (1.0 + sc_ref[...]) + sh_ref[...]
    o_ref[...] = h.astype(o_ref.dtype)
    logits = _dot(h, wr_ref[...], HI)
    lane = lax.broadcasted_iota(I32, logits.shape, 1)
    logits = jnp.where(lane < N_EXPERTS, logits, -jnp.inf)
    e = jnp.exp(logits - jnp.max(logits, axis=-1, keepdims=True))
    p = e / jnp.sum(e, axis=-1, keepdims=True)
    p1 = jnp.max(p, axis=-1, keepdims=True)
    e1 = jnp.min(jnp.where(p == p1, lane, 128), axis=-1, keepdims=True)
    rest = jnp.where((lane == e1) | (lane >= N_EXPERTS), -1.0, p)
    p2 = jnp.max(rest, axis=-1, keepdims=True)
    e2 = jnp.min(jnp.where(rest == p2, lane, 128), axis=-1, keepdims=True)
    tot = p1 + p2
    e_ref[...] = jnp.concatenate([e1, e2], axis=1)
    g_ref[...] = jnp.concatenate([p1 / tot, p2 / tot], axis=1)


def _ffn_route(x, sc, sh, tm, w_router):
    n = x.shape[0]
    nb, r, _ = sc.shape
    nt = n // tm
    tiles_per_b = nt // nb
    row = lambda i: (i, 0)
    mod_spec = pl.BlockSpec((None, r, D_MODEL), lambda i: (i // tiles_per_b, 0, 0))
    in_specs = [pl.BlockSpec((tm, D_MODEL), row), mod_spec, mod_spec]
    wr = jnp.pad(w_router, ((0, 0), (0, 128 - N_EXPERTS)))
    return pl.pallas_call(
        _route_kernel, grid=(nt,),
        in_specs=in_specs + [pl.BlockSpec((D_MODEL, 128), lambda i: (0, 0))],
        out_specs=[pl.BlockSpec((tm, D_MODEL), row), pl.BlockSpec((tm, 2), row), pl.BlockSpec((tm, 2), row)],
        out_shape=[jax.ShapeDtypeStruct((n, D_MODEL), F32), jax.ShapeDtypeStruct((n, 2), I32),
                   jax.ShapeDtypeStruct((n, 2), F32)],
        compiler_params=_cparams(("parallel",)), name="ffn_route",
    )(x, sc, sh, wr)


def _gffn_kernel(te_ref, nu_ref, x_ref, *refs, modulate, substitute):
    if modulate:
        sc_ref, sh_ref = refs[:2]
        refs = refs[2:]
    if substitute:
        src_ref, late_ref = refs[:2]
        refs = refs[2:]
    wg_ref, wu_ref, wd_ref, o_ref, acc, x16 = refs
    t = pl.program_id(0)
    f = pl.program_id(1)

    @pl.when((t < nu_ref[0]) & (f == 0))
    def _():
        x = x_ref[...]
        if modulate:
            x = x * (1.0 + sc_ref[...]) + sh_ref[...]
        x = x.astype(BF16)
        if substitute:
            src = src_ref[...]
            late = late_ref[...]
            pick = jnp.where(src == lax.broadcasted_iota(I32, (src.shape[0], late.shape[0]), 1), 1.0, 0.0)
            x = jnp.where(src >= 0, _dot(pick.astype(BF16), late).astype(BF16), x)
        x16[...] = x

    @pl.when(t < nu_ref[0])
    def _():
        x = x16[...]
        a = _dot(x, wg_ref[...].astype(BF16))
        b = _dot(x, wu_ref[...].astype(BF16))
        y = _dot((_silu(a) * b).astype(BF16), wd_ref[...].astype(BF16))

        @pl.when(f == 0)
        def _():
            acc[...] = y

        @pl.when(f > 0)
        def _():
            acc[...] += y

    @pl.when(f == pl.num_programs(1) - 1)
    def _():
        o_ref[...] = jnp.where(t < nu_ref[0], acc[...], 0.0)


def _grouped_ffn(x, tile_expert, n_used, layer, w_gate, w_up, w_down, tm, mod=None, late=None, tf=256):
    n = x.shape[0]
    nt = n // tm
    nf = D_FF // tf
    mod_specs, mod_args = [], []
    if mod is not None:
        nb, r, _ = mod[0].shape
        tiles_per_b = nt // nb
        mod_specs = [pl.BlockSpec((None, r, D_MODEL), lambda t, f, te, nu: (t // tiles_per_b, 0, 0))] * 2
        mod_args = list(mod)
    if late is not None:
        mod_specs = mod_specs + [pl.BlockSpec((tm, 1), lambda t, f, te, nu: (t, 0)),
                                 pl.BlockSpec(late[1].shape, lambda t, f, te, nu: (0, 0))]
        mod_args = mod_args + list(late)
    grid_spec = pltpu.PrefetchScalarGridSpec(
        num_scalar_prefetch=2,
        grid=(nt, nf),
        in_specs=[pl.BlockSpec((tm, D_MODEL), lambda t, f, te, nu: (t, 0))] + mod_specs + [
            pl.BlockSpec((None, None, D_MODEL, tf), lambda t, f, te, nu: (layer, te[t], 0, f)),
            pl.BlockSpec((None, None, D_MODEL, tf), lambda t, f, te, nu: (layer, te[t], 0, f)),
            pl.BlockSpec((None, None, tf, D_MODEL), lambda t, f, te, nu: (layer, te[t], f, 0)),
        ],
        out_specs=pl.BlockSpec((tm, D_MODEL), lambda t, f, te, nu: (t, 0)),
        scratch_shapes=[pltpu.VMEM((tm, D_MODEL), F32), pltpu.VMEM((tm, D_MODEL), BF16)],
    )
    return pl.pallas_call(
        functools.partial(_gffn_kernel, modulate=mod is not None, substitute=late is not None),
        grid_spec=grid_spec,
        out_shape=jax.ShapeDtypeStruct((n, D_MODEL), F32),
        compiler_params=_cparams(("arbitrary", "arbitrary")),
        name="grouped_ffn",
    )(tile_expert, n_used, x, *mod_args, w_gate, w_up, w_down)


def _res_ln_kernel(x_ref, y_ref, gf_ref, g_ref, b_ref, o_ref):
    r = ALPHA * x_ref[...] + gf_ref[...] * y_ref[...]
    o_ref[...] = _layernorm(r, g_ref[...], b_ref[...])


def _res_ln2_kernel(x_ref, y0_ref, y1_ref, gt_ref, gf_ref, g_ref, b_ref, o_ref):
    gt = gt_ref[...]
    y = y0_ref[...] * gt[:, 0:1] + y1_ref[...] * gt[:, 1:2]
    r = ALPHA * x_ref[...] + gf_ref[...] * y
    o_ref[...] = _layernorm(r, g_ref[...], b_ref[...])


def _res_ln(x, ys, gate2, gf, ln_g, ln_b, tm):
    n = x.shape[0]
    nb, r, _ = gf.shape
    nt = n // tm
    tiles_per_b = nt // nb
    row = lambda i: (i, 0)
    const = lambda i: (0, 0)
    big = pl.BlockSpec((tm, D_MODEL), row)
    mod_spec = pl.BlockSpec((None, r, D_MODEL), lambda i: (i // tiles_per_b, 0, 0))
    vec = pl.BlockSpec((1, D_MODEL), const)
    if gate2 is None:
        kern, ins, specs = _res_ln_kernel, (x, ys[0], gf, ln_g, ln_b), [big, big, mod_spec, vec, vec]
    else:
        kern = _res_ln2_kernel
        ins = (x, ys[0], ys[1], gate2, gf, ln_g, ln_b)
        specs = [big, big, big, pl.BlockSpec((tm, 2), row), mod_spec, vec, vec]
    return pl.pallas_call(
        kern, grid=(nt,), in_specs=specs, out_specs=big,
        out_shape=jax.ShapeDtypeStruct((n, D_MODEL), F32),
        compiler_params=_cparams(("parallel",)), name="res_ln",
    )(*ins)


def _dense_ffn(x, sc, sh, gf, layer, w_gate, w_up, w_down, ln_g, ln_b, tm_mod, tm_ffn):
    n = x.shape[0]
    nt = n // tm_ffn
    y = _grouped_ffn(x, jnp.zeros((nt,), I32), jnp.full((1,), nt, I32), layer,
                     w_gate[:, None], w_up[:, None], w_down[:, None], tm_ffn, mod=(sc, sh))
    return _res_ln(x, (y,), None, gf, ln_g, ln_b, tm_mod)


def _expert_ranks(top_e):
    flat_e = top_e.reshape(-1)
    onehot = flat_e[:, None] == jnp.arange(N_EXPERTS, dtype=I32)[None, :]
    seen = jnp.cumsum(onehot.astype(I32), axis=0)
    pick = lambda table: jnp.sum(jnp.where(onehot, table, 0), axis=1)
    return flat_e, seen[-1], pick(seen) - 1, pick


def _moe_ffn(main, late, w_router, layer, w_gate, w_up, w_down, ln_g, ln_b, tm_e):
    x_m, x_l = main[0], late[0]
    n_m, n_l = x_m.shape[0], x_l.shape[0]
    h_m, top_m, gate_m = _ffn_route(*main[:3], main[4], w_router)
    flat_m, counts_m, rank_m, pick_m = _expert_ranks(top_m)
    padded = (counts_m + n_l + tm_e - 1) // tm_e * tm_e
    pad_end = jnp.cumsum(padded)
    pad_start = pad_end - padded
    n_tiles = -(-(2 * n_m + N_EXPERTS * n_l) // tm_e) + N_EXPERTS
    tile_expert = jnp.minimum(
        jnp.searchsorted(pad_end, jnp.arange(n_tiles, dtype=I32) * tm_e, side="right"), N_EXPERTS - 1).astype(I32)
    n_used = (pad_end[-1] // tm_e).astype(I32).reshape(1)
    slot_m = (pick_m(pad_start[None, :]) + rank_m).reshape(n_m, 2)
    rows_m = jnp.zeros((n_tiles * tm_e,), I32).at[slot_m.reshape(-1)].set(
        jnp.arange(2 * n_m, dtype=I32) // 2, unique_indices=True)
    xb = h_m[rows_m]

    h_l, top_l, gate_l = _ffn_route(*late[:3], late[4], w_router)
    flat_l, counts_l, rank_l, pick_l = _expert_ranks(top_l)
    slot_l = (pick_l((pad_start + counts_m)[None, :]) + rank_l).reshape(n_l, 2)
    src_l = jnp.full((n_tiles * tm_e,), -1, I32).at[slot_l.reshape(-1)].set(
        jnp.arange(2 * n_l, dtype=I32) // 2, unique_indices=True).reshape(-1, 1)
    yb = _grouped_ffn(xb, tile_expert, n_used, layer, w_gate, w_up, w_down, tm_e, late=(src_l, h_l.astype(BF16)))
    out_m = _res_ln(x_m, (yb[slot_m[:, 0]], yb[slot_m[:, 1]]), gate_m, main[3], ln_g, ln_b, main[4])
    out_l = _res_ln(x_l, (yb[slot_l[:, 0]], yb[slot_l[:, 1]]), gate_l, late[3], ln_g, ln_b, late[4])
    return out_m, out_l


def kernel(x_prompt, x_sample, cache_k, cache_v, cache_idx_k, state_delta, state_conv, page_table, c_prompt, c_sample, w_ada, b_ada, w_in, conv_w, a_log, dt_bias, dn_norm_g, w_branch, w_out, ln1_g, ln1_b, ln2_g, ln2_b, w_ffn_gate, w_ffn_up, w_ffn_down, w_router, w_exp_gate, w_exp_up, w_exp_down):
    bp, seq, _ = x_prompt.shape
    bs = x_sample.shape[0]
    n_p = bp * seq
    n_phys = cache_k.shape[1]
    past = page_table.shape[1] * PAGE

    c_all = jnp.concatenate([c_prompt, c_sample], axis=0)
    c_rows = -(-c_all.shape[0] // 8) * 8
    c_all = jnp.pad(c_all, ((0, c_rows - c_all.shape[0]), (0, 0)))
    mods = _ada_all(c_all, w_ada, b_ada)

    cache_k4 = cache_k.reshape(DEPTH, n_phys, PAGE_ROWS, ATT_DIM)
    cache_v4 = cache_v.reshape(DEPTH, n_phys, PAGE_ROWS, ATT_DIM)
    cache_idx_t = jnp.swapaxes(cache_idx_k, 2, 3)
    ck = 512
    tq = 256

    xp = x_prompt.reshape(n_p, D_MODEL)
    xs = x_sample.reshape(bs, D_MODEL)
    lane = jnp.arange(128)
    outs_p, outs_s = [], []
    for l in range(DEPTH):
        j = l // 2
        mod_p = [mods[l, :bp, u * D_MODEL:(u + 1) * D_MODEL].reshape(bp, 1, D_MODEL) for u in range(6)]
        mod_s = [mods[l, bp:bp + bs, u * D_MODEL:(u + 1) * D_MODEL].reshape(1, bs, D_MODEL) for u in range(6)]
        w_groups = _split_w_in(w_in[l])
        head_lane = jnp.clip(lane - SM_A, 0, DN_HEADS - 1)
        in_a = (lane >= SM_A) & (lane < SM_A + DN_HEADS)
        alog_row = jnp.where(in_a, a_log[l][head_lane], 0.0).reshape(1, 128)
        dt_row = jnp.where(in_a, dt_bias[l][head_lane], 0.0).reshape(1, 128)
        norm_g = dn_norm_g[l].reshape(1, DN_DIM)
        wa16 = w_branch[l, 0].astype(BF16)
        wb16 = w_branch[l, 1].astype(BF16)
        wo16 = w_out[l].astype(BF16)
        ln1 = (ln1_g[l].reshape(1, -1), ln1_b[l].reshape(1, -1))
        ln2 = (ln2_g[l].reshape(1, -1), ln2_b[l].reshape(1, -1))

        pr = _in_proj(xp, mod_p[1], mod_p[0], w_groups, tm=tq, ck=ck)
        oa, s_new_p = _delta_prompt(pr["qkv_a"], pr["z"], pr["small"], conv_w[l], alog_row, dt_row, norm_g, bp)
        ob = _attn_prompt(pr["q_b"], pr["iq"], pr["small"], pr["k_b16"], pr["v_b16"], pr["small16"], bp, tq=tq, ck=ck)
        x1 = _mix(oa, ob, pr["gates"], xp, mod_p[2], wa16, wb16, wo16, *ln1, tm=512)
        conv_p = pr["qkv_a"].reshape(bp, seq, -1)[:, seq - (CONV_W - 1):, :]
        outs_p.append((conv_p, s_new_p, pr["k_b"].reshape(bp, seq, ATT_HEADS, ATT_DIM),
                       pr["v_b"].reshape(bp, seq, ATT_HEADS, ATT_DIM),
                       pr["small"][:, SM_IK:SM_IK + IDX_DIM].reshape(bp, seq, IDX_DIM)))

        sr = _in_proj(xs, mod_s[1], mod_s[0], w_groups, tm=bs)
        oa_s, s_new_s = _delta_sample(sr["qkv_a"], state_conv[l], sr["z"], sr["small"], conv_w[l], alog_row, dt_row,
                                      norm_g, l, state_delta)
        ik_s = sr["small"][:, SM_IK:SM_IK + IDX_DIM]
        iq3 = sr["iq"].astype(F32).reshape(bs, IDX_HEADS, IDX_DIM)
        iw3 = sr["small"][:, SM_IW:SM_IW + IDX_HEADS].reshape(bs, IDX_HEADS, 1)
        scores = _sample_scores(l, page_table, iq3, iw3, ik_s.reshape(bs, 1, IDX_DIM), cache_idx_t)
        selb = _sample_select(scores.reshape(bs, -1)).reshape(scores.shape)
        head_rows = lambda a: jnp.pad(a.astype(F32).reshape(bs, ATT_HEADS, ATT_DIM), ((0, 0), (0, 8 - ATT_HEADS), (0, 0)))
        selb4 = jnp.repeat(selb, ATT_HEADS, axis=-1)
        ob_s = _sample_attend(l, page_table, head_rows(sr["q_b"]), head_rows(sr["k_b"]), head_rows(sr["v_b"]),
                              selb4, selb, cache_k4, cache_v4)[:, :ATT_HEADS, :].reshape(bs, ATT_WIDTH)
        x1s = _mix(oa_s, ob_s, sr["gates"], xs, mod_s[2], wa16, wb16, wo16, *ln1, tm=bs)

        if l % 2 == 0:
            xp = _dense_ffn(x1, mod_p[4], mod_p[3], mod_p[5], j, w_ffn_gate, w_ffn_up, w_ffn_down, *ln2,
                            tm_mod=512, tm_ffn=1024)
            xs = _dense_ffn(x1s, mod_s[4], mod_s[3], mod_s[5], j, w_ffn_gate, w_ffn_up, w_ffn_down, *ln2,
                            tm_mod=bs, tm_ffn=bs)
        else:
            xp, xs = _moe_ffn((x1, mod_p[4], mod_p[3], mod_p[5], 512), (x1s, mod_s[4], mod_s[3], mod_s[5], bs),
                              w_router[j], j, w_exp_gate, w_exp_up, w_exp_down, *ln2, tm_e=1024)
        conv_s =jnp.concatenate([state_conv[l][:, 1:, :], sr["qkv_a"][:, None, :]], axis=1)
        outs_s.append((conv_s, s_new_s, sr["k_b"].reshape(bs, 1, ATT_HEADS, ATT_DIM),
                       sr["v_b"].reshape(bs, 1, ATT_HEADS, ATT_DIM), ik_s.reshape(bs, 1, IDX_DIM)))

    stack = lambda rows, u: jnp.stack([r[u] for r in rows])
    return (xp.reshape(bp, seq, D_MODEL), xs.reshape(bs, 1, D_MODEL),
            stack(outs_p, 2), stack(outs_p, 3), stack(outs_p, 4), stack(outs_p, 1), stack(outs_p, 0),
            stack(outs_s, 2), stack(outs_s, 3), stack(outs_s, 4), stack(outs_s, 1), stack(outs_s, 0))
```

```python
import functools

import jax
import jax.numpy as jnp
from jax import lax
from jax.experimental import pallas as pl
from jax.experimental.pallas import tpu as pltpu

F32 = jnp.float32
BF16 = jnp.bfloat16
I32 = jnp.int32
HI = lax.Precision.HIGHEST

D_MODEL = 1024
DEPTH = 4
PAGE = 128
DN_HEADS = 4
DN_DIM = 128
DN_WIDTH = DN_HEADS * DN_DIM
CONV_W = 4
DN_CHUNK = 64
ATT_HEADS = 4
ATT_DIM = 128
ATT_WIDTH = ATT_HEADS * ATT_DIM
IDX_HEADS = 8
IDX_DIM = 64
TOPK = 256
D_FF = 2816
N_EXPERTS = 8
ALPHA = (2.0 * DEPTH) ** 0.25
EPS = 1e-5
NEG = -1e30
LOG2E = 1.4426950408889634

SM_IK = 0
SM_B = 64
SM_A = 68
SM_IW = 72

VMEM_LIMIT = 56 * 1024 * 1024


def _cparams(sem):
    return pltpu.CompilerParams(dimension_semantics=sem, vmem_limit_bytes=VMEM_LIMIT)


def _sigmoid(x):
    return jax.nn.sigmoid(x)


def _silu(x):
    return x * jax.nn.sigmoid(x)


def _softplus(x):
    return jnp.maximum(x, 0.0) + jnp.log1p(jnp.exp(-jnp.abs(x)))


def _dot(a, b, precision=None):
    return jnp.dot(a, b, precision=precision, preferred_element_type=F32)


def _dot_nt(a, b, precision=None):
    return lax.dot_general(a, b, (((1,), (1,)), ((), ())), precision=precision, preferred_element_type=F32)


def _dot_tn(a, b, precision=None):
    return lax.dot_general(a, b, (((0,), (0,)), ((), ())), precision=precision, preferred_element_type=F32)


def _split(a):
    hi = a.astype(BF16)
    return hi, (a - hi.astype(F32)).astype(BF16)


def _dot3(a, b):
    return _dot(a[0], b[0]) + (_dot(a[0], b[1]) + _dot(a[1], b[0]))


def _layernorm(r, g, b):
    mu = jnp.mean(r, axis=-1, keepdims=True)
    rc = r - mu
    var = jnp.mean(rc * rc, axis=-1, keepdims=True)
    return rc * lax.rsqrt(var + EPS) * g + b


def _ada_kernel(c_ref, w_ref, b_ref, o_ref):
    o_ref[...] = _dot(_silu(c_ref[...]), w_ref[...], HI) + b_ref[...]


def _ada_all(c, w_ada, b_ada):
    rows = c.shape[0]
    nj = w_ada.shape[2] // D_MODEL
    return pl.pallas_call(
        _ada_kernel,
        grid=(DEPTH, nj),
        in_specs=[
            pl.BlockSpec((rows, D_MODEL), lambda l, j: (0, 0)),
            pl.BlockSpec((None, D_MODEL, D_MODEL), lambda l, j: (l, 0, j)),
            pl.BlockSpec((None, 1, D_MODEL), lambda l, j: (l, 0, j)),
        ],
        out_specs=pl.BlockSpec((None, rows, D_MODEL), lambda l, j: (l, 0, j)),
        out_shape=jax.ShapeDtypeStruct((DEPTH, rows, nj * D_MODEL), F32),
        compiler_params=_cparams(("arbitrary", "arbitrary")),
        name="ada_mod",
    )(c, w_ada, b_ada.reshape(DEPTH, 1, -1))


IN_GROUPS = (
    ("qkv_a", 3 * DN_WIDTH, (F32,)),
    ("z", DN_WIDTH, (F32,)),
    ("q_b", ATT_WIDTH, (BF16,)),
    ("k_b", ATT_WIDTH, (F32, BF16)),
    ("v_b", ATT_WIDTH, (F32, BF16)),
    ("iq", IDX_HEADS * IDX_DIM, (BF16,)),
    ("gates", 2 * D_MODEL, (F32,)),
    ("small", 128, (F32, BF16)),
)


HEAD_MAJOR_ROWS = (("k_b", F32), ("v_b", F32))
IDX_Q_TILES = ("iq", BF16)
VALUES_T = ("v_b", BF16)
IDX_K_ONLY = ("small", BF16)


def _split_w_in(w):
    o = 0
    parts = {}
    for name, width in (("qkv_a", 3 * DN_WIDTH), ("z", DN_WIDTH), ("b", DN_HEADS), ("a", DN_HEADS),
                        ("q_b", ATT_WIDTH), ("k_b", ATT_WIDTH), ("v_b", ATT_WIDTH),
                        ("iq", IDX_HEADS * IDX_DIM), ("ik", IDX_DIM), ("iw", IDX_HEADS), ("gates", 2 * D_MODEL)):
        parts[name] = w[:, o:o + width]
        o += width
    small = jnp.concatenate(
        [parts["ik"], parts["b"], parts["a"], parts["iw"],
         jnp.zeros((w.shape[0], 128 - IDX_DIM - 2 * DN_HEADS - IDX_HEADS), w.dtype)], axis=1)
    parts["small"] = small
    return [parts[name].astype(BF16) for name, _, _ in IN_GROUPS]


def _in_kernel(x_ref, sc_ref, sh_ref, *refs, attn_layouts):
    n = len(IN_GROUPS)
    w_refs, o_refs = refs[:n], refs[n:]
    h = (x_ref[...] * (1.0 + sc_ref[...]) + sh_ref[...]).astype(BF16)
    k = 0
    for (name, width, dtypes), w_ref in zip(IN_GROUPS, w_refs):
        y = _dot(h, w_ref[...])
        for dt in dtypes:
            if (name, dt) in HEAD_MAJOR_ROWS:
                rows = y.shape[0]
                for hd in range(ATT_HEADS):
                    o_refs[k][pl.ds(hd, rows, stride=ATT_HEADS), :] = y[:, hd * ATT_DIM:(hd + 1) * ATT_DIM]
            elif attn_layouts and (name, dt) == IDX_Q_TILES:
                rows = y.shape[0]
                for hd in range(IDX_HEADS):
                    o_refs[k][hd * rows:(hd + 1) * rows, :] = y[:, hd * IDX_DIM:(hd + 1) * IDX_DIM].astype(dt)
            elif attn_layouts and (name, dt) == VALUES_T:
                o_refs[k][...] = y.T.astype(dt)
            elif (name, dt) == IDX_K_ONLY:
                o_refs[k][...] = y[:, SM_IK:SM_IK + IDX_DIM].astype(dt)
            else:
                o_refs[k][...] = y.astype(dt)
            k += 1


def _in_proj(x, sc, sh, w_groups, tm, ck=None):
    n = x.shape[0]
    nb, r, _ = sc.shape
    nt = n // tm
    tiles_per_b = nt // nb
    attn_layouts = ck is not None
    tiles_per_c = ck // tm if attn_layouts else 1
    mod_spec = pl.BlockSpec((None, r, D_MODEL), lambda i: (i // tiles_per_b, 0, 0))
    in_specs = [pl.BlockSpec((tm, D_MODEL), lambda i: (i, 0)), mod_spec, mod_spec]
    for (name, width, _), w in zip(IN_GROUPS, w_groups):
        in_specs.append(pl.BlockSpec((D_MODEL, width), lambda i: (0, 0), pipeline_mode=pl.Buffered(1)))
    out_specs, out_shape = [], []
    for name, width, dtypes in IN_GROUPS:
        for dt in dtypes:
            if (name, dt) in HEAD_MAJOR_ROWS:
                out_specs.append(pl.BlockSpec((tm * ATT_HEADS, ATT_DIM), lambda i: (i, 0)))
                out_shape.append(jax.ShapeDtypeStruct((n * ATT_HEADS, ATT_DIM), dt))
            elif attn_layouts and (name, dt) == IDX_Q_TILES:
                out_specs.append(pl.BlockSpec((tm * IDX_HEADS, IDX_DIM), lambda i: (i, 0)))
                out_shape.append(jax.ShapeDtypeStruct((n * IDX_HEADS, IDX_DIM), dt))
            elif attn_layouts and (name, dt) == VALUES_T:
                out_specs.append(pl.BlockSpec(
                    (None, None, width, tm),
                    lambda i: (i // tiles_per_b, (i % tiles_per_b) // tiles_per_c, 0, i % tiles_per_c)))
                out_shape.append(jax.ShapeDtypeStruct((nb, n // nb // ck, width, ck), dt))
            elif (name, dt) == IDX_K_ONLY:
                out_specs.append(pl.BlockSpec((tm, IDX_DIM), lambda i: (i, 0)))
                out_shape.append(jax.ShapeDtypeStruct((n, IDX_DIM), dt))
            else:
                out_specs.append(pl.BlockSpec((tm, width), lambda i: (i, 0)))
                out_shape.append(jax.ShapeDtypeStruct((n, width), dt))
    outs = pl.pallas_call(
        functools.partial(_in_kernel, attn_layouts=attn_layouts),
        grid=(nt,),
        in_specs=in_specs,
        out_specs=out_specs,
        out_shape=out_shape,
        compiler_params=_cparams(("parallel",)),
        name="in_proj",
    )(x, sc, sh, *w_groups)
    names = []
    for name, _, dtypes in IN_GROUPS:
        for dt in dtypes:
            names.append(name if dt == dtypes[0] else name + "16")
    return dict(zip(names, outs))


def _delta_kernel(qkv_ref, z_ref, sm_ref, cw_ref, alog_ref, dt_ref, ng_ref, o_ref, s_out_ref, xs, s_sc, *, tt):
    i = pl.program_id(1)
    nc = tt // DN_CHUNK
    c = DN_CHUNK

    @pl.when(i == 0)
    def _():
        xs[0:8, :] = jnp.zeros((8, 3 * DN_WIDTH), F32)
        s_sc[...] = jnp.zeros_like(s_sc)

    xs[8:8 + tt, :] = qkv_ref[...]
    cw = cw_ref[...]
    y = xs[8:8 + tt, :] * cw[3:4, :]
    for j in range(1, CONV_W):
        y = y + xs[8 - j:8 - j + tt, :] * cw[3 - j:4 - j, :]
    act = _silu(y)
    xs[0:8, :] = xs[tt:tt + 8, :]

    sm = sm_ref[...]
    beta_all = _sigmoid(sm)
    g_all = -jnp.exp(alog_ref[...]) * _softplus(sm + dt_ref[...])
    r_io = lax.broadcasted_iota(I32, (tt, tt), 0)
    c_io = lax.broadcasted_iota(I32, (tt, tt), 1)
    lblk = jnp.where((r_io // c == c_io // c) & (r_io >= c_io), 1.0, 0.0).astype(F32)
    gcum = _dot(lblk, g_all, HI)
    gcum_t = gcum.T

    ri = lax.broadcasted_iota(I32, (c, c), 0)
    ci = lax.broadcasted_iota(I32, (c, c), 1)
    incl = ri >= ci
    strict = ri > ci
    eye = jnp.where(ri == ci, 1.0, 0.0).astype(F32)
    ng = ng_ref[...]

    blocks = [(h, cc) for h in range(DN_HEADS) for cc in range(nc)]
    qs, ks, gis, egs, dmats, qks, nms, rhss = {}, {}, {}, {}, {}, {}, {}, {}
    for h in range(DN_HEADS):
        lo, hi = h * DN_DIM, (h + 1) * DN_DIM
        q_h = act[:, lo:hi]
        k_h = act[:, DN_WIDTH + lo:DN_WIDTH + hi]
        v_h = act[:, 2 * DN_WIDTH + lo:2 * DN_WIDTH + hi]
        q_h = q_h * lax.rsqrt(jnp.sum(q_h * q_h, axis=-1, keepdims=True) + 1e-6) * (DN_DIM ** -0.5)
        k_h = k_h * lax.rsqrt(jnp.sum(k_h * k_h, axis=-1, keepdims=True) + 1e-6)
        for cc in range(nc):
            r0, r1 = cc * c, (cc + 1) * c
            qc, kc, vc = q_h[r0:r1], k_h[r0:r1], v_h[r0:r1]
            gi = gcum[r0:r1, SM_A + h:SM_A + h + 1]
            gj = gcum_t[SM_A + h:SM_A + h + 1, r0:r1]
            bi = beta_all[r0:r1, SM_B + h:SM_B + h + 1]
            dmat = jnp.exp(jnp.where(incl, gi - gj, -jnp.inf))
            eg = jnp.exp(gi)
            key = (h, cc)
            qs[key], ks[key], gis[key], egs[key], dmats[key] = qc, kc, gi, eg, dmat
            qks[key] = _dot_nt(qc, kc) * dmat
            nms[key] = jnp.where(strict, -(bi * _dot_nt(kc, kc) * dmat), 0.0)
            rhss[key] = jnp.concatenate([vc * bi, kc * (bi * eg)], axis=1)

    tms = {key: eye + nms[key] for key in blocks}
    ps = {key: _split(nms[key]) for key in blocks}
    for _ in range(5):
        for key in blocks:
            p2 = _dot3(ps[key], ps[key])
            ps[key] = _split(p2)
            tms[key] = tms[key] + _dot3(_split(tms[key]), ps[key])
    sols = {key: _dot3(_split(tms[key]), _split(rhss[key])) for key in blocks}

    states = [s_sc[h] for h in range(DN_HEADS)]
    for cc in range(nc):
        r0, r1 = cc * c, (cc + 1) * c
        for h in range(DN_HEADS):
            lo, hi = h * DN_DIM, (h + 1) * DN_DIM
            key = (h, cc)
            s_h = states[h]
            gi, eg = gis[key], egs[key]
            u, w = sols[key][:, :DN_DIM], sols[key][:, DN_DIM:]
            v_new = u - _dot(w, s_h)
            o = _dot(qs[key] * eg, s_h) + _dot(qks[key], v_new)
            g_last = gi[c - 1:c, :]
            states[h] = s_h * jnp.exp(g_last) + _dot_tn(ks[key] * jnp.exp(g_last - gi), v_new)
            o = o * lax.rsqrt(jnp.mean(o * o, axis=-1, keepdims=True) + EPS) * ng
            o_ref[r0:r1, lo:hi] = o * _silu(z_ref[r0:r1, lo:hi])
    for h in range(DN_HEADS):
        s_sc[h] = states[h]

    @pl.when(i == pl.num_programs(1) - 1)
    def _():
        s_out_ref[...] = s_sc[...]


def _delta_prompt(qkv, z, small, conv_w, alog_row, dt_row, norm_g, nb, tt=256):
    n = qkv.shape[0]
    nt = n // nb // tt
    row = lambda b, i: (b * nt + i, 0)
    const = lambda b, i: (0, 0)
    return pl.pallas_call(
        functools.partial(_delta_kernel, tt=tt),
        grid=(nb, nt),
        in_specs=[
            pl.BlockSpec((tt, 3 * DN_WIDTH), row),
            pl.BlockSpec((tt, DN_WIDTH), row),
            pl.BlockSpec((tt, 128), row),
            pl.BlockSpec((CONV_W, 3 * DN_WIDTH), const),
            pl.BlockSpec((1, 128), const),
            pl.BlockSpec((1, 128), const),
            pl.BlockSpec((1, DN_DIM), const),
        ],
        out_specs=[
            pl.BlockSpec((tt, DN_WIDTH), row),
            pl.BlockSpec((None, DN_HEADS, DN_DIM, DN_DIM), lambda b, i: (b, 0, 0, 0)),
        ],
        out_shape=[
            jax.ShapeDtypeStruct((n, DN_WIDTH), F32),
            jax.ShapeDtypeStruct((nb, DN_HEADS, DN_DIM, DN_DIM), F32),
        ],
        scratch_shapes=[
            pltpu.VMEM((tt + 8, 3 * DN_WIDTH), F32),
            pltpu.VMEM((DN_HEADS, DN_DIM, DN_DIM), F32),
        ],
        compiler_params=_cparams(("arbitrary", "arbitrary")),
        name="delta_prompt",
    )(qkv, z, small, conv_w, alog_row, dt_row, norm_g)


def _sdelta_kernel(qkv_ref, buf_ref, z_ref, sm_ref, cw_ref, alog_ref, dt_ref, ng_ref, s_ref, o_ref, s_out_ref, *, rows):
    cw = cw_ref[...]
    y = qkv_ref[...] * cw[3:4, :]
    for j in range(CONV_W - 1):
        y = y + buf_ref[:, j, :] * cw[j:j + 1, :]
    act = _silu(y)
    sm = sm_ref[...]
    beta_all = _sigmoid(sm)
    g_all = -jnp.exp(alog_ref[...]) * _softplus(sm + dt_ref[...])
    ng = ng_ref[...]
    sub = lax.broadcasted_iota(I32, (8, DN_DIM), 0)
    for h in range(DN_HEADS):
        lo, hi = h * DN_DIM, (h + 1) * DN_DIM
        q_h = act[:, lo:hi]
        k_h = act[:, DN_WIDTH + lo:DN_WIDTH + hi]
        v_h = act[:, 2 * DN_WIDTH + lo:2 * DN_WIDTH + hi]
        q_h = q_h * lax.rsqrt(jnp.sum(q_h * q_h, axis=-1, keepdims=True) + 1e-6) * (DN_DIM ** -0.5)
        k_h = k_h * lax.rsqrt(jnp.sum(k_h * k_h, axis=-1, keepdims=True) + 1e-6)
        g = g_all[:, SM_A + h:SM_A + h + 1]
        b = beta_all[:, SM_B + h:SM_B + h + 1]
        eg = jnp.exp(g)
        u = v_h * b
        w = k_h * (b * eg)
        qe = q_h * eg
        qk = jnp.sum(q_h * k_h, axis=-1, keepdims=True)
        z_h = z_ref[:, lo:hi]
        for r in range(rows):
            s = s_ref[r, h]
            w8 = jnp.where(sub == 0, w[r:r + 1], 0.0)
            q8 = jnp.where(sub == 0, qe[r:r + 1], 0.0)
            k8 = jnp.where(sub == 0, k_h[r:r + 1], 0.0)
            v_new = u[r:r + 1] - _dot(w8, s, HI)[0:1]
            o = _dot(q8, s, HI)[0:1] + qk[r:r + 1] * v_new
            vn8 = jnp.where(sub == 0, v_new, 0.0)
            s_out_ref[r, h] = s * eg[r:r + 1] + _dot_tn(k8, vn8, HI)
            o = o * lax.rsqrt(jnp.mean(o * o, axis=-1, keepdims=True) + EPS) * ng
            o_ref[r:r + 1, lo:hi] = o * _silu(z_h[r:r + 1])


def _delta_sample(qkv, buf, z, small, conv_w, alog_row, dt_row, norm_g, layer, state, rows=8):
    n = qkv.shape[0]
    row = lambda i: (i, 0)
    const = lambda i: (0, 0)
    return pl.pallas_call(
        functools.partial(_sdelta_kernel, rows=rows),
        grid=(n // rows,),
        in_specs=[
            pl.BlockSpec((rows, 3 * DN_WIDTH), row),
            pl.BlockSpec((rows, CONV_W - 1, 3 * DN_WIDTH), lambda i: (i, 0, 0)),
            pl.BlockSpec((rows, DN_WIDTH), row),
            pl.BlockSpec((rows, 128), row),
            pl.BlockSpec((CONV_W, 3 * DN_WIDTH), const),
            pl.BlockSpec((1, 128), const),
            pl.BlockSpec((1, 128), const),
            pl.BlockSpec((1, DN_DIM), const),
            pl.BlockSpec((None, rows, DN_HEADS, DN_DIM, DN_DIM), lambda i: (layer, i, 0, 0, 0)),
        ],
        out_specs=[
            pl.BlockSpec((rows, DN_WIDTH), row),
            pl.BlockSpec((rows, DN_HEADS, DN_DIM, DN_DIM), lambda i: (i, 0, 0, 0)),
        ],
        out_shape=[
            jax.ShapeDtypeStruct((n, DN_WIDTH), F32),
            jax.ShapeDtypeStruct(state.shape[1:], F32),
        ],
        compiler_params=_cparams(("parallel",)),
        name="delta_sample",
    )(qkv, buf, z, small, conv_w, alog_row, dt_row, norm_g, state)


def _score_keys(s):
    s = jnp.where(s == 0.0, 0.0, s)
    k = pltpu.bitcast(s, I32)
    return jnp.where(k < 0, k ^ jnp.int32(0x7FFFFFFF), k)


SEARCH_ROUND = 4


def _alibi_slope(h):
    return 2.0 ** (-8.0 * (h + 1) / ATT_HEADS)


def _fold_groups(x, reduce_fn, ways=8):
    g = x.shape[0]
    if g % ways or g <= ways:
        return reduce_fn(x, axis=0)
    return reduce_fn(reduce_fn(x.reshape(g // ways, ways, 8, x.shape[-1]), axis=0), axis=0)


def _sublane_allreduce(x, op):
    for shift in (4, 2, 1):
        x = op(x, pltpu.roll(x, shift, axis=0))
    return x


def _attn_kernel(q_ref, iq_ref, sm_ref, k_ref, vt_ref, ik_ref, o_ref, keys, alibi, acc_sc, *, tq, ck):
    i = pl.program_id(1)
    nk = (i * tq + tq + ck - 1) // ck
    g8 = ck // 8
    t = i * tq + lax.broadcasted_iota(I32, (8, tq), 1)
    sub = lax.broadcasted_iota(I32, (ck, 1), 0)
    sub3 = lax.broadcasted_iota(I32, (g8, 8, 1), 0) * 8 + lax.broadcasted_iota(I32, (g8, 8, 1), 1)

    @pl.when((pl.program_id(0) == 0) & (i == 0))
    def _():
        for h in range(ATT_HEADS):
            alibi[h] = jnp.broadcast_to(sub.astype(F32) * (_alibi_slope(h) * LOG2E), (ck, tq))

    sm_t = sm_ref[...].T
    iw = [jnp.broadcast_to(sm_t[SM_IW + h:SM_IW + h + 1, :] * (IDX_HEADS ** -0.5 * IDX_DIM ** -0.5), (8, tq))
          for h in range(IDX_HEADS)]
    iq = iq_ref[...]

    def score_chunk(c, diagonal):
        off = pl.multiple_of(c * ck, ck)
        d = _dot_nt(ik_ref[pl.ds(off, ck), :], iq)
        s = jnp.maximum(d[:, 0:tq].reshape(g8, 8, tq), 0.0) * iw[0][None]
        for h in range(1, IDX_HEADS):
            s = s + jnp.maximum(d[:, h * tq:(h + 1) * tq].reshape(g8, 8, tq), 0.0) * iw[h][None]
        if diagonal:
            s = jnp.where(off + sub3 <= t[None], s, -jnp.inf)
        keys[c] = _score_keys(s)

    def score_full(c, carry):
        score_chunk(c, False)
        return carry

    lax.fori_loop(0, nk - 1, score_full, 0)
    score_chunk(nk - 1, True)

    kth = jnp.minimum(t + 1, TOPK)

    def count(pred_fn):
        def body(c, cnt):
            return cnt + _fold_groups(jnp.where(pred_fn(keys[c], c), 1, 0).astype(I32), jnp.sum)
        return _sublane_allreduce(lax.fori_loop(0, nk, body, jnp.zeros((8, tq), I32)), jnp.add)

    def search_cond(st):
        return (st[0] < 32) & (st[3] > 0)

    def search_round(st):
        it, thr, cnt_thr, _ = st
        for u in range(SEARCH_ROUND):
            cand = thr ^ jnp.left_shift(jnp.int32(1), 31 - u - it)
            cnt = count(lambda kc, c: kc >= cand[None])
            ok = cnt >= kth
            thr = jnp.where(ok, cand, thr)
            cnt_thr = jnp.where(ok, cnt, cnt_thr)
        return it + SEARCH_ROUND, thr, cnt_thr, jnp.max(jnp.where(cnt_thr != kth, 1, 0))

    _, thr, _, unresolved = lax.while_loop(
        search_cond, search_round,
        (jnp.int32(0), jnp.full((8, tq), -2 ** 31, I32), jnp.full((8, tq), 2 ** 30, I32), jnp.int32(1)))

    @pl.when(unresolved > 0)
    def _():
        need = kth - count(lambda kc, c: kc > thr[None])

        def pos_step(it, cut):
            cand = cut | jnp.left_shift(jnp.int32(1), 13 - it)
            cnt = count(lambda kc, c: (kc == thr[None]) & (c * ck + sub3 < cand[None]))
            return jnp.where(cnt < need, cand, cut)
        cut = lax.fori_loop(0, 14, pos_step, jnp.zeros((8, tq), I32))

        def drop(c, carry):
            kc = keys[c]
            keys[c] = jnp.where((kc == thr[None]) & (c * ck + sub3 > cut[None]), jnp.int32(-2 ** 31), kc)
            return carry
        lax.fori_loop(0, nk, drop, 0)

    acc_sc[...] = jnp.zeros_like(acc_sc)
    q = q_ref[...]
    qh = [q[:, h * ATT_DIM:(h + 1) * ATT_DIM] for h in range(ATT_HEADS)]
    heads = range(ATT_HEADS)

    def attend_chunk(c, carry):
        ms, ls = carry
        off = pl.multiple_of(c * ck, ck)
        selb = jnp.where(keys[c] >= thr[None], 0.0, NEG).reshape(ck, tq)
        kc = k_ref[pl.ds(off, ck), :]
        off_f = off.astype(F32)
        c_off = [off_f * (_alibi_slope(h) * LOG2E) for h in heads]
        zs = [(_dot_nt(kc[:, h * ATT_DIM:(h + 1) * ATT_DIM], qh[h]) * (ATT_DIM ** -0.5 * LOG2E)
               + (alibi[h] + selb)).reshape(g8, 8, tq) for h in heads]
        m_new = [jnp.maximum(ms[h], _sublane_allreduce(_fold_groups(zs[h], jnp.max), jnp.maximum) + c_off[h])
                 for h in heads]
        a = [jnp.exp2(ms[h] - m_new[h]) for h in heads]
        ps = [jnp.exp2(zs[h] - (m_new[h] - c_off[h])[None]) for h in heads]
        l_new = [a[h] * ls[h] + _sublane_allreduce(_fold_groups(ps[h], jnp.sum), jnp.add) for h in heads]
        for h in heads:
            pv = _dot(vt_ref[c, h * ATT_DIM:(h + 1) * ATT_DIM, :], ps[h].reshape(ck, tq).astype(BF16))
            acc_sc[h] = a[h][None] * acc_sc[h] + pv.reshape(ATT_DIM // 8, 8, tq)
        return tuple(m_new), tuple(l_new)

    init = (tuple(jnp.full((8, tq), NEG, F32) for _ in heads), tuple(jnp.zeros((8, tq), F32) for _ in heads))
    _, ls = lax.fori_loop(0, nk, attend_chunk, init)
    for h in heads:
        o_ref[:, h * ATT_DIM:(h + 1) * ATT_DIM] = (acc_sc[h] / ls[h][None]).reshape(ATT_DIM, tq).T


def _attn_prompt(q16, iq16, small, k16, vt16, ik16, nb, tq=128, ck=512):
    n = q16.shape[0]
    s = n // nb
    nq = s // tq
    row = lambda b, i: (b * nq + i, 0)
    full = lambda b, i: (b, 0)
    once = pl.Buffered(1)
    return pl.pallas_call(
        functools.partial(_attn_kernel, tq=tq, ck=ck),
        grid=(nb, nq),
        in_specs=[
            pl.BlockSpec((tq, ATT_WIDTH), row),
            pl.BlockSpec((IDX_HEADS * tq, IDX_DIM), row),
            pl.BlockSpec((tq, 128), row),
            pl.BlockSpec((s, ATT_WIDTH), full, pipeline_mode=once),
            pl.BlockSpec((None, s // ck, ATT_WIDTH, ck), lambda b, i: (b, 0, 0, 0), pipeline_mode=once),
            pl.BlockSpec((s, IDX_DIM), full, pipeline_mode=once),
        ],
        out_specs=pl.BlockSpec((tq, ATT_WIDTH), row),
        out_shape=jax.ShapeDtypeStruct((n, ATT_WIDTH), F32),
        scratch_shapes=[
            pltpu.VMEM((s // ck, ck // 8, 8, tq), I32),
            pltpu.VMEM((ATT_HEADS, ck, tq), F32),
            pltpu.VMEM((ATT_HEADS, ATT_DIM // 8, 8, tq), F32),
        ],
        compiler_params=_cparams(("arbitrary", "arbitrary")),
        name="attn_prompt",
    )(q16, iq16, small, k16, vt16, ik16)


ATTEND_PAGES = 16
SCORE_PAGES = 32


def _sscore_kernel(pt_ref, iq_ref, iw_ref, ikn_ref, *refs, pgs):
    pages, o_ref = refs[:pgs], refs[pgs]
    j = pl.program_id(1)
    iq = iq_ref[...]
    iw = iw_ref[...] * (IDX_HEADS ** -0.5)

    @pl.when(j < pl.num_programs(1) - 1)
    def _():
        for u in range(pgs):
            d = _dot(iq, pages[u][...]) * (IDX_DIM ** -0.5)
            o_ref[:, u * PAGE:(u + 1) * PAGE] = jnp.sum(jnp.maximum(d, 0.0) * iw, axis=0, keepdims=True)

    @pl.when(j == pl.num_programs(1) - 1)
    def _():
        d = _dot_nt(iq, jnp.broadcast_to(ikn_ref[...], (8, IDX_DIM)))[:, 0:1] * (IDX_DIM ** -0.5)
        s_new = jnp.sum(jnp.maximum(d, 0.0) * iw, axis=0, keepdims=True)
        lane = lax.broadcasted_iota(I32, (1, pgs * PAGE), 1)
        o_ref[...] = jnp.where(lane == 0, s_new, -jnp.inf)


def _sample_scores(layer, page_table, iq3, iw3, ik_new3, cache_idx_t):
    nbatch, n_pages = page_table.shape
    pgs = min(SCORE_PAGES, n_pages)
    nj = n_pages // pgs

    def page_map(u):
        return lambda b, j, pt: (layer, pt[b, jnp.minimum(j * pgs + u, n_pages - 1)], 0, 0)

    grid_spec = pltpu.PrefetchScalarGridSpec(
        num_scalar_prefetch=1,
        grid=(nbatch, nj + 1),
        in_specs=[
            pl.BlockSpec((None, IDX_HEADS, IDX_DIM), lambda b, j, pt: (b, 0, 0)),
            pl.BlockSpec((None, IDX_HEADS, 1), lambda b, j, pt: (b, 0, 0)),
            pl.BlockSpec((None, 1, IDX_DIM), lambda b, j, pt: (b, 0, 0)),
        ] + [pl.BlockSpec((None, None, IDX_DIM, PAGE), page_map(u)) for u in range(pgs)],
        out_specs=pl.BlockSpec((None, 1, pgs * PAGE), lambda b, j, pt: (b, 0, j)),
    )
    return pl.pallas_call(
        functools.partial(_sscore_kernel, pgs=pgs),
        grid_spec=grid_spec,
        out_shape=jax.ShapeDtypeStruct((nbatch, 1, (nj + 1) * pgs * PAGE), F32),
        compiler_params=_cparams(("arbitrary", "arbitrary")),
        name="sample_scores",
    )(page_table, iq3, iw3, ik_new3, *([cache_idx_t] * pgs))


def _sselect_kernel(s_ref, o_ref):
    keys = _score_keys(s_ref[...])
    rows, width = keys.shape
    pos = lax.broadcasted_iota(I32, (1, width), 1)

    def cnt(m):
        return jnp.sum(jnp.where(m, 1, 0).astype(I32), axis=-1, keepdims=True)

    def search_step(it, thr):
        cand = thr ^ jnp.left_shift(jnp.int32(1), 31 - it)
        return jnp.where(cnt(keys >= cand) >= TOPK, cand, thr)

    thr = lax.fori_loop(0, 32, search_step, jnp.full((rows, 1), -2 ** 31, I32))
    need = TOPK - cnt(keys > thr)
    eq = keys == thr

    def pos_step(it, cut):
        cand = cut | jnp.left_shift(jnp.int32(1), 14 - it)
        return jnp.where(cnt(eq & (pos < cand)) < need, cand, cut)

    cut = lax.fori_loop(0, 15, pos_step, jnp.zeros((rows, 1), I32))
    sel = (keys > thr) | (eq & (pos <= cut))
    o_ref[...] = jnp.where(sel, 0.0, NEG)


def _sample_select(scores):
    return pl.pallas_call(
        _sselect_kernel,
        out_shape=jax.ShapeDtypeStruct(scores.shape, F32),
        compiler_params=pltpu.CompilerParams(vmem_limit_bytes=VMEM_LIMIT),
        name="sample_select",
    )(scores)


PAGE_ROWS = PAGE * ATT_HEADS


def _sattn_kernel(pt_ref, q_ref, kn_ref, vn_ref, selb_ref, seln_ref, *refs, past, pg):
    kp, vp = refs[:pg], refs[pg:2 * pg]
    o_ref, lg_sc, l_sc, pn_sc, acc_sc = refs[2 * pg:2 * pg + 5]
    ph = pl.program_id(1)
    j = pl.program_id(2)
    scale = ATT_DIM ** -0.5
    q8 = q_ref[...]
    row = lax.broadcasted_iota(I32, (8, 1), 0)
    lane = lax.broadcasted_iota(I32, (1, PAGE_ROWS), 1)
    slope = jnp.exp2(-8.0 * (row + 1).astype(F32) / ATT_HEADS)
    own = jnp.bitwise_and(lane, ATT_HEADS - 1) == row
    tok = jnp.right_shift(lane, 2)

    @pl.when(ph == 0)
    def _():
        for u in range(pg):
            dist = (past - ((j * pg + u) * PAGE + tok)).astype(F32)
            lg = (_dot_nt(q8.astype(BF16), kp[u][...].astype(BF16)) * scale - slope * dist
                  + selb_ref[:, u * PAGE_ROWS:(u + 1) * PAGE_ROWS])
            lg_sc[j * pg + u] = jnp.where(own, lg, NEG)

    @pl.when((ph == 1) & (j == 0))
    def _():
        allg = lg_sc[...]
        lgn = jnp.sum(q8 * kn_ref[...], axis=-1, keepdims=True) * scale + seln_ref[:, 0:1]
        m = jnp.maximum(jnp.max(jnp.max(allg, axis=0), axis=-1, keepdims=True), lgn)
        p = jnp.exp(allg - m)
        lg_sc[...] = p
        pn = jnp.exp(lgn - m)
        l_sc[...] = jnp.sum(jnp.sum(p, axis=0), axis=-1, keepdims=True) + pn
        pn_sc[...] = pn
        acc_sc[...] = jnp.zeros_like(acc_sc)

    @pl.when(ph == 1)
    def _():
        acc = acc_sc[...]
        for u in range(pg):
            acc = acc + _dot(lg_sc[j * pg + u].astype(BF16), vp[u][...].astype(BF16))
        acc_sc[...] = acc

    @pl.when((ph == 1) & (j == pl.num_programs(2) - 1))
    def _():
        o_ref[...] = (acc_sc[...] + pn_sc[...] * vn_ref[...]) / l_sc[...]


def _sample_attend(layer, page_table, q8, kn8, vn8, selb4, selb, cache_k4, cache_v4):
    nbatch, n_pages = page_table.shape
    pg = min(ATTEND_PAGES, n_pages)
    nj = n_pages // pg
    past = n_pages * PAGE

    def k_map(u):
        return lambda b, ph, j, pt: (layer, pt[b, jnp.where(ph == 0, j, nj - 1) * pg + u], 0, 0)

    def v_map(u):
        return lambda b, ph, j, pt: (layer, pt[b, jnp.where(ph == 0, 0, j) * pg + u], 0, 0)

    vec = pl.BlockSpec((None, 8, ATT_DIM), lambda b, ph, j, pt: (b, 0, 0))
    grid_spec = pltpu.PrefetchScalarGridSpec(
        num_scalar_prefetch=1,
        grid=(nbatch, 2, nj),
        in_specs=[
            vec, vec, vec,
            pl.BlockSpec((None, 1, pg * PAGE_ROWS), lambda b, ph, j, pt: (b, 0, jnp.where(ph == 0, j, nj - 1))),
            pl.BlockSpec((None, 1, 128), lambda b, ph, j, pt: (b, 0, past // 128)),
        ] + [pl.BlockSpec((None, None, PAGE_ROWS, ATT_DIM), k_map(u)) for u in range(pg)]
          + [pl.BlockSpec((None, None, PAGE_ROWS, ATT_DIM), v_map(u)) for u in range(pg)],
        out_specs=vec,
        scratch_shapes=[
            pltpu.VMEM((n_pages, 8, PAGE_ROWS), F32),
            pltpu.VMEM((8, 1), F32),
            pltpu.VMEM((8, 1), F32),
            pltpu.VMEM((8, ATT_DIM), F32),
        ],
    )
    return pl.pallas_call(
        functools.partial(_sattn_kernel, past=past, pg=pg),
        grid_spec=grid_spec,
        out_shape=jax.ShapeDtypeStruct((nbatch, 8, ATT_DIM), F32),
        compiler_params=_cparams(("arbitrary", "arbitrary", "arbitrary")),
        name="sample_attend",
    )(page_table, q8, kn8, vn8, selb4, selb, *([cache_k4] * pg), *([cache_v4] * pg))


def _mix_kernel(oa_ref, ob_ref, gates_ref, x_ref, gm_ref, wa_ref, wb_ref, wo_ref, g_ref, b_ref, o_ref):
    br_a = _dot(oa_ref[...].astype(BF16), wa_ref[...])
    br_b = _dot(ob_ref[...].astype(BF16), wb_ref[...])
    merged = _sigmoid(gates_ref[:, :D_MODEL]) * br_a + _sigmoid(gates_ref[:, D_MODEL:]) * br_b
    y = _dot(merged.astype(BF16), wo_ref[...])
    r = ALPHA * x_ref[...] + gm_ref[...] * y
    o_ref[...] = _layernorm(r, g_ref[...], b_ref[...])


def _mix(oa, ob, gates, x, gm, wa, wb, wo, ln_g, ln_b, tm):
    n = x.shape[0]
    nb, r, _ = gm.shape
    nt = n // tm
    tiles_per_b = nt // nb
    row = lambda i: (i, 0)
    const = lambda i: (0, 0)
    return pl.pallas_call(
        _mix_kernel,
        grid=(nt,),
        in_specs=[
            pl.BlockSpec((tm, DN_WIDTH), row),
            pl.BlockSpec((tm, ATT_WIDTH), row),
            pl.BlockSpec((tm, 2 * D_MODEL), row),
            pl.BlockSpec((tm, D_MODEL), row),
            pl.BlockSpec((None, r, D_MODEL), lambda i: (i // tiles_per_b, 0, 0)),
            pl.BlockSpec((DN_WIDTH, D_MODEL), const),
            pl.BlockSpec((ATT_WIDTH, D_MODEL), const),
            pl.BlockSpec((D_MODEL, D_MODEL), const),
            pl.BlockSpec((1, D_MODEL), const),
            pl.BlockSpec((1, D_MODEL), const),
        ],
        out_specs=pl.BlockSpec((tm, D_MODEL), row),
        out_shape=jax.ShapeDtypeStruct((n, D_MODEL), F32),
        compiler_params=_cparams(("parallel",)),
        name="mix_out",
    )(oa, ob, gates, x, gm, wa, wb, wo, ln_g, ln_b)


def _route_kernel(x_ref, sc_ref, sh_ref, wr_ref, o_ref, e_ref, g_ref):
    h = x_ref[...] * (1.0 + sc_ref[...]) + sh_ref[...]
    o_ref[...] = h.astype(o_ref.dtype)
    logits = _dot(h, wr_ref[...], HI)
    lane = lax.broadcasted_iota(I32, logits.shape, 1)
    logits = jnp.where(lane < N_EXPERTS, logits, -jnp.inf)
    e = jnp.exp(logits - jnp.max(logits, axis=-1, keepdims=True))
    p = e / jnp.sum(e, axis=-1, keepdims=True)
    p1 = jnp.max(p, axis=-1, keepdims=True)
    e1 = jnp.min(jnp.where(p == p1, lane, 128), axis=-1, keepdims=True)
    rest = jnp.where((lane == e1) | (lane >= N_EXPERTS), -1.0, p)
    p2 = jnp.max(rest, axis=-1, keepdims=True)
    e2 = jnp.min(jnp.where(rest == p2, lane, 128), axis=-1, keepdims=True)
    tot = p1 + p2
    e_ref[...] = jnp.concatenate([e1, e2], axis=1)
    g_ref[...] = jnp.concatenate([p1 / tot, p2 / tot], axis=1)


def _ffn_route(x, sc, sh, tm, w_router):
    n = x.shape[0]
    nb, r, _ = sc.shape
    nt = n // tm
    tiles_per_b = nt // nb
    row = lambda i: (i, 0)
    mod_spec = pl.BlockSpec((None, r, D_MODEL), lambda i: (i // tiles_per_b, 0, 0))
    in_specs = [pl.BlockSpec((tm, D_MODEL), row), mod_spec, mod_spec]
    wr = jnp.pad(w_router, ((0, 0), (0, 128 - N_EXPERTS)))
    return pl.pallas_call(
        _route_kernel, grid=(nt,),
        in_specs=in_specs + [pl.BlockSpec((D_MODEL, 128), lambda i: (0, 0))],
        out_specs=[pl.BlockSpec((tm, D_MODEL), row), pl.BlockSpec((tm, 2), row), pl.BlockSpec((tm, 2), row)],
        out_shape=[jax.ShapeDtypeStruct((n, D_MODEL), F32), jax.ShapeDtypeStruct((n, 2), I32),
                   jax.ShapeDtypeStruct((n, 2), F32)],
        compiler_params=_cparams(("parallel",)), name="ffn_route",
    )(x, sc, sh, wr)


def _gffn_kernel(te_ref, nu_ref, x_ref, *refs, modulate, substitute):
    if modulate:
        sc_ref, sh_ref = refs[:2]
        refs = refs[2:]
    if substitute:
        src_ref, late_ref = refs[:2]
        refs = refs[2:]
    wg_ref, wu_ref, wd_ref, o_ref, acc, x16 = refs
    t = pl.program_id(0)
    f = pl.program_id(1)

    @pl.when((t < nu_ref[0]) & (f == 0))
    def _():
        x = x_ref[...]
        if modulate:
            x = x * (1.0 + sc_ref[...]) + sh_ref[...]
        x = x.astype(BF16)
        if substitute:
            src = src_ref[...]
            late = late_ref[...]
            pick = jnp.where(src == lax.broadcasted_iota(I32, (src.shape[0], late.shape[0]), 1), 1.0, 0.0)
            x = jnp.where(src >= 0, _dot(pick.astype(BF16), late).astype(BF16), x)
        x16[...] = x

    @pl.when(t < nu_ref[0])
    def _():
        x = x16[...]
        a = _dot(x, wg_ref[...].astype(BF16))
        b = _dot(x, wu_ref[...].astype(BF16))
        y = _dot((_silu(a) * b).astype(BF16), wd_ref[...].astype(BF16))

        @pl.when(f == 0)
        def _():
            acc[...] = y

        @pl.when(f > 0)
        def _():
            acc[...] += y

    @pl.when(f == pl.num_programs(1) - 1)
    def _():
        o_ref[...] = jnp.where(t < nu_ref[0], acc[...], 0.0)


def _grouped_ffn(x, tile_expert, n_used, layer, w_gate, w_up, w_down, tm, mod=None, late=None, tf=256):
    n = x.shape[0]
    nt = n // tm
    nf = D_FF // tf
    mod_specs, mod_args = [], []
    if mod is not None:
        nb, r, _ = mod[0].shape
        tiles_per_b = nt // nb
        mod_specs = [pl.BlockSpec((None, r, D_MODEL), lambda t, f, te, nu: (t // tiles_per_b, 0, 0))] * 2
        mod_args = list(mod)
    if late is not None:
        mod_specs = mod_specs + [pl.BlockSpec((tm, 1), lambda t, f, te, nu: (t, 0)),
                                 pl.BlockSpec(late[1].shape, lambda t, f, te, nu: (0, 0))]
        mod_args = mod_args + list(late)
    grid_spec = pltpu.PrefetchScalarGridSpec(
        num_scalar_prefetch=2,
        grid=(nt, nf),
        in_specs=[pl.BlockSpec((tm, D_MODEL), lambda t, f, te, nu: (t, 0))] + mod_specs + [
            pl.BlockSpec((None, None, D_MODEL, tf), lambda t, f, te, nu: (layer, te[t], 0, f)),
            pl.BlockSpec((None, None, D_MODEL, tf), lambda t, f, te, nu: (layer, te[t], 0, f)),
            pl.BlockSpec((None, None, tf, D_MODEL), lambda t, f, te, nu: (layer, te[t], f, 0)),
        ],
        out_specs=pl.BlockSpec((tm, D_MODEL), lambda t, f, te, nu: (t, 0)),
        scratch_shapes=[pltpu.VMEM((tm, D_MODEL), F32), pltpu.VMEM((tm, D_MODEL), BF16)],
    )
    return pl.pallas_call(
        functools.partial(_gffn_kernel, modulate=mod is not None, substitute=late is not None),
        grid_spec=grid_spec,
        out_shape=jax.ShapeDtypeStruct((n, D_MODEL), F32),
        compiler_params=_cparams(("arbitrary", "arbitrary")),
        name="grouped_ffn",
    )(tile_expert, n_used, x, *mod_args, w_gate, w_up, w_down)


def _res_ln_kernel(x_ref, y_ref, gf_ref, g_ref, b_ref, o_ref):
    r = ALPHA * x_ref[...] + gf_ref[...] * y_ref[...]
    o_ref[...] = _layernorm(r, g_ref[...], b_ref[...])


def _res_ln2_kernel(x_ref, y0_ref, y1_ref, gt_ref, gf_ref, g_ref, b_ref, o_ref):
    gt = gt_ref[...]
    y = y0_ref[...] * gt[:, 0:1] + y1_ref[...] * gt[:, 1:2]
    r = ALPHA * x_ref[...] + gf_ref[...] * y
    o_ref[...] = _layernorm(r, g_ref[...], b_ref[...])


def _res_ln(x, ys, gate2, gf, ln_g, ln_b, tm):
    n = x.shape[0]
    nb, r, _ = gf.shape
    nt = n // tm
    tiles_per_b = nt // nb
    row = lambda i: (i, 0)
    const = lambda i: (0, 0)
    big = pl.BlockSpec((tm, D_MODEL), row)
    mod_spec = pl.BlockSpec((None, r, D_MODEL), lambda i: (i // tiles_per_b, 0, 0))
    vec = pl.BlockSpec((1, D_MODEL), const)
    if gate2 is None:
        kern, ins, specs = _res_ln_kernel, (x, ys[0], gf, ln_g, ln_b), [big, big, mod_spec, vec, vec]
    else:
        kern = _res_ln2_kernel
        ins = (x, ys[0], ys[1], gate2, gf, ln_g, ln_b)
        specs = [big, big, big, pl.BlockSpec((tm, 2), row), mod_spec, vec, vec]
    return pl.pallas_call(
        kern, grid=(nt,), in_specs=specs, out_specs=big,
        out_shape=jax.ShapeDtypeStruct((n, D_MODEL), F32),
        compiler_params=_cparams(("parallel",)), name="res_ln",
    )(*ins)


def _dense_ffn(x, sc, sh, gf, layer, w_gate, w_up, w_down, ln_g, ln_b, tm_mod, tm_ffn):
    n = x.shape[0]
    nt = n // tm_ffn
    y = _grouped_ffn(x, jnp.zeros((nt,), I32), jnp.full((1,), nt, I32), layer,
                     w_gate[:, None], w_up[:, None], w_down[:, None], tm_ffn, mod=(sc, sh))
    return _res_ln(x, (y,), None, gf, ln_g, ln_b, tm_mod)


def _expert_ranks(top_e):
    flat_e = top_e.reshape(-1)
    onehot = flat_e[:, None] == jnp.arange(N_EXPERTS, dtype=I32)[None, :]
    seen = jnp.cumsum(onehot.astype(I32), axis=0)
    pick = lambda table: jnp.sum(jnp.where(onehot, table, 0), axis=1)
    return flat_e, seen[-1], pick(seen) - 1, pick


def _moe_ffn(main, late, w_router, layer, w_gate, w_up, w_down, ln_g, ln_b, tm_e):
    x_m, x_l = main[0], late[0]
    n_m, n_l = x_m.shape[0], x_l.shape[0]
    h_m, top_m, gate_m = _ffn_route(*main[:3], main[4], w_router)
    flat_m, counts_m, rank_m, pick_m = _expert_ranks(top_m)
    padded = (counts_m + n_l + tm_e - 1) // tm_e * tm_e
    pad_end = jnp.cumsum(padded)
    pad_start = pad_end - padded
    n_tiles = -(-(2 * n_m + N_EXPERTS * n_l) // tm_e) + N_EXPERTS
    tile_expert = jnp.minimum(
        jnp.searchsorted(pad_end, jnp.arange(n_tiles, dtype=I32) * tm_e, side="right"), N_EXPERTS - 1).astype(I32)
    n_used = (pad_end[-1] // tm_e).astype(I32).reshape(1)
    slot_m = (pick_m(pad_start[None, :]) + rank_m).reshape(n_m, 2)
    rows_m = jnp.zeros((n_tiles * tm_e,), I32).at[slot_m.reshape(-1)].set(
        jnp.arange(2 * n_m, dtype=I32) // 2, unique_indices=True)
    xb = h_m[rows_m]

    h_l, top_l, gate_l = _ffn_route(*late[:3], late[4], w_router)
    flat_l, counts_l, rank_l, pick_l = _expert_ranks(top_l)
    slot_l = (pick_l((pad_start + counts_m)[None, :]) + rank_l).reshape(n_l, 2)
    src_l = jnp.full((n_tiles * tm_e,), -1, I32).at[slot_l.reshape(-1)].set(
        jnp.arange(2 * n_l, dtype=I32) // 2, unique_indices=True).reshape(-1, 1)
    yb = _grouped_ffn(xb, tile_expert, n_used, layer, w_gate, w_up, w_down, tm_e, late=(src_l, h_l.astype(BF16)))
    out_m = _res_ln(x_m, (yb[slot_m[:, 0]], yb[slot_m[:, 1]]), gate_m, main[3], ln_g, ln_b, main[4])
    out_l = _res_ln(x_l, (yb[slot_l[:, 0]], yb[slot_l[:, 1]]), gate_l, late[3], ln_g, ln_b, late[4])
    return out_m, out_l


def kernel(x_prompt, x_sample, cache_k, cache_v, cache_idx_k, state_delta, state_conv, page_table, c_prompt, c_sample, w_ada, b_ada, w_in, conv_w, a_log, dt_bias, dn_norm_g, w_branch, w_out, ln1_g, ln1_b, ln2_g, ln2_b, w_ffn_gate, w_ffn_up, w_ffn_down, w_router, w_exp_gate, w_exp_up, w_exp_down):
    bp, seq, _ = x_prompt.shape
    bs = x_sample.shape[0]
    n_p = bp * seq
    n_phys = cache_k.shape[1]
    past = page_table.shape[1] * PAGE

    c_all = jnp.concatenate([c_prompt, c_sample], axis=0)
    c_rows = -(-c_all.shape[0] // 8) * 8
    c_all = jnp.pad(c_all, ((0, c_rows - c_all.shape[0]), (0, 0)))
    mods = _ada_all(c_all, w_ada, b_ada)

    cache_k4 = cache_k.reshape(DEPTH, n_phys, PAGE_ROWS, ATT_DIM)
    cache_v4 = cache_v.reshape(DEPTH, n_phys, PAGE_ROWS, ATT_DIM)
    cache_idx_t = jnp.swapaxes(cache_idx_k, 2, 3)
    ck = 512
    tq = 256

    xp = x_prompt.reshape(n_p, D_MODEL)
    xs = x_sample.reshape(bs, D_MODEL)
    lane = jnp.arange(128)
    outs_p, outs_s = [], []
    for l in range(DEPTH):
        j = l // 2
        mod_p = [mods[l, :bp, u * D_MODEL:(u + 1) * D_MODEL].reshape(bp, 1, D_MODEL) for u in range(6)]
        mod_s = [mods[l, bp:bp + bs, u * D_MODEL:(u + 1) * D_MODEL].reshape(1, bs, D_MODEL) for u in range(6)]
        w_groups = _split_w_in(w_in[l])
        head_lane = jnp.clip(lane - SM_A, 0, DN_HEADS - 1)
        in_a = (lane >= SM_A) & (lane < SM_A + DN_HEADS)
        alog_row = jnp.where(in_a, a_log[l][head_lane], 0.0).reshape(1, 128)
        dt_row = jnp.where(in_a, dt_bias[l][head_lane], 0.0).reshape(1, 128)
        norm_g = dn_norm_g[l].reshape(1, DN_DIM)
        wa16 = w_branch[l, 0].astype(BF16)
        wb16 = w_branch[l, 1].astype(BF16)
        wo16 = w_out[l].astype(BF16)
        ln1 = (ln1_g[l].reshape(1, -1), ln1_b[l].reshape(1, -1))
        ln2 = (ln2_g[l].reshape(1, -1), ln2_b[l].reshape(1, -1))

        pr = _in_proj(xp, mod_p[1], mod_p[0], w_groups, tm=tq, ck=ck)
        oa, s_new_p = _delta_prompt(pr["qkv_a"], pr["z"], pr["small"], conv_w[l], alog_row, dt_row, norm_g, bp)
        ob = _attn_prompt(pr["q_b"], pr["iq"], pr["small"], pr["k_b16"], pr["v_b16"], pr["small16"], bp, tq=tq, ck=ck)
        x1 = _mix(oa, ob, pr["gates"], xp, mod_p[2], wa16, wb16, wo16, *ln1, tm=512)
        conv_p = pr["qkv_a"].reshape(bp, seq, -1)[:, seq - (CONV_W - 1):, :]
        outs_p.append((conv_p, s_new_p, pr["k_b"].reshape(bp, seq, ATT_HEADS, ATT_DIM),
                       pr["v_b"].reshape(bp, seq, ATT_HEADS, ATT_DIM),
                       pr["small"][:, SM_IK:SM_IK + IDX_DIM].reshape(bp, seq, IDX_DIM)))

        sr = _in_proj(xs, mod_s[1], mod_s[0], w_groups, tm=bs)
        oa_s, s_new_s = _delta_sample(sr["qkv_a"], state_conv[l], sr["z"], sr["small"], conv_w[l], alog_row, dt_row,
                                      norm_g, l, state_delta)
        ik_s = sr["small"][:, SM_IK:SM_IK + IDX_DIM]
        iq3 = sr["iq"].astype(F32).reshape(bs, IDX_HEADS, IDX_DIM)
        iw3 = sr["small"][:, SM_IW:SM_IW + IDX_HEADS].reshape(bs, IDX_HEADS, 1)
        scores = _sample_scores(l, page_table, iq3, iw3, ik_s.reshape(bs, 1, IDX_DIM), cache_idx_t)
        selb = _sample_select(scores.reshape(bs, -1)).reshape(scores.shape)
        head_rows = lambda a: jnp.pad(a.astype(F32).reshape(bs, ATT_HEADS, ATT_DIM), ((0, 0), (0, 8 - ATT_HEADS), (0, 0)))
        selb4 = jnp.repeat(selb, ATT_HEADS, axis=-1)
        ob_s = _sample_attend(l, page_table, head_rows(sr["q_b"]), head_rows(sr["k_b"]), head_rows(sr["v_b"]),
                              selb4, selb, cache_k4, cache_v4)[:, :ATT_HEADS, :].reshape(bs, ATT_WIDTH)
        x1s = _mix(oa_s, ob_s, sr["gates"], xs, mod_s[2], wa16, wb16, wo16, *ln1, tm=bs)

        if l % 2 == 0:
            xp = _dense_ffn(x1, mod_p[4], mod_p[3], mod_p[5], j, w_ffn_gate, w_ffn_up, w_ffn_down, *ln2,
                            tm_mod=512, tm_ffn=1024)
            xs = _dense_ffn(x1s, mod_s[4], mod_s[3], mod_s[5], j, w_ffn_gate, w_ffn_up, w_ffn_down, *ln2,
                            tm_mod=bs, tm_ffn=bs)
        else:
            xp, xs = _moe_ffn((x1, mod_p[4], mod_p[3], mod_p[5], 512), (x1s, mod_s[4], mod_s[3], mod_s[5], bs),
                              w_router[j], j, w_exp_gate, w_exp_up, w_exp_down, *ln2, tm_e=1024)
        conv_s =jnp.concatenate([state_conv[l][:, 1:, :], sr["qkv_a"][:, None, :]], axis=1)
        outs_s.append((conv_s, s_new_s, sr["k_b"].reshape(bs, 1, ATT_HEADS, ATT_DIM),
                       sr["v_b"].reshape(bs, 1, ATT_HEADS, ATT_DIM), ik_s.reshape(bs, 1, IDX_DIM)))

    stack = lambda rows, u: jnp.stack([r[u] for r in rows])
    return (xp.reshape(bp, seq, D_MODEL), xs.reshape(bs, 1, D_MODEL),
            stack(outs_p, 2), stack(outs_p, 3), stack(outs_p, 4), stack(outs_p, 1), stack(outs_p, 0),
            stack(outs_s, 2), stack(outs_s, 3), stack(outs_s, 4), stack(outs_s, 1), stack(outs_s, 0))
```
